```python
import math
import jax, jax.numpy as jnp
from jax import lax
import numpy as np

D_MODEL = 2048
BATCH = 8
SEQ = 2048
DEPTH = 1

HEAD_DIM = 128
FOX_HEADS = 8
MOBA_HEADS = 4
MEM_HEADS = 4
FOX_W = FOX_HEADS * HEAD_DIM
MOBA_W = MOBA_HEADS * HEAD_DIM
MEM_W = MEM_HEADS * HEAD_DIM
MIX_W = FOX_W + MOBA_W + MEM_W
IN_W = 3 * FOX_W + FOX_HEADS + 3 * MOBA_W + MEM_W
D_FF = 5632
N_MEM = 256
FOX_Q_BLOCK = 128
MOBA_BLOCK = 256
MOBA_TOPK = 3
MOBA_Q_CHUNK = 32
REL_BUCKETS = 32
REL_MAX_DIST = 128
EPS = 1e-6
NEG = -1e30

kernel_name = "hymba_fox_moba_macaron_layer"


def rmsnorm(x, g):
    xf = x.astype(jnp.float32)
    y = xf * lax.rsqrt(jnp.mean(xf * xf, axis=-1, keepdims=True) + EPS)
    return (y * g.astype(jnp.float32)).astype(x.dtype)


def swiglu(h, w1, w3, w2):
    return (jax.nn.silu(h @ w1) * (h @ w3)) @ w2


def t5_bucket(dist):
    n = jnp.maximum(dist, 0)
    max_exact = REL_BUCKETS // 2
    nf = jnp.maximum(n, 1).astype(jnp.float32)
    large = max_exact + (jnp.log(nf / max_exact) / math.log(REL_MAX_DIST / max_exact)
                         * (REL_BUCKETS - max_exact)).astype(jnp.int32)
    large = jnp.minimum(large, REL_BUCKETS - 1)
    return jnp.where(n < max_exact, n, large)


def fox_attention(q, k, v, logf):
    B, H, S, Dh = q.shape
    c = jnp.cumsum(logf, axis=-1)
    kpos = jnp.arange(S)
    scale = Dh ** -0.5

    def block(i):
        s0 = i * FOX_Q_BLOCK
        qb = lax.dynamic_slice_in_dim(q, s0, FOX_Q_BLOCK, axis=2)
        cb = lax.dynamic_slice_in_dim(c, s0, FOX_Q_BLOCK, axis=2)
        qpos = s0 + jnp.arange(FOX_Q_BLOCK)
        logits = (jnp.einsum('bhqd,bhkd->bhqk', qb, k).astype(jnp.float32) * scale
                  + cb[..., :, None] - c[..., None, :])
        logits = jnp.where(kpos[None, :] <= qpos[:, None], logits, NEG)
        p = jax.nn.softmax(logits, axis=-1).astype(v.dtype)
        return jnp.einsum('bhqk,bhkd->bhqd', p, v)

    out = lax.map(block, jnp.arange(S // FOX_Q_BLOCK))
    return jnp.moveaxis(out, 0, 2).reshape(B, H, S, Dh)


def moba_attention(q, k, v, rel_bias):
    B, H, S, Dh = q.shape
    nb = -(-S // MOBA_BLOCK)
    pad = nb * MOBA_BLOCK - S
    kp = jnp.pad(k, ((0, 0), (0, 0), (0, pad), (0, 0)))
    vp = jnp.pad(v, ((0, 0), (0, 0), (0, pad), (0, 0)))
    kb = kp.reshape(B, H, nb, MOBA_BLOCK, Dh)
    vb = vp.reshape(B, H, nb, MOBA_BLOCK, Dh)
    kmean = jnp.mean(kb.astype(jnp.float32), axis=3)
    topk = min(MOBA_TOPK, nb)
    scale = Dh ** -0.5
    b_i = jnp.arange(B)[:, None, None, None]
    h_i = jnp.arange(H)[None, :, None, None]
    h_i5 = jnp.arange(H)[None, :, None, None, None]
    blk_ids = jnp.arange(nb)
    offs = jnp.arange(MOBA_BLOCK)

    def chunk(i):
        s0 = i * MOBA_Q_CHUNK
        qc = lax.dynamic_slice_in_dim(q, s0, MOBA_Q_CHUNK, axis=2)
        qpos = s0 + jnp.arange(MOBA_Q_CHUNK)
        own = s0 // MOBA_BLOCK
        gate = jnp.einsum('bhqd,bhnd->bhqn', qc.astype(jnp.float32), kmean)
        gate = jnp.where(blk_ids < own, gate, NEG)
        _, idx = lax.top_k(gate, topk)
        sel_valid = idx < own
        ksel = kb[b_i, h_i, idx]
        vsel = vb[b_i, h_i, idx]
        sel_pos = idx[..., None] * MOBA_BLOCK + offs
        l_sel = jnp.einsum('bhqd,bhqnkd->bhqnk', qc, ksel).astype(jnp.float32) * scale
        l_sel = l_sel + rel_bias[t5_bucket(qpos[:, None, None] - sel_pos), h_i5].astype(jnp.float32)
        l_sel = jnp.where(sel_valid[..., None], l_sel, NEG)
        l_sel = l_sel.reshape(B, H, MOBA_Q_CHUNK, topk * MOBA_BLOCK)
        own_start = own * MOBA_BLOCK
        kown = lax.dynamic_slice_in_dim(kp, own_start, MOBA_BLOCK, axis=2)
        vown = lax.dynamic_slice_in_dim(vp, own_start, MOBA_BLOCK, axis=2)
        dist_own = qpos[:, None] - (own_start + offs)[None, :]
        l_own = jnp.einsum('bhqd,bhkd->bhqk', qc, kown).astype(jnp.float32) * scale
        l_own = l_own + jnp.moveaxis(rel_bias[t5_bucket(dist_own)], -1, 0).astype(jnp.float32)
        l_own = jnp.where(dist_own >= 0, l_own, NEG)
        p = jax.nn.softmax(jnp.concatenate([l_sel, l_own], axis=-1), axis=-1).astype(v.dtype)
        p_sel = p[..., :topk * MOBA_BLOCK].reshape(B, H, MOBA_Q_CHUNK, topk, MOBA_BLOCK)
        p_own = p[..., topk * MOBA_BLOCK:]
        return (jnp.einsum('bhqnk,bhqnkd->bhqd', p_sel, vsel)
                + jnp.einsum('bhqk,bhkd->bhqd', p_own, vown))

    out = lax.map(chunk, jnp.arange(S // MOBA_Q_CHUNK))
    return jnp.moveaxis(out, 0, 2).reshape(B, H, S, Dh)


def memory_attention(q, k, v):
    logits = jnp.einsum('bhqd,bhmd->bhqm', q, k).astype(jnp.float32) * (q.shape[-1] ** -0.5)
    p = jax.nn.softmax(logits, axis=-1).astype(v.dtype)
    return jnp.einsum('bhqm,bhmd->bhqd', p, v)


def setup_inputs(seed: int = 0) -> dict:
    key = jax.random.key(seed)
    ks = jax.random.split(key, 24)
    f32 = jnp.float32

    def nrm(k, shape, fan_in):
        return jax.random.normal(k, shape, f32) * (fan_in ** -0.5)

    def gain(k, shape):
        return 1.0 + 0.05 * jax.random.normal(k, shape, f32)

    L = DEPTH
    return {
        "x": jax.random.normal(ks[0], (BATCH, SEQ, D_MODEL), f32),
        "mem": jax.random.normal(ks[1], (BATCH, N_MEM, D_MODEL), f32),
        "ffn1_norm": gain(ks[2], (L, D_MODEL)),
        "ffn1_w1": nrm(ks[3], (L, D_MODEL, D_FF), D_MODEL),
        "ffn1_w3": nrm(ks[4], (L, D_MODEL, D_FF), D_MODEL),
        "ffn1_w2": nrm(ks[5], (L, D_FF, D_MODEL), D_FF),
        "mix_norm": gain(ks[6], (L, D_MODEL)),
        "mem_norm": gain(ks[7], (L, D_MODEL)),
        "w_in": nrm(ks[8], (L, D_MODEL, IN_W), D_MODEL),
        "b_forget": jax.random.uniform(ks[9], (L, FOX_HEADS), f32, 1.0, 3.0),
        "w_mem_kv": nrm(ks[10], (L, D_MODEL, 2 * MEM_W), D_MODEL),
        "fox_q_gain": gain(ks[11], (L, HEAD_DIM)),
        "fox_k_gain": gain(ks[12], (L, HEAD_DIM)),
        "moba_q_gain": gain(ks[13], (L, HEAD_DIM)),
        "moba_k_gain": gain(ks[14], (L, HEAD_DIM)),
        "mem_q_gain": gain(ks[15], (L, HEAD_DIM)),
        "mem_k_gain": gain(ks[16], (L, HEAD_DIM)),
        "w_out": nrm(ks[17], (L, MIX_W, D_MODEL), MIX_W),
        "ffn2_norm": gain(ks[18], (L, D_MODEL)),
        "ffn2_w1": nrm(ks[19], (L, D_MODEL, D_FF), D_MODEL),
        "ffn2_w3": nrm(ks[20], (L, D_MODEL, D_FF), D_MODEL),
        "ffn2_w2": nrm(ks[21], (L, D_FF, D_MODEL), D_FF),
        "rel_bias": 0.1 * jax.random.normal(ks[22], (REL_BUCKETS, MOBA_HEADS), f32),
    }


def reference(x, mem, ffn1_norm, ffn1_w1, ffn1_w3, ffn1_w2, mix_norm, mem_norm, w_in,
              b_forget, w_mem_kv, fox_q_gain, fox_k_gain, moba_q_gain, moba_k_gain,
              mem_q_gain, mem_k_gain, w_out, ffn2_norm, ffn2_w1, ffn2_w3, ffn2_w2, rel_bias):
    B, S, _ = x.shape
    M = mem.shape[1]
    splits = np.cumsum([FOX_W, FOX_W, FOX_W, FOX_HEADS, MOBA_W, MOBA_W, MOBA_W]).tolist()

    def heads(t, n_heads, length):
        return t.reshape(B, length, n_heads, HEAD_DIM).transpose(0, 2, 1, 3)

    for l in range(DEPTH):
        x = x + 0.5 * swiglu(rmsnorm(x, ffn1_norm[l]), ffn1_w1[l], ffn1_w3[l], ffn1_w2[l])

        h = rmsnorm(x, mix_norm[l])
        proj = h @ w_in[l]
        fq, fk, fv, ff, bq, bk, bv, cq = jnp.split(proj, splits, axis=-1)

        fq = rmsnorm(heads(fq, FOX_HEADS, S), fox_q_gain[l])
        fk = rmsnorm(heads(fk, FOX_HEADS, S), fox_k_gain[l])
        fv = heads(fv, FOX_HEADS, S)
        logf = jax.nn.log_sigmoid(ff.astype(jnp.float32)
                                  + b_forget[l].astype(jnp.float32)).transpose(0, 2, 1)
        o_fox = fox_attention(fq, fk, fv, logf)

        bq = rmsnorm(heads(bq, MOBA_HEADS, S), moba_q_gain[l])
        bk = rmsnorm(heads(bk, MOBA_HEADS, S), moba_k_gain[l])
        bv = heads(bv, MOBA_HEADS, S)
        o_moba = moba_attention(bq, bk, bv, rel_bias)

        cq = rmsnorm(heads(cq, MEM_HEADS, S), mem_q_gain[l])
        mkv = rmsnorm(mem, mem_norm[l]) @ w_mem_kv[l]
        ck, cv = jnp.split(mkv, 2, axis=-1)
        ck = rmsnorm(heads(ck, MEM_HEADS, M), mem_k_gain[l])
        cv = heads(cv, MEM_HEADS, M)
        o_mem = memory_attention(cq, ck, cv)

        o = jnp.concatenate([o_fox, o_moba, o_mem], axis=1)
        o = o.transpose(0, 2, 1, 3).reshape(B, S, MIX_W)
        x = x + o @ w_out[l]

        x = x + 0.5 * swiglu(rmsnorm(x, ffn2_norm[l]), ffn2_w1[l], ffn2_w3[l], ffn2_w2[l])
    return x
```

```python
import functools
import math

import jax
import jax.numpy as jnp
from jax import lax
from jax.experimental import pallas as pl
from jax.experimental.pallas import tpu as pltpu

HEAD_DIM = 128
FOX_HEADS = 8
MOBA_HEADS = 4
MEM_HEADS = 4
MOBA_BLOCK = 256
MOBA_TOPK = 3
REL_BUCKETS = 32
REL_MAX_DIST = 128
EPS = 1e-6
NEG = -1e30

LANES = 128
ATTN_Q_BLOCK = 256
VMEM_LIMIT_BYTES = 56 * 1024 * 1024

F32 = jnp.float32
BF16 = jnp.bfloat16


def _rms_scale(x):
    return lax.rsqrt(jnp.mean(x * x, axis=-1, keepdims=True) + EPS)


def _dot(a, b):
    return jnp.dot(a, b, preferred_element_type=F32)


def _dot_nt(a, b):
    return lax.dot_general(a, b, (((1,), (1,)), ((), ())), preferred_element_type=F32)


def _split3(x):
    hi = x.astype(BF16)
    r1 = x - hi.astype(F32)
    mid = r1.astype(BF16)
    lo = (r1 - mid.astype(F32)).astype(BF16)
    return hi, mid, lo


def _ffn_kernel(x_ref, g_ref, w1_ref, w3_ref, w2_ref, o_ref, xn_ref):
    f = pl.program_id(1)

    @pl.when(f == 0)
    def _():
        x = x_ref[...]
        xn_ref[...] = (x * _rms_scale(x) * g_ref[...]).astype(BF16)
        o_ref[...] = jnp.zeros_like(o_ref)

    xn = xn_ref[...]
    h1 = _dot(xn, w1_ref[...])
    h3 = _dot(xn, w3_ref[...])
    act = (h1 * jax.nn.sigmoid(h1) * h3).astype(BF16)
    o_ref[...] += _dot(act, w2_ref[...])

    @pl.when(f == pl.num_programs(1) - 1)
    def _():
        o_ref[...] = x_ref[...] + 0.5 * o_ref[...]


def _ffn(x2d, gain, w1, w3, w2, *, tm, tf):
    m, d = x2d.shape
    d_ff = w1.shape[1]
    return pl.pallas_call(
        _ffn_kernel,
        grid=(m // tm, d_ff // tf),
        in_specs=[
            pl.BlockSpec((tm, d), lambda i, f: (i, 0)),
            pl.BlockSpec((1, d), lambda i, f: (0, 0)),
            pl.BlockSpec((d, tf), lambda i, f: (0, f)),
            pl.BlockSpec((d, tf), lambda i, f: (0, f)),
            pl.BlockSpec((tf, d), lambda i, f: (f, 0)),
        ],
        out_specs=pl.BlockSpec((tm, d), lambda i, f: (i, 0)),
        out_shape=jax.ShapeDtypeStruct((m, d), F32),
        scratch_shapes=[pltpu.VMEM((tm, d), BF16)],
        compiler_params=pltpu.CompilerParams(
            dimension_semantics=("arbitrary", "arbitrary"),
            vmem_limit_bytes=VMEM_LIMIT_BYTES),
        name="ffn",
    )(x2d, gain.reshape(1, d), w1, w3, w2)


def _head_norm_store(y, hg_ref, hflag_ref, p_ref):
    for hh in range(y.shape[1] // HEAD_DIM):
        cols = slice(hh * HEAD_DIM, (hh + 1) * HEAD_DIM)
        yh = y[:, cols]
        normed = yh * _rms_scale(yh) * hg_ref[:, cols]
        p_ref[:, cols] = jnp.where(hflag_ref[:, cols] > 0.0, normed, yh).astype(BF16)


def _proj_kernel(x_ref, g_ref, w_ref, hg_ref, hflag_ref, p_ref, xn_ref):
    @pl.when(pl.program_id(1) == 0)
    def _():
        x = x_ref[...]
        xn_ref[...] = (x * _rms_scale(x) * g_ref[...]).astype(BF16)

    _head_norm_store(_dot(xn_ref[...], w_ref[...]), hg_ref, hflag_ref, p_ref)


def _proj_forget_kernel(x_ref, g_ref, w_ref, hg_ref, hflag_ref, wf_ref, bf_ref,
                        p_ref, c_ref, xn_ref, carry_ref, *, tiles_per_seq, cum_block):
    i = pl.program_id(0)

    @pl.when(pl.program_id(1) == 0)
    def _():
        x = x_ref[...]
        xn_ref[...] = (x * _rms_scale(x) * g_ref[...]).astype(BF16)

        @pl.when(i % tiles_per_seq == 0)
        def _():
            carry_ref[...] = jnp.zeros_like(carry_ref)

        z = _dot(xn_ref[...], wf_ref[...]) + bf_ref[...]
        logf = jnp.minimum(z, 0.0) - jnp.log1p(jnp.exp(-jnp.abs(z)))
        r = lax.broadcasted_iota(jnp.int32, (cum_block, cum_block), 0)
        c = lax.broadcasted_iota(jnp.int32, (cum_block, cum_block), 1)
        tril = (c <= r).astype(BF16)
        carry = carry_ref[...]
        for blk in range(logf.shape[0] // cum_block):
            rows = slice(blk * cum_block, (blk + 1) * cum_block)
            hi, mid, lo = _split3(logf[rows, :])
            cum = _dot(tril, hi) + _dot(tril, mid) + _dot(tril, lo) + carry
            c_ref[rows, :] = cum
            carry = cum[cum_block - 1:cum_block, :]
        carry_ref[...] = carry

    _head_norm_store(_dot(xn_ref[...], w_ref[...]), hg_ref, hflag_ref, p_ref)


def _norm_proj(x2d, gain, w, head_gain, head_flag, *, tm, tn, forget=None, seq_len=None):
    m, d = x2d.shape
    n = w.shape[1]
    in_specs = [
        pl.BlockSpec((tm, d), lambda i, j: (i, 0)),
        pl.BlockSpec((1, d), lambda i, j: (0, 0)),
        pl.BlockSpec((d, tn), lambda i, j: (0, j)),
        pl.BlockSpec((1, tn), lambda i, j: (0, j)),
        pl.BlockSpec((1, tn), lambda i, j: (0, j)),
    ]
    args = [x2d, gain.reshape(1, d), w, head_gain.reshape(1, n), head_flag.reshape(1, n)]
    p_spec = pl.BlockSpec((tm, tn), lambda i, j: (i, j))
    p_shape = jax.ShapeDtypeStruct((m, n), BF16)
    params = pltpu.CompilerParams(dimension_semantics=("arbitrary", "arbitrary"),
                                  vmem_limit_bytes=VMEM_LIMIT_BYTES)
    if forget is None:
        return pl.pallas_call(
            _proj_kernel, grid=(m // tm, n // tn), in_specs=in_specs, out_specs=p_spec,
            out_shape=p_shape, scratch_shapes=[pltpu.VMEM((tm, d), BF16)],
            compiler_params=params, name="mem_proj")(*args)
    wf, bf = forget
    in_specs += [pl.BlockSpec((d, LANES), lambda i, j: (0, 0)),
                 pl.BlockSpec((1, LANES), lambda i, j: (0, 0))]
    kern = functools.partial(_proj_forget_kernel, tiles_per_seq=seq_len // tm,
                             cum_block=min(tm, 256))
    return pl.pallas_call(
        kern, grid=(m // tm, n // tn), in_specs=in_specs,
        out_specs=[p_spec, pl.BlockSpec((tm, LANES), lambda i, j: (i, 0))],
        out_shape=[p_shape, jax.ShapeDtypeStruct((m, LANES), F32)],
        scratch_shapes=[pltpu.VMEM((tm, d), BF16), pltpu.VMEM((1, LANES), F32)],
        compiler_params=params, name="mix_proj")(*args, wf, bf)


def _lane_column(x, lane):
    lanes = lax.broadcasted_iota(jnp.int32, x.shape, 1)
    return jnp.sum(jnp.where(lanes == lane, x, 0.0), axis=-1, keepdims=True)


def _fox_kernel(q_ref, k_ref, v_ref, c_ref, o_ref, crow_ref, *, seq_len):
    h = pl.program_id(1)
    tq = ATTN_Q_BLOCK
    nq = seq_len // tq
    scale = HEAD_DIM ** -0.5
    r = lax.broadcasted_iota(jnp.int32, (tq, tq), 0)
    c = lax.broadcasted_iota(jnp.int32, (tq, tq), 1)
    diag = r == c

    for j in range(nq):
        cj = _lane_column(c_ref[j * tq:(j + 1) * tq, :], h)
        crow_ref[:, j * tq:(j + 1) * tq] = jnp.sum(jnp.where(diag, cj, 0.0), axis=0, keepdims=True)

    for i in range(nq):
        n = (i + 1) * tq
        q = q_ref[i * tq:(i + 1) * tq, :]
        s = _dot_nt(q, k_ref[0:n, :])
        c_col = _lane_column(c_ref[i * tq:(i + 1) * tq, :], h)
        logits = s * scale + (c_col - crow_ref[:, 0:n])
        kpos = lax.broadcasted_iota(jnp.int32, (tq, n), 1)
        qpos = i * tq + lax.broadcasted_iota(jnp.int32, (tq, n), 0)
        logits = jnp.where(kpos <= qpos, logits, NEG)
        mx = jnp.max(logits, axis=-1, keepdims=True)
        p = jnp.exp(logits - mx)
        denom = jnp.sum(p, axis=-1, keepdims=True)
        pv = _dot(p.astype(BF16), v_ref[0:n, :])
        o_ref[i * tq:(i + 1) * tq, :] = (pv / denom).astype(BF16)


def _fox_attn(p, cum, *, batch, seq_len):
    m = p.shape[0]
    hd = HEAD_DIM
    return pl.pallas_call(
        functools.partial(_fox_kernel, seq_len=seq_len),
        grid=(batch, FOX_HEADS),
        in_specs=[
            pl.BlockSpec((seq_len, hd), lambda b, h: (b, h)),
            pl.BlockSpec((seq_len, hd), lambda b, h: (b, FOX_HEADS + h)),
            pl.BlockSpec((seq_len, hd), lambda b, h: (b, 2 * FOX_HEADS + h)),
            pl.BlockSpec((seq_len, LANES), lambda b, h: (b, 0)),
        ],
        out_specs=pl.BlockSpec((seq_len, hd), lambda b, h: (b, h)),
        out_shape=jax.ShapeDtypeStruct((m, FOX_HEADS * hd), BF16),
        scratch_shapes=[pltpu.VMEM((1, seq_len), F32)],
        compiler_params=pltpu.CompilerParams(
            dimension_semantics=("arbitrary", "arbitrary"),
            vmem_limit_bytes=VMEM_LIMIT_BYTES),
        name="fox_attn",
    )(p, p, p, cum)


def _t5_bias(dist, rel_ref, h):
    n = jnp.maximum(dist, 0)
    max_exact = REL_BUCKETS // 2
    nf = jnp.maximum(n, 1).astype(F32)
    large = max_exact + (jnp.log(nf / max_exact) / math.log(REL_MAX_DIST / max_exact)
                         * (REL_BUCKETS - max_exact)).astype(jnp.int32)
    large = jnp.minimum(large, REL_BUCKETS - 1)
    bucket = jnp.where(n < max_exact, n, large)
    bias = jnp.zeros(dist.shape, F32)
    for b in range(REL_BUCKETS):
        bias = jnp.where(bucket == b, rel_ref[b, h], bias)
    return bias


def _moba_kernel(rel_ref, q_ref, k_ref, v_ref, o_ref, town_ref, tprev_ref, expand_ref, kmean_ref,
                 *, seq_len):
    h = pl.program_id(0)
    blk = MOBA_BLOCK
    nb = seq_len // blk
    scale = HEAD_DIM ** -0.5
    r = lax.broadcasted_iota(jnp.int32, (blk, blk), 0)
    c = lax.broadcasted_iota(jnp.int32, (blk, blk), 1)

    @pl.when(pl.program_id(1) == 0)
    def _():
        town_ref[...] = _t5_bias(r - c, rel_ref, h)
        tprev_ref[...] = _t5_bias(r - c + blk, rel_ref, h)
        j = lax.broadcasted_iota(jnp.int32, expand_ref.shape, 0)
        s = lax.broadcasted_iota(jnp.int32, expand_ref.shape, 1)
        expand_ref[...] = ((s >= j * blk) & (s < (j + 1) * blk)).astype(BF16)

    kmean_ref[...] = jnp.zeros_like(kmean_ref)
    for j in range(nb):
        kmean_ref[j:j + 1, :] = jnp.mean(k_ref[j * blk:(j + 1) * blk, :].astype(F32),
                                         axis=0, keepdims=True)
    km_hi, km_mid, km_lo = _split3(kmean_ref[...])

    far_bias = _t5_bias(jnp.full((1, 1), 2 * blk, jnp.int32), rel_ref, h)
    lanes = lax.broadcasted_iota(jnp.int32, (blk, LANES), 1)

    for i in range(nb):
        q = q_ref[i * blk:(i + 1) * blk, :]
        n = (i + 1) * blk
        s = _dot_nt(q, k_ref[0:n, :]) * scale
        own = jnp.where(c <= r, s[:, i * blk:] + town_ref[...], NEG)
        pieces = []
        if i >= 1:
            past = s[:, :i * blk]
            if i >= 2:
                bias = jnp.concatenate(
                    [jnp.broadcast_to(far_bias, (blk, (i - 1) * blk)), tprev_ref[...]], axis=1)
            else:
                bias = tprev_ref[...]
            past = past + bias
            if i > MOBA_TOPK:
                gate = _dot_nt(q, km_hi) + _dot_nt(q, km_mid) + _dot_nt(q, km_lo)
                valid = lanes < i
                g = jnp.where(valid, gate, NEG)
                sel = jnp.zeros(g.shape, jnp.bool_)
                for _ in range(MOBA_TOPK):
                    best = jnp.max(g, axis=-1, keepdims=True)
                    first = jnp.min(jnp.where(g == best, lanes, LANES), axis=-1, keepdims=True)
                    pick = lanes == first
                    sel = sel | pick
                    g = jnp.where(pick, -jnp.inf, g)
                sel = (sel & valid).astype(BF16)
                chosen = _dot(sel, expand_ref[:, 0:i * blk])
                past = jnp.where(chosen > 0.5, past, NEG)
            pieces.append(past)
        pieces.append(own)
        logits = jnp.concatenate(pieces, axis=1) if len(pieces) > 1 else own
        mx = jnp.max(logits, axis=-1, keepdims=True)
        p = jnp.exp(logits - mx)
        denom = jnp.sum(p, axis=-1, keepdims=True)
        pv = _dot(p.astype(BF16), v_ref[0:n, :])
        o_ref[i * blk:(i + 1) * blk, :] = (pv / denom).astype(BF16)


def _moba_attn(p, rel_bias, *, batch, seq_len, col0):
    m = p.shape[0]
    hd = HEAD_DIM
    blk = MOBA_BLOCK
    return pl.pallas_call(
        functools.partial(_moba_kernel, seq_len=seq_len),
        grid=(MOBA_HEADS, batch),
        in_specs=[
            pl.BlockSpec(memory_space=pltpu.SMEM),
            pl.BlockSpec((seq_len, hd), lambda h, b: (b, col0 + h)),
            pl.BlockSpec((seq_len, hd), lambda h, b: (b, col0 + MOBA_HEADS + h)),
            pl.BlockSpec((seq_len, hd), lambda h, b: (b, col0 + 2 * MOBA_HEADS + h)),
        ],
        out_specs=pl.BlockSpec((seq_len, hd), lambda h, b: (b, h)),
        out_shape=jax.ShapeDtypeStruct((m, MOBA_HEADS * hd), BF16),
        scratch_shapes=[
            pltpu.VMEM((blk, blk), F32),
            pltpu.VMEM((blk, blk), F32),
            pltpu.VMEM((LANES, seq_len), BF16),
            pltpu.VMEM((LANES, hd), F32),
        ],
        compiler_params=pltpu.CompilerParams(
            dimension_semantics=("arbitrary", "arbitrary"),
            vmem_limit_bytes=VMEM_LIMIT_BYTES),
        name="moba_attn",
    )(rel_bias, p, p, p)


def _mem_kernel(q_ref, k_ref, v_ref, o_ref, *, seq_len, tq):
    scale = HEAD_DIM ** -0.5
    k = k_ref[...]
    v = v_ref[...]
    for i in range(seq_len // tq):
        rows = slice(i * tq, (i + 1) * tq)
        logits = _dot_nt(q_ref[rows, :], k) * scale
        mx = jnp.max(logits, axis=-1, keepdims=True)
        p = jnp.exp(logits - mx)
        denom = jnp.sum(p, axis=-1, keepdims=True)
        o_ref[rows, :] = (_dot(p.astype(BF16), v) / denom).astype(BF16)


def _mem_attn(p, mkv, *, batch, seq_len, n_mem, col0):
    m = p.shape[0]
    hd = HEAD_DIM
    return pl.pallas_call(
        functools.partial(_mem_kernel, seq_len=seq_len, tq=min(seq_len, 512)),
        grid=(batch, MEM_HEADS),
        in_specs=[
            pl.BlockSpec((seq_len, hd), lambda b, h: (b, col0 + h)),
            pl.BlockSpec((n_mem, hd), lambda b, h: (b, h)),
            pl.BlockSpec((n_mem, hd), lambda b, h: (b, MEM_HEADS + h)),
        ],
        out_specs=pl.BlockSpec((seq_len, hd), lambda b, h: (b, h)),
        out_shape=jax.ShapeDtypeStruct((m, MEM_HEADS * hd), BF16),
        compiler_params=pltpu.CompilerParams(
            dimension_semantics=("arbitrary", "arbitrary"),
            vmem_limit_bytes=VMEM_LIMIT_BYTES),
        name="mem_attn",
    )(p, mkv, mkv)


def _out_proj_kernel(x_ref, of_ref, ob_ref, om_ref, w_ref, o_ref):
    wf = of_ref.shape[1]
    wb = ob_ref.shape[1]
    acc = _dot(of_ref[...], w_ref[0:wf, :])
    acc += _dot(ob_ref[...], w_ref[wf:wf + wb, :])
    acc += _dot(om_ref[...], w_ref[wf + wb:, :])
    o_ref[...] = x_ref[...] + acc


def _out_proj(x2d, o_fox, o_moba, o_mem, w_out, *, tm):
    m, d = x2d.shape
    return pl.pallas_call(
        _out_proj_kernel,
        grid=(m // tm,),
        in_specs=[
            pl.BlockSpec((tm, d), lambda i: (i, 0)),
            pl.BlockSpec((tm, o_fox.shape[1]), lambda i: (i, 0)),
            pl.BlockSpec((tm, o_moba.shape[1]), lambda i: (i, 0)),
            pl.BlockSpec((tm, o_mem.shape[1]), lambda i: (i, 0)),
            pl.BlockSpec(w_out.shape, lambda i: (0, 0)),
        ],
        out_specs=pl.BlockSpec((tm, d), lambda i: (i, 0)),
        out_shape=jax.ShapeDtypeStruct((m, d), F32),
        compiler_params=pltpu.CompilerParams(
            dimension_semantics=("arbitrary",), vmem_limit_bytes=VMEM_LIMIT_BYTES),
        name="out_proj",
    )(x2d, o_fox, o_moba, o_mem, w_out)


def _tile(total, want):
    t = min(total, want)
    assert total % t == 0, (total, want)
    return t


def kernel(x, mem, ffn1_norm, ffn1_w1, ffn1_w3, ffn1_w2, mix_norm, mem_norm, w_in, b_forget,
           w_mem_kv, fox_q_gain, fox_k_gain, moba_q_gain, moba_k_gain, mem_q_gain, mem_k_gain,
           w_out, ffn2_norm, ffn2_w1, ffn2_w3, ffn2_w2, rel_bias):
    batch, seq_len, d = x.shape
    n_mem = mem.shape[1]
    depth = w_in.shape[0]
    fox_w = FOX_HEADS * HEAD_DIM
    moba_w = MOBA_HEADS * HEAD_DIM
    mem_w = MEM_HEADS * HEAD_DIM
    m = batch * seq_len
    tm = _tile(seq_len, 512)
    ones = jnp.ones((HEAD_DIM,), F32)

    x2d = x.reshape(m, d)
    mem2d = mem.reshape(batch * n_mem, d)
    for l in range(depth):
        x2d = _ffn(x2d, ffn1_norm[l], ffn1_w1[l].astype(BF16), ffn1_w3[l].astype(BF16),
                   ffn1_w2[l].astype(BF16), tm=tm, tf=_tile(ffn1_w1.shape[2], 512))

        f0 = 3 * fox_w
        w_main = jnp.concatenate([w_in[l][:, :f0], w_in[l][:, f0 + FOX_HEADS:]], axis=1).astype(BF16)
        w_forget = jnp.pad(w_in[l][:, f0:f0 + FOX_HEADS], ((0, 0), (0, LANES - FOX_HEADS))).astype(BF16)
        b_pad = jnp.pad(b_forget[l].astype(F32), (0, LANES - FOX_HEADS)).reshape(1, LANES)
        head_gain = jnp.concatenate(
            [jnp.tile(fox_q_gain[l], FOX_HEADS), jnp.tile(fox_k_gain[l], FOX_HEADS),
             jnp.tile(ones, FOX_HEADS), jnp.tile(moba_q_gain[l], MOBA_HEADS),
             jnp.tile(moba_k_gain[l], MOBA_HEADS), jnp.tile(ones, MOBA_HEADS),
             jnp.tile(mem_q_gain[l], MEM_HEADS)]).astype(F32)
        head_flag = jnp.concatenate(
            [jnp.ones((2 * fox_w,), F32), jnp.zeros((fox_w,), F32), jnp.ones((2 * moba_w,), F32),
             jnp.zeros((moba_w,), F32), jnp.ones((mem_w,), F32)])
        proj, cum = _norm_proj(x2d, mix_norm[l], w_main, head_gain, head_flag, tm=tm, tn=512,
                               forget=(w_forget, b_pad), seq_len=seq_len)

        kv_gain = jnp.concatenate([jnp.tile(mem_k_gain[l], MEM_HEADS), jnp.tile(ones, MEM_HEADS)])
        kv_flag = jnp.concatenate([jnp.ones((mem_w,), F32), jnp.zeros((mem_w,), F32)])
        mkv = _norm_proj(mem2d, mem_norm[l], w_mem_kv[l].astype(BF16), kv_gain.astype(F32), kv_flag,
                         tm=_tile(batch * n_mem, 512), tn=512)

        o_fox = _fox_attn(proj, cum, batch=batch, seq_len=seq_len)
        o_moba = _moba_attn(proj, rel_bias.astype(F32), batch=batch, seq_len=seq_len,
                            col0=3 * FOX_HEADS)
        o_mem = _mem_attn(proj, mkv, batch=batch, seq_len=seq_len, n_mem=n_mem,
                          col0=3 * FOX_HEADS + 3 * MOBA_HEADS)
        x2d = _out_proj(x2d, o_fox, o_moba, o_mem, w_out[l].astype(BF16), tm=tm)

        x2d = _ffn(x2d, ffn2_norm[l], ffn2_w1[l].astype(BF16), ffn2_w3[l].astype(BF16),
                   ffn2_w2[l].astype(BF16), tm=tm, tf=_tile(ffn2_w1.shape[2], 512))
    return x2d.reshape(batch, seq_len, d)
```

```python
import functools
import math

import jax
import jax.numpy as jnp
from jax import lax
from jax.experimental import pallas as pl
from jax.experimental.pallas import tpu as pltpu

HEAD_DIM = 128
FOX_HEADS = 8
MOBA_HEADS = 4
MEM_HEADS = 4
MOBA_BLOCK = 256
MOBA_TOPK = 3
REL_BUCKETS = 32
REL_MAX_DIST = 128
EPS = 1e-6
NEG = -1e30
LOG2E = math.log2(math.e)

LANES = 128
MXU_WIDTH = 256
ATTN_Q_BLOCK = 256
VMEM_LIMIT_BYTES = 56 * 1024 * 1024

F32 = jnp.float32
BF16 = jnp.bfloat16


def _rms_scale(x):
    return lax.rsqrt(jnp.mean(x * x, axis=-1, keepdims=True) + EPS)


def _dot(a, b):
    return jnp.dot(a, b, preferred_element_type=F32)


def _dot_nt(a, b):
    return lax.dot_general(a, b, (((1,), (1,)), ((), ())), preferred_element_type=F32)


def _split3(x):
    hi = x.astype(BF16)
    r1 = x - hi.astype(F32)
    mid = r1.astype(BF16)
    lo = (r1 - mid.astype(F32)).astype(BF16)
    return hi, mid, lo


def _softmax_pv(logits2, v):
    mx = jnp.max(logits2, axis=-1, keepdims=True)
    p = jnp.exp2(logits2 - mx)
    denom = jnp.sum(p, axis=-1, keepdims=True)
    return _dot(p.astype(BF16), v) / denom


def _ffn_kernel(x_ref, g_ref, w1_ref, w3_ref, w2_ref, o_ref, xn_ref):
    @pl.when(pl.program_id(1) == 0)
    def _():
        x = x_ref[...]
        xn_ref[...] = (x * _rms_scale(x) * g_ref[...]).astype(BF16)
        o_ref[...] = x

    xn = xn_ref[...]
    h1 = _dot(xn, w1_ref[...])
    h3 = _dot(xn, w3_ref[...])
    act = (0.5 * h1 * jax.nn.sigmoid(h1) * h3).astype(BF16)
    o_ref[...] += _dot(act, w2_ref[...])


def _ffn(x2d, gain, w1, w3, w2, *, tm, tf):
    m, d = x2d.shape
    d_ff = w1.shape[1]
    return pl.pallas_call(
        _ffn_kernel,
        grid=(m // tm, d_ff // tf),
        in_specs=[
            pl.BlockSpec((tm, d), lambda i, f: (i, 0)),
            pl.BlockSpec((1, d), lambda i, f: (0, 0)),
            pl.BlockSpec((d, tf), lambda i, f: (0, f)),
            pl.BlockSpec((d, tf), lambda i, f: (0, f)),
            pl.BlockSpec((tf, d), lambda i, f: (f, 0)),
        ],
        out_specs=pl.BlockSpec((tm, d), lambda i, f: (i, 0)),
        out_shape=jax.ShapeDtypeStruct((m, d), F32),
        scratch_shapes=[pltpu.VMEM((tm, d), BF16)],
        compiler_params=pltpu.CompilerParams(
            dimension_semantics=("arbitrary", "arbitrary"),
            vmem_limit_bytes=VMEM_LIMIT_BYTES),
        name="ffn",
    )(x2d, gain.reshape(1, d), w1, w3, w2)


def _project_heads(xn_ref, w_ref, hg_ref, hflag_ref, p_ref):
    xn = xn_ref[...]
    for sb in range(w_ref.shape[1] // MXU_WIDTH):
        y = _dot(xn, w_ref[:, sb * MXU_WIDTH:(sb + 1) * MXU_WIDTH])
        for hh in range(MXU_WIDTH // HEAD_DIM):
            cols = slice(sb * MXU_WIDTH + hh * HEAD_DIM, sb * MXU_WIDTH + (hh + 1) * HEAD_DIM)
            yh = y[:, hh * HEAD_DIM:(hh + 1) * HEAD_DIM]
            normed = yh * _rms_scale(yh) * hg_ref[:, cols]
            p_ref[:, cols] = jnp.where(hflag_ref[:, cols] > 0.0, normed, yh).astype(BF16)


def _proj_kernel(x_ref, g_ref, w_ref, hg_ref, hflag_ref, p_ref, xn_ref):
    @pl.when(pl.program_id(1) == 0)
    def _():
        x = x_ref[...]
        xn_ref[...] = (x * _rms_scale(x) * g_ref[...]).astype(BF16)

    _project_heads(xn_ref, w_ref, hg_ref, hflag_ref, p_ref)


def _proj_forget_kernel(x_ref, g_ref, w_ref, hg_ref, hflag_ref, wf_ref, bf_ref,
                        p_ref, c_ref, xn_ref, carry_ref, *, tiles_per_seq, cum_block):
    i = pl.program_id(0)

    @pl.when(pl.program_id(1) == 0)
    def _():
        x = x_ref[...]
        xn_ref[...] = (x * _rms_scale(x) * g_ref[...]).astype(BF16)

        @pl.when(i % tiles_per_seq == 0)
        def _():
            carry_ref[...] = jnp.zeros_like(carry_ref)

        z = _dot(xn_ref[...], wf_ref[...]) + bf_ref[...]
        logf = jnp.minimum(z, 0.0) - jnp.log1p(jnp.exp(-jnp.abs(z)))
        r = lax.broadcasted_iota(jnp.int32, (cum_block, cum_block), 0)
        c = lax.broadcasted_iota(jnp.int32, (cum_block, cum_block), 1)
        tril = (c <= r).astype(BF16)
        local = []
        for blk in range(logf.shape[0] // cum_block):
            hi, mid, lo = _split3(logf[blk * cum_block:(blk + 1) * cum_block, :])
            local.append(_dot(tril, hi) + _dot(tril, mid) + _dot(tril, lo))
        carry = carry_ref[...]
        for blk, loc in enumerate(local):
            cum = loc + carry
            c_ref[blk * cum_block:(blk + 1) * cum_block, :] = cum
            carry = cum[cum_block - 1:cum_block, :]
        carry_ref[...] = carry

    _project_heads(xn_ref, w_ref, hg_ref, hflag_ref, p_ref)


def _norm_proj(x2d, gain, w, head_gain, head_flag, *, tm, tn, forget=None, seq_len=None):
    m, d = x2d.shape
    n = w.shape[1]
    in_specs = [
        pl.BlockSpec((tm, d), lambda i, j: (i, 0)),
        pl.BlockSpec((1, d), lambda i, j: (0, 0)),
        pl.BlockSpec((d, tn), lambda i, j: (0, j)),
        pl.BlockSpec((1, tn), lambda i, j: (0, j)),
        pl.BlockSpec((1, tn), lambda i, j: (0, j)),
    ]
    args = [x2d, gain.reshape(1, d), w, head_gain.reshape(1, n), head_flag.reshape(1, n)]
    p_spec = pl.BlockSpec((tm, tn), lambda i, j: (i, j))
    p_shape = jax.ShapeDtypeStruct((m, n), BF16)
    params = pltpu.CompilerParams(dimension_semantics=("arbitrary", "arbitrary"),
                                  vmem_limit_bytes=VMEM_LIMIT_BYTES)
    if forget is None:
        return pl.pallas_call(
            _proj_kernel, grid=(m // tm, n // tn), in_specs=in_specs, out_specs=p_spec,
            out_shape=p_shape, scratch_shapes=[pltpu.VMEM((tm, d), BF16)],
            compiler_params=params, name="mem_proj")(*args)
    wf, bf = forget
    in_specs += [pl.BlockSpec((d, LANES), lambda i, j: (0, 0)),
                 pl.BlockSpec((1, LANES), lambda i, j: (0, 0))]
    kern = functools.partial(_proj_forget_kernel, tiles_per_seq=seq_len // tm,
                             cum_block=min(tm, 256))
    return pl.pallas_call(
        kern, grid=(m // tm, n // tn), in_specs=in_specs,
        out_specs=[p_spec, pl.BlockSpec((tm, LANES), lambda i, j: (i, 0))],
        out_shape=[p_shape, jax.ShapeDtypeStruct((m, LANES), F32)],
        scratch_shapes=[pltpu.VMEM((tm, d), BF16), pltpu.VMEM((1, LANES), F32)],
        compiler_params=params, name="mix_proj")(*args, wf, bf)


def _lane_column(x, lane):
    lanes = lax.broadcasted_iota(jnp.int32, x.shape, 1)
    return jnp.sum(jnp.where(lanes == lane, x, 0.0), axis=-1, keepdims=True)


def _fox_kernel(q_ref, k_ref, v_ref, c_ref, o_ref, crow_ref, *, seq_len):
    h = pl.program_id(1)
    tq = ATTN_Q_BLOCK
    nq = seq_len // tq
    r = lax.broadcasted_iota(jnp.int32, (tq, tq), 0)
    c = lax.broadcasted_iota(jnp.int32, (tq, tq), 1)
    diag = r == c
    causal = c <= r

    for j in range(nq):
        cj = _lane_column(c_ref[j * tq:(j + 1) * tq, :], h) * LOG2E
        crow_ref[:, j * tq:(j + 1) * tq] = jnp.sum(jnp.where(diag, cj, 0.0), axis=0, keepdims=True)

    def qk(i):
        return _dot_nt(q_ref[i * tq:(i + 1) * tq, :], k_ref[0:(i + 1) * tq, :])

    s_next = qk(0)
    for i in range(nq):
        n = (i + 1) * tq
        s = s_next
        if i + 1 < nq:
            s_next = qk(i + 1)
        c_col = _lane_column(c_ref[i * tq:(i + 1) * tq, :], h) * LOG2E
        logits = s + (c_col - crow_ref[:, 0:n])
        own = jnp.where(causal, logits[:, i * tq:], NEG)
        if i > 0:
            logits = jnp.concatenate([logits[:, :i * tq], own], axis=1)
        else:
            logits = own
        o_ref[i * tq:(i + 1) * tq, :] = _softmax_pv(logits, v_ref[0:n, :]).astype(BF16)


def _fox_attn(p, cum, *, batch, seq_len):
    m = p.shape[0]
    hd = HEAD_DIM
    return pl.pallas_call(
        functools.partial(_fox_kernel, seq_len=seq_len),
        grid=(batch, FOX_HEADS),
        in_specs=[
            pl.BlockSpec((seq_len, hd), lambda b, h: (b, h)),
            pl.BlockSpec((seq_len, hd), lambda b, h: (b, FOX_HEADS + h)),
            pl.BlockSpec((seq_len, hd), lambda b, h: (b, 2 * FOX_HEADS + h)),
            pl.BlockSpec((seq_len, LANES), lambda b, h: (b, 0)),
        ],
        out_specs=pl.BlockSpec((seq_len, hd), lambda b, h: (b, h)),
        out_shape=jax.ShapeDtypeStruct((m, FOX_HEADS * hd), BF16),
        scratch_shapes=[pltpu.VMEM((1, seq_len), F32)],
        compiler_params=pltpu.CompilerParams(
            dimension_semantics=("arbitrary", "arbitrary"),
            vmem_limit_bytes=VMEM_LIMIT_BYTES),
        name="fox_attn",
    )(p, p, p, cum)


def _t5_bias(dist, rel_ref, h):
    n = jnp.maximum(dist, 0)
    max_exact = REL_BUCKETS // 2
    nf = jnp.maximum(n, 1).astype(F32)
    large = max_exact + (jnp.log(nf / max_exact) / math.log(REL_MAX_DIST / max_exact)
                         * (REL_BUCKETS - max_exact)).astype(jnp.int32)
    large = jnp.minimum(large, REL_BUCKETS - 1)
    bucket = jnp.where(n < max_exact, n, large)
    bias = jnp.zeros(dist.shape, F32)
    for b in range(REL_BUCKETS):
        bias = jnp.where(bucket == b, rel_ref[b, h], bias)
    return bias


def _moba_kernel(rel_ref, q_ref, k_ref, v_ref, o_ref, town_ref, tprev_ref, kaug_ref, qaug_ref,
                 kmean_ref, *, seq_len):
    h = pl.program_id(0)
    blk = MOBA_BLOCK
    nb = seq_len // blk
    r = lax.broadcasted_iota(jnp.int32, (blk, blk), 0)
    c = lax.broadcasted_iota(jnp.int32, (blk, blk), 1)
    lane_grp = lax.broadcasted_iota(jnp.int32, (1, LANES), 1) >> 3

    @pl.when(pl.program_id(1) == 0)
    def _():
        town_ref[...] = jnp.where(c <= r, _t5_bias(r - c, rel_ref, h) * LOG2E, NEG)
        tprev_ref[...] = _t5_bias(r - c + blk, rel_ref, h) * LOG2E
        s = lax.broadcasted_iota(jnp.int32, (seq_len, LANES), 0)
        ln = lax.broadcasted_iota(jnp.int32, (seq_len, LANES), 1)
        first_key = (ln & 7) * blk
        onehot = (ln < 32) & (s >= first_key) & (s < first_key + blk)
        kaug_ref[:, HEAD_DIM:] = jnp.where(onehot, 1.0, 0.0).astype(BF16)

    kaug_ref[:, :HEAD_DIM] = k_ref[...]

    kmean_ref[...] = jnp.zeros_like(kmean_ref)
    for j in range(nb):
        kmean_ref[j:j + 1, :] = jnp.mean(k_ref[j * blk:(j + 1) * blk, :].astype(F32),
                                         axis=0, keepdims=True)
    km_hi, km_mid, km_lo = _split3(kmean_ref[...])

    far = jnp.full((1, LANES), rel_ref[REL_BUCKETS - 1, h] * LOG2E, F32)
    far_hi, far_mid, far_lo = _split3(far)
    far_parts = jnp.where(lane_grp == 1, far_hi.astype(F32),
                          jnp.where(lane_grp == 2, far_mid.astype(F32),
                                    jnp.where(lane_grp == 3, far_lo.astype(F32), 0.0)))

    def far_bias(row0, nrows):
        rows = row0 + lax.broadcasted_iota(jnp.int32, (nrows, LANES), 0)
        lanes = lax.broadcasted_iota(jnp.int32, (nrows, LANES), 1)
        own = rows >> (blk.bit_length() - 1)
        return jnp.where((lanes & 7) <= own - 2, far_parts, 0.0), lanes, own

    qaug_ref[:, :HEAD_DIM] = q_ref[...]
    late = min(seq_len, (MOBA_TOPK + 1) * blk)
    qaug_ref[:late, HEAD_DIM:] = far_bias(0, late)[0].astype(BF16)
    if seq_len > late:
        ql = q_ref[late:, :]
        gate = _dot_nt(ql, km_hi) + _dot_nt(ql, km_mid) + _dot_nt(ql, km_lo)
        aug, lanes_l, own_l = far_bias(late, seq_len - late)
        valid = lanes_l < own_l
        g = jnp.where(valid, gate, NEG)
        sel = jnp.zeros(g.shape, jnp.bool_)
        for _ in range(MOBA_TOPK):
            best = jnp.max(g, axis=-1, keepdims=True)
            first = jnp.min(jnp.where(g == best, lanes_l, LANES), axis=-1, keepdims=True)
            pick = lanes_l == first
            sel = sel | pick
            g = jnp.where(pick, -jnp.inf, g)
        qaug_ref[late:, HEAD_DIM:] = jnp.where(valid & jnp.logical_not(sel), NEG, aug).astype(BF16)

    def qk(i):
        return _dot_nt(qaug_ref[i * blk:(i + 1) * blk, :], kaug_ref[0:(i + 1) * blk, :])

    s_next = qk(0)
    for i in range(nb):
        n = (i + 1) * blk
        s = s_next
        if i + 1 < nb:
            s_next = qk(i + 1)
        pieces = [s[:, i * blk:] + town_ref[...]]
        if i >= 1:
            pieces.insert(0, s[:, (i - 1) * blk:i * blk] + tprev_ref[...])
        if i >= 2:
            pieces.insert(0, s[:, :(i - 1) * blk])
        logits = jnp.concatenate(pieces, axis=1) if len(pieces) > 1 else pieces[0]
        o_ref[i * blk:(i + 1) * blk, :] = _softmax_pv(logits, v_ref[0:n, :]).astype(BF16)


def _moba_attn(p, rel_bias, *, batch, seq_len, col0):
    m = p.shape[0]
    hd = HEAD_DIM
    blk = MOBA_BLOCK
    assert seq_len % blk == 0 and seq_len // blk <= 8
    return pl.pallas_call(
        functools.partial(_moba_kernel, seq_len=seq_len),
        grid=(MOBA_HEADS, batch),
        in_specs=[
            pl.BlockSpec(memory_space=pltpu.SMEM),
            pl.BlockSpec((seq_len, hd), lambda h, b: (b, col0 + h)),
            pl.BlockSpec((seq_len, hd), lambda h, b: (b, col0 + MOBA_HEADS + h)),
            pl.BlockSpec((seq_len, hd), lambda h, b: (b, col0 + 2 * MOBA_HEADS + h)),
        ],
        out_specs=pl.BlockSpec((seq_len, hd), lambda h, b: (b, h)),
        out_shape=jax.ShapeDtypeStruct((m, MOBA_HEADS * hd), BF16),
        scratch_shapes=[
            pltpu.VMEM((blk, blk), F32),
            pltpu.VMEM((blk, blk), F32),
            pltpu.VMEM((seq_len, hd + LANES), BF16),
            pltpu.VMEM((seq_len, hd + LANES), BF16),
            pltpu.VMEM((LANES, hd), F32),
        ],
        compiler_params=pltpu.CompilerParams(
            dimension_semantics=("arbitrary", "arbitrary"),
            vmem_limit_bytes=VMEM_LIMIT_BYTES),
        name="moba_attn",
    )(rel_bias, p, p, p)


def _mem_kernel(q_ref, k_ref, v_ref, o_ref, *, seq_len, tq):
    k = k_ref[...]
    v = v_ref[...]
    for i in range(seq_len // tq):
        rows = slice(i * tq, (i + 1) * tq)
        o_ref[rows, :] = _softmax_pv(_dot_nt(q_ref[rows, :], k), v).astype(BF16)


def _mem_attn(p, mkv, *, batch, seq_len, n_mem, col0):
    m = p.shape[0]
    hd = HEAD_DIM
    return pl.pallas_call(
        functools.partial(_mem_kernel, seq_len=seq_len, tq=min(seq_len, 512)),
        grid=(batch, MEM_HEADS),
        in_specs=[
            pl.BlockSpec((seq_len, hd), lambda b, h: (b, col0 + h)),
            pl.BlockSpec((n_mem, hd), lambda b, h: (b, h)),
            pl.BlockSpec((n_mem, hd), lambda b, h: (b, MEM_HEADS + h)),
        ],
        out_specs=pl.BlockSpec((seq_len, hd), lambda b, h: (b, h)),
        out_shape=jax.ShapeDtypeStruct((m, MEM_HEADS * hd), BF16),
        compiler_params=pltpu.CompilerParams(
            dimension_semantics=("arbitrary", "arbitrary"),
            vmem_limit_bytes=VMEM_LIMIT_BYTES),
        name="mem_attn",
    )(p, mkv, mkv)


def _out_proj_kernel(x_ref, of_ref, ob_ref, om_ref, w_ref, o_ref):
    wf = of_ref.shape[1]
    wb = ob_ref.shape[1]
    acc = _dot(of_ref[...], w_ref[0:wf, :])
    acc += _dot(ob_ref[...], w_ref[wf:wf + wb, :])
    acc += _dot(om_ref[...], w_ref[wf + wb:, :])
    o_ref[...] = x_ref[...] + acc


def _out_proj(x2d, o_fox, o_moba, o_mem, w_out, *, tm):
    m, d = x2d.shape
    return pl.pallas_call(
        _out_proj_kernel,
        grid=(m // tm,),
        in_specs=[
            pl.BlockSpec((tm, d), lambda i: (i, 0)),
            pl.BlockSpec((tm, o_fox.shape[1]), lambda i: (i, 0)),
            pl.BlockSpec((tm, o_moba.shape[1]), lambda i: (i, 0)),
            pl.BlockSpec((tm, o_mem.shape[1]), lambda i: (i, 0)),
            pl.BlockSpec(w_out.shape, lambda i: (0, 0)),
        ],
        out_specs=pl.BlockSpec((tm, d), lambda i: (i, 0)),
        out_shape=jax.ShapeDtypeStruct((m, d), F32),
        compiler_params=pltpu.CompilerParams(
            dimension_semantics=("arbitrary",), vmem_limit_bytes=VMEM_LIMIT_BYTES),
        name="out_proj",
    )(x2d, o_fox, o_moba, o_mem, w_out)


def _tile(total, want):
    t = min(total, want)
    assert total % t == 0, (total, want)
    return t


def kernel(x, mem, ffn1_norm, ffn1_w1, ffn1_w3, ffn1_w2, mix_norm, mem_norm, w_in, b_forget,
           w_mem_kv, fox_q_gain, fox_k_gain, moba_q_gain, moba_k_gain, mem_q_gain, mem_k_gain,
           w_out, ffn2_norm, ffn2_w1, ffn2_w3, ffn2_w2, rel_bias):
    batch, seq_len, d = x.shape
    n_mem = mem.shape[1]
    depth = w_in.shape[0]
    fox_w = FOX_HEADS * HEAD_DIM
    moba_w = MOBA_HEADS * HEAD_DIM
    mem_w = MEM_HEADS * HEAD_DIM
    m = batch * seq_len
    tm = _tile(seq_len, 1024)
    tm_out = _tile(seq_len, 512)
    ones = jnp.ones((HEAD_DIM,), F32)
    q_scale = HEAD_DIM ** -0.5 * LOG2E

    x2d = x.reshape(m, d)
    mem2d = mem.reshape(batch * n_mem, d)
    for l in range(depth):
        x2d = _ffn(x2d, ffn1_norm[l], ffn1_w1[l].astype(BF16), ffn1_w3[l].astype(BF16),
                   ffn1_w2[l].astype(BF16), tm=tm, tf=_tile(ffn1_w1.shape[2], 512))

        f0 = 3 * fox_w
        w_bf = w_in[l].astype(BF16)
        w_main = jnp.concatenate([w_bf[:, :f0], w_bf[:, f0 + FOX_HEADS:]], axis=1)
        w_forget = jnp.pad(w_bf[:, f0:f0 + FOX_HEADS], ((0, 0), (0, LANES - FOX_HEADS)))
        b_pad = jnp.pad(b_forget[l].astype(F32), (0, LANES - FOX_HEADS)).reshape(1, LANES)
        head_gain = jnp.concatenate(
            [jnp.tile(fox_q_gain[l] * q_scale, FOX_HEADS), jnp.tile(fox_k_gain[l], FOX_HEADS),
             jnp.tile(ones, FOX_HEADS), jnp.tile(moba_q_gain[l] * q_scale, MOBA_HEADS),
             jnp.tile(moba_k_gain[l], MOBA_HEADS), jnp.tile(ones, MOBA_HEADS),
             jnp.tile(mem_q_gain[l] * q_scale, MEM_HEADS)]).astype(F32)
        head_flag = jnp.concatenate(
            [jnp.ones((2 * fox_w,), F32), jnp.zeros((fox_w,), F32), jnp.ones((2 * moba_w,), F32),
             jnp.zeros((moba_w,), F32), jnp.ones((mem_w,), F32)])
        proj, cum = _norm_proj(x2d, mix_norm[l], w_main, head_gain, head_flag, tm=tm, tn=1024,
                               forget=(w_forget, b_pad), seq_len=seq_len)

        kv_gain = jnp.concatenate([jnp.tile(mem_k_gain[l], MEM_HEADS), jnp.tile(ones, MEM_HEADS)])
        kv_flag = jnp.concatenate([jnp.ones((mem_w,), F32), jnp.zeros((mem_w,), F32)])
        mkv = _norm_proj(mem2d, mem_norm[l], w_mem_kv[l].astype(BF16), kv_gain.astype(F32), kv_flag,
                         tm=_tile(batch * n_mem, 512), tn=512)

        o_fox = _fox_attn(proj, cum, batch=batch, seq_len=seq_len)
        o_moba = _moba_attn(proj, rel_bias.astype(F32), batch=batch, seq_len=seq_len,
                            col0=3 * FOX_HEADS)
        o_mem = _mem_attn(proj, mkv, batch=batch, seq_len=seq_len, n_mem=n_mem,
                          col0=3 * FOX_HEADS + 3 * MOBA_HEADS)
        x2d = _out_proj(x2d, o_fox, o_moba, o_mem, w_out[l].astype(BF16), tm=tm_out)

        x2d = _ffn(x2d, ffn2_norm[l], ffn2_w1[l].astype(BF16), ffn2_w3[l].astype(BF16),
                   ffn2_w2[l].astype(BF16), tm=tm, tf=_tile(ffn2_w1.shape[2], 512))
    return x2d.reshape(batch, seq_len, d)
```

```python
import functools
import math

import jax
import jax.numpy as jnp
from jax import lax
from jax.experimental import pallas as pl
from jax.experimental.pallas import tpu as pltpu

HEAD_DIM = 128
FOX_HEADS = 8
MOBA_HEADS = 4
MEM_HEADS = 4
MOBA_BLOCK = 256
MOBA_TOPK = 3
REL_BUCKETS = 32
REL_MAX_DIST = 128
EPS = 1e-6
NEG = -1e30
LOG2E = math.log2(math.e)

LANES = 128
BF16_SUBLANES = 16
MXU_WIDTH = 256
ATTN_Q_BLOCK = 256
VMEM_LIMIT_BYTES = 60 * 1024 * 1024

F32 = jnp.float32
BF16 = jnp.bfloat16


def _rms_scale(x):
    return lax.rsqrt(jnp.mean(x * x, axis=-1, keepdims=True) + EPS)


def _dot(a, b):
    return jnp.dot(a, b, preferred_element_type=F32)


def _dot_nt(a, b):
    return lax.dot_general(a, b, (((1,), (1,)), ((), ())), preferred_element_type=F32)


def _split3(x):
    hi = x.astype(BF16)
    r1 = x - hi.astype(F32)
    mid = r1.astype(BF16)
    lo = (r1 - mid.astype(F32)).astype(BF16)
    return hi, mid, lo


def _softmax_pv(logits2, v):
    mx = jnp.max(logits2, axis=-1, keepdims=True)
    p = jnp.exp2(logits2 - mx)
    denom = jnp.sum(p, axis=-1, keepdims=True)
    return _dot(p.astype(BF16), v) / denom


class _SideJob:
    def __init__(self, inputs, outputs, body):
        self.inputs, self.outputs, self.body = inputs, outputs, body


def _ffn_kernel(*refs, jobs):
    x_ref, g_ref, w1_ref, w3_ref, w2_ref = refs[:5]
    n_side_in = sum(len(j.inputs) for j in jobs)
    side_in = refs[5:5 + n_side_in]
    o_ref = refs[5 + n_side_in]
    side_out = refs[6 + n_side_in:-1]
    xn_ref = refs[-1]

    @pl.when(pl.program_id(1) == 0)
    def _():
        x = x_ref[...]
        xn_ref[...] = (x * _rms_scale(x) * g_ref[...]).astype(BF16)
        o_ref[...] = x

    xn = xn_ref[...]
    h1 = _dot(xn, w1_ref[...])
    h3 = _dot(xn, w3_ref[...])
    act = (0.5 * h1 * jax.nn.sigmoid(h1) * h3).astype(BF16)
    o_ref[...] += _dot(act, w2_ref[...])

    for job in jobs:
        ins, side_in = side_in[:len(job.inputs)], side_in[len(job.inputs):]
        outs, side_out = side_out[:len(job.outputs)], side_out[len(job.outputs):]
        job.body(ins, outs)


def _ffn(x2d, gain, w1, w3, w2, *, tm, tf, jobs=()):
    m, d = x2d.shape
    d_ff = w1.shape[1]
    side_in = [io for j in jobs for io in j.inputs]
    side_out = [io for j in jobs for io in j.outputs]
    res = pl.pallas_call(
        functools.partial(_ffn_kernel, jobs=jobs),
        grid=(m // tm, d_ff // tf),
        in_specs=[
            pl.BlockSpec((tm, d), lambda i, f: (i, 0)),
            pl.BlockSpec((1, d), lambda i, f: (0, 0)),
            pl.BlockSpec((d, tf), lambda i, f: (0, f)),
            pl.BlockSpec((d, tf), lambda i, f: (0, f)),
            pl.BlockSpec((tf, d), lambda i, f: (f, 0)),
        ] + [spec for _, spec in side_in],
        out_specs=[pl.BlockSpec((tm, d), lambda i, f: (i, 0))] + [spec for _, spec in side_out],
        out_shape=[jax.ShapeDtypeStruct((m, d), F32)] + [struct for struct, _ in side_out],
        scratch_shapes=[pltpu.VMEM((tm, d), BF16)],
        compiler_params=pltpu.CompilerParams(
            dimension_semantics=("arbitrary", "arbitrary"),
            vmem_limit_bytes=VMEM_LIMIT_BYTES),
        name="ffn",
    )(x2d, gain.reshape(1, d), w1, w3, w2, *[arr for arr, _ in side_in])
    return res[0], res[1:]


def _cast_job(w, block, index_map):
    def body(ins, outs):
        outs[0][...] = ins[0][...].astype(BF16)
    spec = pl.BlockSpec(block, index_map)
    return _SideJob([(w, spec)], [(jax.ShapeDtypeStruct(w.shape, BF16), spec)], body)


def _row_slabs(nrows, n_f, n_steps):
    rows = BF16_SUBLANES
    while nrows % rows or nrows // rows > n_steps:
        rows += BF16_SUBLANES
    last = nrows // rows - 1
    return rows, lambda i, f: (jnp.minimum(i * n_f + f, last), 0)


def _w_in_job(w_in, n_f, n_steps):
    d, in_w = w_in.shape
    f0 = 3 * FOX_HEADS * HEAD_DIM
    rows, row_map = _row_slabs(d, n_f, n_steps)

    def body(ins, outs):
        w = ins[0][...]
        outs[0][...] = jnp.concatenate([w[:, :f0], w[:, f0 + FOX_HEADS:]], axis=1).astype(BF16)
        lanes = lax.broadcasted_iota(jnp.int32, (rows, LANES), 1)
        outs[1][...] = jnp.where(lanes < FOX_HEADS, w[:, f0:f0 + LANES], 0.0).astype(BF16)

    return _SideJob(
        [(w_in, pl.BlockSpec((rows, in_w), row_map))],
        [(jax.ShapeDtypeStruct((d, in_w - FOX_HEADS), BF16),
          pl.BlockSpec((rows, in_w - FOX_HEADS), row_map)),
         (jax.ShapeDtypeStruct((d, LANES), BF16), pl.BlockSpec((rows, LANES), row_map))],
        body)


def _project_heads(xn_ref, w_ref, hg_ref, hflag_ref, p_ref):
    xn = xn_ref[...]
    for sb in range(w_ref.shape[1] // MXU_WIDTH):
        y = _dot(xn, w_ref[:, sb * MXU_WIDTH:(sb + 1) * MXU_WIDTH])
        for hh in range(MXU_WIDTH // HEAD_DIM):
            cols = slice(sb * MXU_WIDTH + hh * HEAD_DIM, sb * MXU_WIDTH + (hh + 1) * HEAD_DIM)
            yh = y[:, hh * HEAD_DIM:(hh + 1) * HEAD_DIM]
            normed = yh * _rms_scale(yh) * hg_ref[:, cols]
            p_ref[:, cols] = jnp.where(hflag_ref[:, cols] > 0.0, normed, yh).astype(BF16)


def _proj_kernel(x_ref, g_ref, w_ref, hg_ref, hflag_ref, p_ref, xn_ref):
    @pl.when(pl.program_id(1) == 0)
    def _():
        x = x_ref[...]
        xn_ref[...] = (x * _rms_scale(x) * g_ref[...]).astype(BF16)

    _project_heads(xn_ref, w_ref, hg_ref, hflag_ref, p_ref)


def _proj_forget_kernel(x_ref, g_ref, w_ref, hg_ref, hflag_ref, wf_ref, bf_ref,
                        p_ref, c_ref, xn_ref, carry_ref, *, tiles_per_seq, cum_block):
    i = pl.program_id(0)

    @pl.when(pl.program_id(1) == 0)
    def _():
        x = x_ref[...]
        xn_ref[...] = (x * _rms_scale(x) * g_ref[...]).astype(BF16)

        @pl.when(i % tiles_per_seq == 0)
        def _():
            carry_ref[...] = jnp.zeros_like(carry_ref)

        z = _dot(xn_ref[...], wf_ref[...]) + bf_ref[...]
        logf = jnp.minimum(z, 0.0) - jnp.log1p(jnp.exp(-jnp.abs(z)))
        r = lax.broadcasted_iota(jnp.int32, (cum_block, cum_block), 0)
        c = lax.broadcasted_iota(jnp.int32, (cum_block, cum_block), 1)
        tril = (c <= r).astype(BF16)
        local = []
        for blk in range(logf.shape[0] // cum_block):
            hi, mid, lo = _split3(logf[blk * cum_block:(blk + 1) * cum_block, :])
            local.append(_dot(tril, hi) + _dot(tril, mid) + _dot(tril, lo))
        carry = carry_ref[...]
        for blk, loc in enumerate(local):
            cum = loc + carry
            c_ref[blk * cum_block:(blk + 1) * cum_block, :] = cum
            carry = cum[cum_block - 1:cum_block, :]
        carry_ref[...] = carry

    _project_heads(xn_ref, w_ref, hg_ref, hflag_ref, p_ref)


def _norm_proj(x2d, gain, w, head_gain, head_flag, *, tm, tn, forget=None, seq_len=None):
    m, d = x2d.shape
    n = w.shape[1]
    in_specs = [
        pl.BlockSpec((tm, d), lambda i, j: (i, 0)),
        pl.BlockSpec((1, d), lambda i, j: (0, 0)),
        pl.BlockSpec((d, tn), lambda i, j: (0, j)),
        pl.BlockSpec((1, tn), lambda i, j: (0, j)),
        pl.BlockSpec((1, tn), lambda i, j: (0, j)),
    ]
    args = [x2d, gain.reshape(1, d), w, head_gain.reshape(1, n), head_flag.reshape(1, n)]
    p_spec = pl.BlockSpec((tm, tn), lambda i, j: (i, j))
    p_shape = jax.ShapeDtypeStruct((m, n), BF16)
    params = pltpu.CompilerParams(dimension_semantics=("arbitrary", "arbitrary"),
                                  vmem_limit_bytes=VMEM_LIMIT_BYTES)
    if forget is None:
        return pl.pallas_call(
            _proj_kernel, grid=(m // tm, n // tn), in_specs=in_specs, out_specs=p_spec,
            out_shape=p_shape, scratch_shapes=[pltpu.VMEM((tm, d), BF16)],
            compiler_params=params, name="mem_proj")(*args)
    wf, bf = forget
    in_specs += [pl.BlockSpec((d, LANES), lambda i, j: (0, 0)),
                 pl.BlockSpec((1, LANES), lambda i, j: (0, 0))]
    kern = functools.partial(_proj_forget_kernel, tiles_per_seq=seq_len // tm,
                             cum_block=min(tm, 256))
    return pl.pallas_call(
        kern, grid=(m // tm, n // tn), in_specs=in_specs,
        out_specs=[p_spec, pl.BlockSpec((tm, LANES), lambda i, j: (i, 0))],
        out_shape=[p_shape, jax.ShapeDtypeStruct((m, LANES), F32)],
        scratch_shapes=[pltpu.VMEM((tm, d), BF16), pltpu.VMEM((1, LANES), F32)],
        compiler_params=params, name="mix_proj")(*args, wf, bf)


def _lane_column(x, lane):
    lanes = lax.broadcasted_iota(jnp.int32, x.shape, 1)
    return jnp.sum(jnp.where(lanes == lane, x, 0.0), axis=-1, keepdims=True)


def _fox_kernel(q_ref, k_ref, v_ref, c_ref, o_ref, crow_ref, *, seq_len):
    h = pl.program_id(1)
    tq = ATTN_Q_BLOCK
    nq = seq_len // tq
    r = lax.broadcasted_iota(jnp.int32, (tq, tq), 0)
    c = lax.broadcasted_iota(jnp.int32, (tq, tq), 1)
    diag = r == c
    causal = c <= r

    for j in range(nq):
        cj = _lane_column(c_ref[j * tq:(j + 1) * tq, :], h) * LOG2E
        crow_ref[:, j * tq:(j + 1) * tq] = jnp.sum(jnp.where(diag, cj, 0.0), axis=0, keepdims=True)

    def qk(i):
        return _dot_nt(q_ref[i * tq:(i + 1) * tq, :], k_ref[0:(i + 1) * tq, :])

    s_next = qk(0)
    for i in range(nq):
        n = (i + 1) * tq
        s = s_next
        if i + 1 < nq:
            s_next = qk(i + 1)
        c_col = _lane_column(c_ref[i * tq:(i + 1) * tq, :], h) * LOG2E
        logits = s + (c_col - crow_ref[:, 0:n])
        own = jnp.where(causal, logits[:, i * tq:], NEG)
        if i > 0:
            logits = jnp.concatenate([logits[:, :i * tq], own], axis=1)
        else:
            logits = own
        o_ref[i * tq:(i + 1) * tq, :] = _softmax_pv(logits, v_ref[0:n, :]).astype(BF16)


def _fox_attn(p, cum, *, batch, seq_len):
    m = p.shape[0]
    hd = HEAD_DIM
    return pl.pallas_call(
        functools.partial(_fox_kernel, seq_len=seq_len),
        grid=(batch, FOX_HEADS),
        in_specs=[
            pl.BlockSpec((seq_len, hd), lambda b, h: (b, h)),
            pl.BlockSpec((seq_len, hd), lambda b, h: (b, FOX_HEADS + h)),
            pl.BlockSpec((seq_len, hd), lambda b, h: (b, 2 * FOX_HEADS + h)),
            pl.BlockSpec((seq_len, LANES), lambda b, h: (b, 0)),
        ],
        out_specs=pl.BlockSpec((seq_len, hd), lambda b, h: (b, h)),
        out_shape=jax.ShapeDtypeStruct((m, FOX_HEADS * hd), BF16),
        scratch_shapes=[pltpu.VMEM((1, seq_len), F32)],
        compiler_params=pltpu.CompilerParams(
            dimension_semantics=("arbitrary", "arbitrary"),
            vmem_limit_bytes=VMEM_LIMIT_BYTES),
        name="fox_attn",
    )(p, p, p, cum)


def _t5_bias(dist, rel_ref, h):
    n = jnp.maximum(dist, 0)
    max_exact = REL_BUCKETS // 2
    nf = jnp.maximum(n, 1).astype(F32)
    large = max_exact + (jnp.log(nf / max_exact) / math.log(REL_MAX_DIST / max_exact)
                         * (REL_BUCKETS - max_exact)).astype(jnp.int32)
    large = jnp.minimum(large, REL_BUCKETS - 1)
    bucket = jnp.where(n < max_exact, n, large)
    bias = jnp.zeros(dist.shape, F32)
    for b in range(REL_BUCKETS):
        bias = jnp.where(bucket == b, rel_ref[b, h], bias)
    return bias


def _moba_kernel(rel_ref, q_ref, k_ref, v_ref, o_ref, town_ref, tprev_ref, kaug_ref, qaug_ref,
                 kmean_ref, *, seq_len):
    h = pl.program_id(0)
    blk = MOBA_BLOCK
    nb = seq_len // blk
    r = lax.broadcasted_iota(jnp.int32, (blk, blk), 0)
    c = lax.broadcasted_iota(jnp.int32, (blk, blk), 1)
    lane_grp = lax.broadcasted_iota(jnp.int32, (1, LANES), 1) >> 3

    @pl.when(pl.program_id(1) == 0)
    def _():
        town_ref[...] = jnp.where(c <= r, _t5_bias(r - c, rel_ref, h) * LOG2E, NEG)
        tprev_ref[...] = _t5_bias(r - c + blk, rel_ref, h) * LOG2E
        s = lax.broadcasted_iota(jnp.int32, (seq_len, LANES), 0)
        ln = lax.broadcasted_iota(jnp.int32, (seq_len, LANES), 1)
        first_key = (ln & 7) * blk
        onehot = (ln < 32) & (s >= first_key) & (s < first_key + blk)
        kaug_ref[:, HEAD_DIM:] = jnp.where(onehot, 1.0, 0.0).astype(BF16)

    kaug_ref[:, :HEAD_DIM] = k_ref[...]

    kmean_ref[...] = jnp.zeros_like(kmean_ref)
    for j in range(nb):
        kmean_ref[j:j + 1, :] = jnp.mean(k_ref[j * blk:(j + 1) * blk, :].astype(F32),
                                         axis=0, keepdims=True)
    km_hi, km_mid, km_lo = _split3(kmean_ref[...])

    far = jnp.full((1, LANES), rel_ref[REL_BUCKETS - 1, h] * LOG2E, F32)
    far_hi, far_mid, far_lo = _split3(far)
    far_parts = jnp.where(lane_grp == 1, far_hi.astype(F32),
                          jnp.where(lane_grp == 2, far_mid.astype(F32),
                                    jnp.where(lane_grp == 3, far_lo.astype(F32), 0.0)))

    def far_bias(row0, nrows):
        rows = row0 + lax.broadcasted_iota(jnp.int32, (nrows, LANES), 0)
        lanes = lax.broadcasted_iota(jnp.int32, (nrows, LANES), 1)
        own = rows >> (blk.bit_length() - 1)
        return jnp.where((lanes & 7) <= own - 2, far_parts, 0.0), lanes, own

    qaug_ref[:, :HEAD_DIM] = q_ref[...]
    late = min(seq_len, (MOBA_TOPK + 1) * blk)
    qaug_ref[:late, HEAD_DIM:] = far_bias(0, late)[0].astype(BF16)
    if seq_len > late:
        ql = q_ref[late:, :]
        gate = _dot_nt(ql, km_hi) + _dot_nt(ql, km_mid) + _dot_nt(ql, km_lo)
        aug, lanes_l, own_l = far_bias(late, seq_len - late)
        valid = lanes_l < own_l
        g = jnp.where(valid, gate, NEG)
        sel = jnp.zeros(g.shape, jnp.bool_)
        for _ in range(MOBA_TOPK):
            best = jnp.max(g, axis=-1, keepdims=True)
            first = jnp.min(jnp.where(g == best, lanes_l, LANES), axis=-1, keepdims=True)
            pick = lanes_l == first
            sel = sel | pick
            g = jnp.where(pick, -jnp.inf, g)
        qaug_ref[late:, HEAD_DIM:] = jnp.where(valid & jnp.logical_not(sel), NEG, aug).astype(BF16)

    def qk(i):
        return _dot_nt(qaug_ref[i * blk:(i + 1) * blk, :], kaug_ref[0:(i + 1) * blk, :])

    s_next = qk(0)
    for i in range(nb):
        n = (i + 1) * blk
        s = s_next
        if i + 1 < nb:
            s_next = qk(i + 1)
        pieces = [s[:, i * blk:] + town_ref[...]]
        if i >= 1:
            pieces.insert(0, s[:, (i - 1) * blk:i * blk] + tprev_ref[...])
        if i >= 2:
            pieces.insert(0, s[:, :(i - 1) * blk])
        logits = jnp.concatenate(pieces, axis=1) if len(pieces) > 1 else pieces[0]
        o_ref[i * blk:(i + 1) * blk, :] = _softmax_pv(logits, v_ref[0:n, :]).astype(BF16)


def _moba_attn(p, rel_bias, *, batch, seq_len, col0):
    m = p.shape[0]
    hd = HEAD_DIM
    blk = MOBA_BLOCK
    assert seq_len % blk == 0 and seq_len // blk <= 8
    return pl.pallas_call(
        functools.partial(_moba_kernel, seq_len=seq_len),
        grid=(MOBA_HEADS, batch),
        in_specs=[
            pl.BlockSpec(memory_space=pltpu.SMEM),
            pl.BlockSpec((seq_len, hd), lambda h, b: (b, col0 + h)),
            pl.BlockSpec((seq_len, hd), lambda h, b: (b, col0 + MOBA_HEADS + h)),
            pl.BlockSpec((seq_len, hd), lambda h, b: (b, col0 + 2 * MOBA_HEADS + h)),
        ],
        out_specs=pl.BlockSpec((seq_len, hd), lambda h, b: (b, h)),
        out_shape=jax.ShapeDtypeStruct((m, MOBA_HEADS * hd), BF16),
        scratch_shapes=[
            pltpu.VMEM((blk, blk), F32),
            pltpu.VMEM((blk, blk), F32),
            pltpu.VMEM((seq_len, hd + LANES), BF16),
            pltpu.VMEM((seq_len, hd + LANES), BF16),
            pltpu.VMEM((LANES, hd), F32),
        ],
        compiler_params=pltpu.CompilerParams(
            dimension_semantics=("arbitrary", "arbitrary"),
            vmem_limit_bytes=VMEM_LIMIT_BYTES),
        name="moba_attn",
    )(rel_bias, p, p, p)


def _mem_kernel(q_ref, k_ref, v_ref, o_ref, *, seq_len, tq):
    k = k_ref[...]
    v = v_ref[...]
    for i in range(seq_len // tq):
        rows = slice(i * tq, (i + 1) * tq)
        o_ref[rows, :] = _softmax_pv(_dot_nt(q_ref[rows, :], k), v).astype(BF16)


def _mem_attn(p, mkv, *, batch, seq_len, n_mem, col0):
    m = p.shape[0]
    hd = HEAD_DIM
    return pl.pallas_call(
        functools.partial(_mem_kernel, seq_len=seq_len, tq=min(seq_len, 512)),
        grid=(batch, MEM_HEADS),
        in_specs=[
            pl.BlockSpec((seq_len, hd), lambda b, h: (b, col0 + h)),
            pl.BlockSpec((n_mem, hd), lambda b, h: (b, h)),
            pl.BlockSpec((n_mem, hd), lambda b, h: (b, MEM_HEADS + h)),
        ],
        out_specs=pl.BlockSpec((seq_len, hd), lambda b, h: (b, h)),
        out_shape=jax.ShapeDtypeStruct((m, MEM_HEADS * hd), BF16),
        compiler_params=pltpu.CompilerParams(
            dimension_semantics=("arbitrary", "arbitrary"),
            vmem_limit_bytes=VMEM_LIMIT_BYTES),
        name="mem_attn",
    )(p, mkv, mkv)


def _out_proj_kernel(x_ref, of_ref, ob_ref, om_ref, w_ref, o_ref):
    wf = of_ref.shape[1]
    wb = ob_ref.shape[1]
    acc = _dot(of_ref[...], w_ref[0:wf, :])
    acc += _dot(ob_ref[...], w_ref[wf:wf + wb, :])
    acc += _dot(om_ref[...], w_ref[wf + wb:, :])
    o_ref[...] = x_ref[...] + acc


def _out_proj(x2d, o_fox, o_moba, o_mem, w_out, *, tm):
    m, d = x2d.shape
    return pl.pallas_call(
        _out_proj_kernel,
        grid=(m // tm,),
        in_specs=[
            pl.BlockSpec((tm, d), lambda i: (i, 0)),
            pl.BlockSpec((tm, o_fox.shape[1]), lambda i: (i, 0)),
            pl.BlockSpec((tm, o_moba.shape[1]), lambda i: (i, 0)),
            pl.BlockSpec((tm, o_mem.shape[1]), lambda i: (i, 0)),
            pl.BlockSpec(w_out.shape, lambda i: (0, 0)),
        ],
        out_specs=pl.BlockSpec((tm, d), lambda i: (i, 0)),
        out_shape=jax.ShapeDtypeStruct((m, d), F32),
        compiler_params=pltpu.CompilerParams(
            dimension_semantics=("arbitrary",), vmem_limit_bytes=VMEM_LIMIT_BYTES),
        name="out_proj",
    )(x2d, o_fox, o_moba, o_mem, w_out)


def _tile(total, want):
    t = min(total, want)
    assert total % t == 0, (total, want)
    return t


def kernel(x, mem, ffn1_norm, ffn1_w1, ffn1_w3, ffn1_w2, mix_norm, mem_norm, w_in, b_forget,
           w_mem_kv, fox_q_gain, fox_k_gain, moba_q_gain, moba_k_gain, mem_q_gain, mem_k_gain,
           w_out, ffn2_norm, ffn2_w1, ffn2_w3, ffn2_w2, rel_bias):
    batch, seq_len, d = x.shape
    n_mem = mem.shape[1]
    depth = w_in.shape[0]
    fox_w = FOX_HEADS * HEAD_DIM
    moba_w = MOBA_HEADS * HEAD_DIM
    mem_w = MEM_HEADS * HEAD_DIM
    m = batch * seq_len
    tm = _tile(seq_len, 1024)
    tm_out = _tile(seq_len, 512)
    ones = jnp.ones((HEAD_DIM,), F32)
    q_scale = HEAD_DIM ** -0.5 * LOG2E

    x2d = x.reshape(m, d)
    mem2d = mem.reshape(batch * n_mem, d)
    for l in range(depth):
        tf = _tile(ffn1_w1.shape[2], 512)
        n_i, n_f = m // tm, ffn1_w1.shape[2] // tf
        d_ff2 = ffn2_w1.shape[2]
        rows, ff_cols = d // n_i, d_ff2 // n_f
        n_steps = n_i * n_f
        jobs = (
            _cast_job(ffn2_w1[l], (rows, ff_cols), lambda i, f: (i, f)),
            _cast_job(ffn2_w3[l], (rows, ff_cols), lambda i, f: (i, f)),
            _cast_job(ffn2_w2[l], (ff_cols, rows), lambda i, f: (f, i)),
            _cast_job(w_out[l], (_row_slabs(w_out.shape[1], n_f, n_steps)[0], d),
                      _row_slabs(w_out.shape[1], n_f, n_steps)[1]),
            _cast_job(w_mem_kv[l], (_row_slabs(d, n_f, n_steps)[0], w_mem_kv.shape[2]),
                      _row_slabs(d, n_f, n_steps)[1]),
            _w_in_job(w_in[l], n_f, n_steps),
        )
        x2d, (w1_2, w3_2, w2_2, w_out_bf, w_mem_bf, w_main, w_forget) = _ffn(
            x2d, ffn1_norm[l], ffn1_w1[l].astype(BF16), ffn1_w3[l].astype(BF16),
            ffn1_w2[l].astype(BF16), tm=tm, tf=tf, jobs=jobs)

        b_pad = jnp.pad(b_forget[l].astype(F32), (0, LANES - FOX_HEADS)).reshape(1, LANES)
        head_gain = jnp.concatenate(
            [jnp.tile(fox_q_gain[l] * q_scale, FOX_HEADS), jnp.tile(fox_k_gain[l], FOX_HEADS),
             jnp.tile(ones, FOX_HEADS), jnp.tile(moba_q_gain[l] * q_scale, MOBA_HEADS),
             jnp.tile(moba_k_gain[l], MOBA_HEADS), jnp.tile(ones, MOBA_HEADS),
             jnp.tile(mem_q_gain[l] * q_scale, MEM_HEADS)]).astype(F32)
        head_flag = jnp.concatenate(
            [jnp.ones((2 * fox_w,), F32), jnp.zeros((fox_w,), F32), jnp.ones((2 * moba_w,), F32),
             jnp.zeros((moba_w,), F32), jnp.ones((mem_w,), F32)])
        proj, cum = _norm_proj(x2d, mix_norm[l], w_main, head_gain, head_flag, tm=tm, tn=1024,
                               forget=(w_forget, b_pad), seq_len=seq_len)

        kv_gain = jnp.concatenate([jnp.tile(mem_k_gain[l], MEM_HEADS), jnp.tile(ones, MEM_HEADS)])
        kv_flag = jnp.concatenate([jnp.ones((mem_w,), F32), jnp.zeros((mem_w,), F32)])
        mkv = _norm_proj(mem2d, mem_norm[l], w_mem_bf, kv_gain.astype(F32), kv_flag,
                         tm=_tile(batch * n_mem, 512), tn=512)

        o_fox = _fox_attn(proj, cum, batch=batch, seq_len=seq_len)
        o_moba = _moba_attn(proj, rel_bias.astype(F32), batch=batch, seq_len=seq_len,
                            col0=3 * FOX_HEADS)
        o_mem = _mem_attn(proj, mkv, batch=batch, seq_len=seq_len, n_mem=n_mem,
                          col0=3 * FOX_HEADS + 3 * MOBA_HEADS)
        x2d = _out_proj(x2d, o_fox, o_moba, o_mem, w_out_bf, tm=tm_out)

        x2d, _ = _ffn(x2d, ffn2_norm[l], w1_2, w3_2, w2_2, tm=tm, tf=_tile(d_ff2, 512))
    return x2d.reshape(batch, seq_len, d)
```

```python
import functools
import math

import jax
import jax.numpy as jnp
from jax import lax
from jax.experimental import pallas as pl
from jax.experimental.pallas import tpu as pltpu

HEAD_DIM = 128
FOX_HEADS = 8
MOBA_HEADS = 4
MEM_HEADS = 4
MOBA_BLOCK = 256
MOBA_TOPK = 3
REL_BUCKETS = 32
REL_MAX_DIST = 128
EPS = 1e-6
NEG = -1e30
LOG2E = math.log2(math.e)

LANES = 128
BF16_SUBLANES = 16
MXU_WIDTH = 256
ATTN_Q_BLOCK = 256
VMEM_LIMIT_BYTES = 60 * 1024 * 1024

F32 = jnp.float32
BF16 = jnp.bfloat16


def _rms_scale(x):
    return lax.rsqrt(jnp.mean(x * x, axis=-1, keepdims=True) + EPS)


def _dot(a, b):
    return jnp.dot(a, b, preferred_element_type=F32)


def _dot_nt(a, b):
    return lax.dot_general(a, b, (((1,), (1,)), ((), ())), preferred_element_type=F32)


def _split3(x):
    hi = x.astype(BF16)
    r1 = x - hi.astype(F32)
    mid = r1.astype(BF16)
    lo = (r1 - mid.astype(F32)).astype(BF16)
    return hi, mid, lo


def _softmax_pv(logits2, v):
    mx = jnp.max(logits2, axis=-1, keepdims=True)
    p = jnp.exp2(logits2 - mx)
    denom = jnp.sum(p, axis=-1, keepdims=True)
    return _dot(p.astype(BF16), v) / denom


class _SideJob:
    def __init__(self, inputs, outputs, body):
        self.inputs, self.outputs, self.body = inputs, outputs, body


def _ffn_kernel(*refs, jobs):
    x_ref, g_ref, w1_ref, w3_ref, w2_ref = refs[:5]
    n_side_in = sum(len(j.inputs) for j in jobs)
    side_in = refs[5:5 + n_side_in]
    o_ref = refs[5 + n_side_in]
    side_out = refs[6 + n_side_in:-1]
    xn_ref = refs[-1]

    @pl.when(pl.program_id(1) == 0)
    def _():
        x = x_ref[...]
        xn_ref[...] = (x * _rms_scale(x) * g_ref[...]).astype(BF16)
        o_ref[...] = x

    xn = xn_ref[...]
    h1 = _dot(xn, w1_ref[...])
    h3 = _dot(xn, w3_ref[...])
    act = (0.5 * h1 * jax.nn.sigmoid(h1) * h3).astype(BF16)
    o_ref[...] += _dot(act, w2_ref[...])

    for job in jobs:
        ins, side_in = side_in[:len(job.inputs)], side_in[len(job.inputs):]
        outs, side_out = side_out[:len(job.outputs)], side_out[len(job.outputs):]
        job.body(ins, outs)


def _ffn(x2d, gain, w1, w3, w2, *, tm, tf, jobs=()):
    m, d = x2d.shape
    d_ff = w1.shape[1]
    side_in = [io for j in jobs for io in j.inputs]
    side_out = [io for j in jobs for io in j.outputs]
    res = pl.pallas_call(
        functools.partial(_ffn_kernel, jobs=jobs),
        grid=(m // tm, d_ff // tf),
        in_specs=[
            pl.BlockSpec((tm, d), lambda i, f: (i, 0)),
            pl.BlockSpec((1, d), lambda i, f: (0, 0)),
            pl.BlockSpec((d, tf), lambda i, f: (0, f)),
            pl.BlockSpec((d, tf), lambda i, f: (0, f)),
            pl.BlockSpec((tf, d), lambda i, f: (f, 0)),
        ] + [spec for _, spec in side_in],
        out_specs=[pl.BlockSpec((tm, d), lambda i, f: (i, 0))] + [spec for _, spec in side_out],
        out_shape=[jax.ShapeDtypeStruct((m, d), F32)] + [struct for struct, _ in side_out],
        scratch_shapes=[pltpu.VMEM((tm, d), BF16)],
        compiler_params=pltpu.CompilerParams(
            dimension_semantics=("arbitrary", "arbitrary"),
            vmem_limit_bytes=VMEM_LIMIT_BYTES),
        name="ffn",
    )(x2d, gain.reshape(1, d), w1, w3, w2, *[arr for arr, _ in side_in])
    return res[0], res[1:]


def _cast_job(w, block, index_map):
    def body(ins, outs):
        outs[0][...] = ins[0][...].astype(BF16)
    spec = pl.BlockSpec(block, index_map)
    return _SideJob([(w, spec)], [(jax.ShapeDtypeStruct(w.shape, BF16), spec)], body)


def _row_slabs(nrows, n_f, n_steps):
    rows = BF16_SUBLANES
    while nrows % rows or nrows // rows > n_steps:
        rows += BF16_SUBLANES
    last = nrows // rows - 1
    return rows, lambda i, f: (jnp.minimum(i * n_f + f, last), 0)


def _w_in_jobs(w_in_t, n_f, n_steps):
    in_w, d = w_in_t.shape
    f0 = 3 * FOX_HEADS * HEAD_DIM
    main_w = in_w - FOX_HEADS
    rows = BF16_SUBLANES
    while main_w % rows or f0 % rows or main_w // rows > n_steps:
        rows += BF16_SUBLANES
    last = main_w // rows - 1
    step = lambda i, f: jnp.minimum(i * n_f + f, last)
    per_row_block = rows // FOX_HEADS

    def main_body(ins, outs):
        t = jnp.minimum(pl.program_id(0) * n_f + pl.program_id(1), last)
        a = ins[0][...]
        shifted = jnp.concatenate([a[FOX_HEADS:], ins[1][...]], axis=0)
        outs[0][...] = jnp.where(t * rows >= f0, shifted, a).astype(BF16)

    def forget_body(ins, outs):
        pad = jnp.zeros((LANES - FOX_HEADS, d), F32)
        outs[0][...] = jnp.concatenate([ins[0][...], pad], axis=0).astype(BF16)

    main = _SideJob(
        [(w_in_t, pl.BlockSpec((rows, d), lambda i, f: (step(i, f), 0))),
         (w_in_t, pl.BlockSpec((FOX_HEADS, d), lambda i, f: ((step(i, f) + 1) * per_row_block, 0)))],
        [(jax.ShapeDtypeStruct((main_w, d), BF16),
          pl.BlockSpec((rows, d), lambda i, f: (step(i, f), 0)))],
        main_body)
    forget = _SideJob(
        [(w_in_t, pl.BlockSpec((FOX_HEADS, d), lambda i, f: (f0 // FOX_HEADS, 0)))],
        [(jax.ShapeDtypeStruct((LANES, d), BF16), pl.BlockSpec((LANES, d), lambda i, f: (0, 0)))],
        forget_body)
    return main, forget


def _project_heads(xn_ref, w_ref, hg_ref, hflag_ref, p_ref, w_transposed):
    xn = xn_ref[...]
    for sb in range(p_ref.shape[1] // MXU_WIDTH):
        if w_transposed:
            y = _dot_nt(xn, w_ref[sb * MXU_WIDTH:(sb + 1) * MXU_WIDTH, :])
        else:
            y = _dot(xn, w_ref[:, sb * MXU_WIDTH:(sb + 1) * MXU_WIDTH])
        for hh in range(MXU_WIDTH // HEAD_DIM):
            cols = slice(sb * MXU_WIDTH + hh * HEAD_DIM, sb * MXU_WIDTH + (hh + 1) * HEAD_DIM)
            yh = y[:, hh * HEAD_DIM:(hh + 1) * HEAD_DIM]
            normed = yh * _rms_scale(yh) * hg_ref[:, cols]
            p_ref[:, cols] = jnp.where(hflag_ref[:, cols] > 0.0, normed, yh).astype(BF16)


def _proj_kernel(x_ref, g_ref, w_ref, hg_ref, hflag_ref, p_ref, xn_ref):
    @pl.when(pl.program_id(1) == 0)
    def _():
        x = x_ref[...]
        xn_ref[...] = (x * _rms_scale(x) * g_ref[...]).astype(BF16)

    _project_heads(xn_ref, w_ref, hg_ref, hflag_ref, p_ref, w_transposed=False)


def _proj_forget_kernel(x_ref, g_ref, w_ref, hg_ref, hflag_ref, wf_ref, bf_ref,
                        p_ref, c_ref, xn_ref, carry_ref, *, tiles_per_seq, cum_block):
    i = pl.program_id(0)
    j = pl.program_id(1)

    @pl.when(j == 0)
    def _():
        x = x_ref[...]
        xn_ref[...] = (x * _rms_scale(x) * g_ref[...]).astype(BF16)

    def forget_gates():
        @pl.when(i % tiles_per_seq == 0)
        def _():
            carry_ref[...] = jnp.zeros_like(carry_ref)

        z = _dot_nt(xn_ref[...], wf_ref[...]) + bf_ref[...]
        logf = jnp.minimum(z, 0.0) - jnp.log1p(jnp.exp(-jnp.abs(z)))
        r = lax.broadcasted_iota(jnp.int32, (cum_block, cum_block), 0)
        c = lax.broadcasted_iota(jnp.int32, (cum_block, cum_block), 1)
        tril = (c <= r).astype(BF16)
        local = []
        for blk in range(logf.shape[0] // cum_block):
            hi, mid, lo = _split3(logf[blk * cum_block:(blk + 1) * cum_block, :])
            local.append(_dot(tril, hi) + _dot(tril, mid) + _dot(tril, lo))
        carry = carry_ref[...]
        for blk, loc in enumerate(local):
            cum = loc + carry
            c_ref[blk * cum_block:(blk + 1) * cum_block, :] = cum
            carry = cum[cum_block - 1:cum_block, :]
        carry_ref[...] = carry

    heads = functools.partial(_project_heads, xn_ref, w_ref, hg_ref, hflag_ref, p_ref,
                              w_transposed=True)

    @pl.when(j == 1)
    def _():
        forget_gates()
        heads()

    @pl.when(j != 1)
    def _():
        heads()


def _norm_proj(x2d, gain, w, head_gain, head_flag, *, tm, tn, forget=None, seq_len=None):
    m, d = x2d.shape
    n = w.shape[1] if forget is None else w.shape[0]
    w_spec = (pl.BlockSpec((d, tn), lambda i, j: (0, j)) if forget is None
              else pl.BlockSpec((tn, d), lambda i, j: (j, 0)))
    in_specs = [
        pl.BlockSpec((tm, d), lambda i, j: (i, 0)),
        pl.BlockSpec((1, d), lambda i, j: (0, 0)),
        w_spec,
        pl.BlockSpec((1, tn), lambda i, j: (0, j)),
        pl.BlockSpec((1, tn), lambda i, j: (0, j)),
    ]
    args = [x2d, gain.reshape(1, d), w, head_gain.reshape(1, n), head_flag.reshape(1, n)]
    p_spec = pl.BlockSpec((tm, tn), lambda i, j: (i, j))
    p_shape = jax.ShapeDtypeStruct((m, n), BF16)
    params = pltpu.CompilerParams(dimension_semantics=("arbitrary", "arbitrary"),
                                  vmem_limit_bytes=VMEM_LIMIT_BYTES)
    if forget is None:
        return pl.pallas_call(
            _proj_kernel, grid=(m // tm, n // tn), in_specs=in_specs, out_specs=p_spec,
            out_shape=p_shape, scratch_shapes=[pltpu.VMEM((tm, d), BF16)],
            compiler_params=params, name="mem_proj")(*args)
    wf, bf = forget
    assert n // tn >= 2
    in_specs += [pl.BlockSpec((LANES, d), lambda i, j: (0, 0)),
                 pl.BlockSpec((1, LANES), lambda i, j: (0, 0))]
    kern = functools.partial(_proj_forget_kernel, tiles_per_seq=seq_len // tm,
                             cum_block=min(tm, 256))
    return pl.pallas_call(
        kern, grid=(m // tm, n // tn), in_specs=in_specs,
        out_specs=[p_spec, pl.BlockSpec((tm, LANES), lambda i, j: (i, 0))],
        out_shape=[p_shape, jax.ShapeDtypeStruct((m, LANES), F32)],
        scratch_shapes=[pltpu.VMEM((tm, d), BF16), pltpu.VMEM((1, LANES), F32)],
        compiler_params=params, name="mix_proj")(*args, wf, bf)


def _lane_column(x, lane):
    lanes = lax.broadcasted_iota(jnp.int32, x.shape, 1)
    return jnp.sum(jnp.where(lanes == lane, x, 0.0), axis=-1, keepdims=True)


def _fox_kernel(q_ref, k_ref, v_ref, c_ref, o_ref, crow_ref, *, seq_len):
    h = pl.program_id(1)
    tq = ATTN_Q_BLOCK
    nq = seq_len // tq
    r = lax.broadcasted_iota(jnp.int32, (tq, tq), 0)
    c = lax.broadcasted_iota(jnp.int32, (tq, tq), 1)
    diag = r == c
    causal = c <= r

    for j in range(nq):
        cj = _lane_column(c_ref[j * tq:(j + 1) * tq, :], h) * LOG2E
        crow_ref[:, j * tq:(j + 1) * tq] = jnp.sum(jnp.where(diag, cj, 0.0), axis=0, keepdims=True)

    def qk(i):
        return _dot_nt(q_ref[i * tq:(i + 1) * tq, :], k_ref[0:(i + 1) * tq, :])

    s_next = qk(0)
    for i in range(nq):
        n = (i + 1) * tq
        s = s_next
        if i + 1 < nq:
            s_next = qk(i + 1)
        c_col = _lane_column(c_ref[i * tq:(i + 1) * tq, :], h) * LOG2E
        logits = s + (c_col - crow_ref[:, 0:n])
        own = jnp.where(causal, logits[:, i * tq:], NEG)
        if i > 0:
            logits = jnp.concatenate([logits[:, :i * tq], own], axis=1)
        else:
            logits = own
        o_ref[i * tq:(i + 1) * tq, :] = _softmax_pv(logits, v_ref[0:n, :]).astype(BF16)


def _fox_attn(p, cum, *, batch, seq_len):
    m = p.shape[0]
    hd = HEAD_DIM
    return pl.pallas_call(
        functools.partial(_fox_kernel, seq_len=seq_len),
        grid=(batch, FOX_HEADS),
        in_specs=[
            pl.BlockSpec((seq_len, hd), lambda b, h: (b, h)),
            pl.BlockSpec((seq_len, hd), lambda b, h: (b, FOX_HEADS + h)),
            pl.BlockSpec((seq_len, hd), lambda b, h: (b, 2 * FOX_HEADS + h)),
            pl.BlockSpec((seq_len, LANES), lambda b, h: (b, 0)),
        ],
        out_specs=pl.BlockSpec((seq_len, hd), lambda b, h: (b, h)),
        out_shape=jax.ShapeDtypeStruct((m, FOX_HEADS * hd), BF16),
        scratch_shapes=[pltpu.VMEM((1, seq_len), F32)],
        compiler_params=pltpu.CompilerParams(
            dimension_semantics=("arbitrary", "arbitrary"),
            vmem_limit_bytes=VMEM_LIMIT_BYTES),
        name="fox_attn",
    )(p, p, p, cum)


def _t5_bias(dist, rel_ref, h):
    n = jnp.maximum(dist, 0)
    max_exact = REL_BUCKETS // 2
    nf = jnp.maximum(n, 1).astype(F32)
    large = max_exact + (jnp.log(nf / max_exact) / math.log(REL_MAX_DIST / max_exact)
                         * (REL_BUCKETS - max_exact)).astype(jnp.int32)
    large = jnp.minimum(large, REL_BUCKETS - 1)
    bucket = jnp.where(n < max_exact, n, large)
    bias = jnp.zeros(dist.shape, F32)
    for b in range(REL_BUCKETS):
        bias = jnp.where(bucket == b, rel_ref[b, h], bias)
    return bias


def _moba_kernel(rel_ref, q_ref, k_ref, v_ref, o_ref, town_ref, tprev_ref, kaug_ref, qaug_ref,
                 kmean_ref, *, seq_len):
    h = pl.program_id(0)
    blk = MOBA_BLOCK
    nb = seq_len // blk
    r = lax.broadcasted_iota(jnp.int32, (blk, blk), 0)
    c = lax.broadcasted_iota(jnp.int32, (blk, blk), 1)
    lane_grp = lax.broadcasted_iota(jnp.int32, (1, LANES), 1) >> 3

    @pl.when(pl.program_id(1) == 0)
    def _():
        town_ref[...] = jnp.where(c <= r, _t5_bias(r - c, rel_ref, h) * LOG2E, NEG)
        tprev_ref[...] = _t5_bias(r - c + blk, rel_ref, h) * LOG2E
        s = lax.broadcasted_iota(jnp.int32, (seq_len, LANES), 0)
        ln = lax.broadcasted_iota(jnp.int32, (seq_len, LANES), 1)
        first_key = (ln & 7) * blk
        onehot = (ln < 32) & (s >= first_key) & (s < first_key + blk)
        kaug_ref[:, HEAD_DIM:] = jnp.where(onehot, 1.0, 0.0).astype(BF16)

    kaug_ref[:, :HEAD_DIM] = k_ref[...]

    kmean_ref[...] = jnp.zeros_like(kmean_ref)
    for j in range(nb):
        kmean_ref[j:j + 1, :] = jnp.mean(k_ref[j * blk:(j + 1) * blk, :].astype(F32),
                                         axis=0, keepdims=True)
    km_hi, km_mid, km_lo = _split3(kmean_ref[...])

    far = jnp.full((1, LANES), rel_ref[REL_BUCKETS - 1, h] * LOG2E, F32)
    far_hi, far_mid, far_lo = _split3(far)
    far_parts = jnp.where(lane_grp == 1, far_hi.astype(F32),
                          jnp.where(lane_grp == 2, far_mid.astype(F32),
                                    jnp.where(lane_grp == 3, far_lo.astype(F32), 0.0)))

    def far_bias(row0, nrows):
        rows = row0 + lax.broadcasted_iota(jnp.int32, (nrows, LANES), 0)
        lanes = lax.broadcasted_iota(jnp.int32, (nrows, LANES), 1)
        own = rows >> (blk.bit_length() - 1)
        return jnp.where((lanes & 7) <= own - 2, far_parts, 0.0), lanes, own

    qaug_ref[:, :HEAD_DIM] = q_ref[...]
    late = min(seq_len, (MOBA_TOPK + 1) * blk)
    qaug_ref[:late, HEAD_DIM:] = far_bias(0, late)[0].astype(BF16)
    if seq_len > late:
        ql = q_ref[late:, :]
        gate = _dot_nt(ql, km_hi) + _dot_nt(ql, km_mid) + _dot_nt(ql, km_lo)
        aug, lanes_l, own_l = far_bias(late, seq_len - late)
        valid = lanes_l < own_l
        g = jnp.where(valid, gate, NEG)
        sel = jnp.zeros(g.shape, jnp.bool_)
        for _ in range(MOBA_TOPK):
            best = jnp.max(g, axis=-1, keepdims=True)
            first = jnp.min(jnp.where(g == best, lanes_l, LANES), axis=-1, keepdims=True)
            pick = lanes_l == first
            sel = sel | pick
            g = jnp.where(pick, -jnp.inf, g)
        qaug_ref[late:, HEAD_DIM:] = jnp.where(valid & jnp.logical_not(sel), NEG, aug).astype(BF16)

    def qk(i):
        return _dot_nt(qaug_ref[i * blk:(i + 1) * blk, :], kaug_ref[0:(i + 1) * blk, :])

    s_next = qk(0)
    for i in range(nb):
        n = (i + 1) * blk
        s = s_next
        if i + 1 < nb:
            s_next = qk(i + 1)
        pieces = [s[:, i * blk:] + town_ref[...]]
        if i >= 1:
            pieces.insert(0, s[:, (i - 1) * blk:i * blk] + tprev_ref[...])
        if i >= 2:
            pieces.insert(0, s[:, :(i - 1) * blk])
        logits = jnp.concatenate(pieces, axis=1) if len(pieces) > 1 else pieces[0]
        o_ref[i * blk:(i + 1) * blk, :] = _softmax_pv(logits, v_ref[0:n, :]).astype(BF16)


def _moba_attn(p, rel_bias, *, batch, seq_len, col0):
    m = p.shape[0]
    hd = HEAD_DIM
    blk = MOBA_BLOCK
    assert seq_len % blk == 0 and seq_len // blk <= 8
    return pl.pallas_call(
        functools.partial(_moba_kernel, seq_len=seq_len),
        grid=(MOBA_HEADS, batch),
        in_specs=[
            pl.BlockSpec(memory_space=pltpu.SMEM),
            pl.BlockSpec((seq_len, hd), lambda h, b: (b, col0 + h)),
            pl.BlockSpec((seq_len, hd), lambda h, b: (b, col0 + MOBA_HEADS + h)),
            pl.BlockSpec((seq_len, hd), lambda h, b: (b, col0 + 2 * MOBA_HEADS + h)),
        ],
        out_specs=pl.BlockSpec((seq_len, hd), lambda h, b: (b, h)),
        out_shape=jax.ShapeDtypeStruct((m, MOBA_HEADS * hd), BF16),
        scratch_shapes=[
            pltpu.VMEM((blk, blk), F32),
            pltpu.VMEM((blk, blk), F32),
            pltpu.VMEM((seq_len, hd + LANES), BF16),
            pltpu.VMEM((seq_len, hd + LANES), BF16),
            pltpu.VMEM((LANES, hd), F32),
        ],
        compiler_params=pltpu.CompilerParams(
            dimension_semantics=("arbitrary", "arbitrary"),
            vmem_limit_bytes=VMEM_LIMIT_BYTES),
        name="moba_attn",
    )(rel_bias, p, p, p)


def _mem_kernel(q_ref, k_ref, v_ref, o_ref, *, seq_len, tq):
    k = k_ref[...]
    v = v_ref[...]
    for i in range(seq_len // tq):
        rows = slice(i * tq, (i + 1) * tq)
        o_ref[rows, :] = _softmax_pv(_dot_nt(q_ref[rows, :], k), v).astype(BF16)


def _mem_attn(p, mkv, *, batch, seq_len, n_mem, col0):
    m = p.shape[0]
    hd = HEAD_DIM
    return pl.pallas_call(
        functools.partial(_mem_kernel, seq_len=seq_len, tq=min(seq_len, 512)),
        grid=(batch, MEM_HEADS),
        in_specs=[
            pl.BlockSpec((seq_len, hd), lambda b, h: (b, col0 + h)),
            pl.BlockSpec((n_mem, hd), lambda b, h: (b, h)),
            pl.BlockSpec((n_mem, hd), lambda b, h: (b, MEM_HEADS + h)),
        ],
        out_specs=pl.BlockSpec((seq_len, hd), lambda b, h: (b, h)),
        out_shape=jax.ShapeDtypeStruct((m, MEM_HEADS * hd), BF16),
        compiler_params=pltpu.CompilerParams(
            dimension_semantics=("arbitrary", "arbitrary"),
            vmem_limit_bytes=VMEM_LIMIT_BYTES),
        name="mem_attn",
    )(p, mkv, mkv)


def _out_proj_kernel(x_ref, of_ref, ob_ref, om_ref, w_ref, o_ref):
    wf = of_ref.shape[1]
    wb = ob_ref.shape[1]
    acc = _dot(of_ref[...], w_ref[0:wf, :])
    acc += _dot(ob_ref[...], w_ref[wf:wf + wb, :])
    acc += _dot(om_ref[...], w_ref[wf + wb:, :])
    o_ref[...] = x_ref[...] + acc


def _out_proj(x2d, o_fox, o_moba, o_mem, w_out, *, tm):
    m, d = x2d.shape
    return pl.pallas_call(
        _out_proj_kernel,
        grid=(m // tm,),
        in_specs=[
            pl.BlockSpec((tm, d), lambda i: (i, 0)),
            pl.BlockSpec((tm, o_fox.shape[1]), lambda i: (i, 0)),
            pl.BlockSpec((tm, o_moba.shape[1]), lambda i: (i, 0)),
            pl.BlockSpec((tm, o_mem.shape[1]), lambda i: (i, 0)),
            pl.BlockSpec(w_out.shape, lambda i: (0, 0)),
        ],
        out_specs=pl.BlockSpec((tm, d), lambda i: (i, 0)),
        out_shape=jax.ShapeDtypeStruct((m, d), F32),
        compiler_params=pltpu.CompilerParams(
            dimension_semantics=("arbitrary",), vmem_limit_bytes=VMEM_LIMIT_BYTES),
        name="out_proj",
    )(x2d, o_fox, o_moba, o_mem, w_out)


def _tile(total, want):
    t = min(total, want)
    assert total % t == 0, (total, want)
    return t


def kernel(x, mem, ffn1_norm, ffn1_w1, ffn1_w3, ffn1_w2, mix_norm, mem_norm, w_in, b_forget,
           w_mem_kv, fox_q_gain, fox_k_gain, moba_q_gain, moba_k_gain, mem_q_gain, mem_k_gain,
           w_out, ffn2_norm, ffn2_w1, ffn2_w3, ffn2_w2, rel_bias):
    batch, seq_len, d = x.shape
    n_mem = mem.shape[1]
    depth = w_in.shape[0]
    fox_w = FOX_HEADS * HEAD_DIM
    moba_w = MOBA_HEADS * HEAD_DIM
    mem_w = MEM_HEADS * HEAD_DIM
    m = batch * seq_len
    tm = _tile(seq_len, 1024)
    tm_out = _tile(seq_len, 512)
    ones = jnp.ones((HEAD_DIM,), F32)
    q_scale = HEAD_DIM ** -0.5 * LOG2E

    x2d = x.reshape(m, d)
    mem2d = mem.reshape(batch * n_mem, d)
    for l in range(depth):
        tf = _tile(ffn1_w1.shape[2], 512)
        n_i, n_f = m // tm, ffn1_w1.shape[2] // tf
        d_ff2 = ffn2_w1.shape[2]
        rows, ff_cols = d // n_i, d_ff2 // n_f
        n_steps = n_i * n_f
        jobs = (
            _cast_job(ffn2_w1[l], (rows, ff_cols), lambda i, f: (i, f)),
            _cast_job(ffn2_w3[l], (rows, ff_cols), lambda i, f: (i, f)),
            _cast_job(ffn2_w2[l], (ff_cols, rows), lambda i, f: (f, i)),
            _cast_job(w_out[l], (_row_slabs(w_out.shape[1], n_f, n_steps)[0], d),
                      _row_slabs(w_out.shape[1], n_f, n_steps)[1]),
            _cast_job(w_mem_kv[l], (_row_slabs(d, n_f, n_steps)[0], w_mem_kv.shape[2]),
                      _row_slabs(d, n_f, n_steps)[1]),
        ) + _w_in_jobs(jnp.swapaxes(w_in[l], 0, 1), n_f, n_steps)
        x2d, (w1_2, w3_2, w2_2, w_out_bf, w_mem_bf, w_main, w_forget) = _ffn(
            x2d, ffn1_norm[l], ffn1_w1[l].astype(BF16), ffn1_w3[l].astype(BF16),
            ffn1_w2[l].astype(BF16), tm=tm, tf=tf, jobs=jobs)

        b_pad = jnp.pad(b_forget[l].astype(F32), (0, LANES - FOX_HEADS)).reshape(1, LANES)
        head_gain = jnp.concatenate(
            [jnp.tile(fox_q_gain[l] * q_scale, FOX_HEADS), jnp.tile(fox_k_gain[l], FOX_HEADS),
             jnp.tile(ones, FOX_HEADS), jnp.tile(moba_q_gain[l] * q_scale, MOBA_HEADS),
             jnp.tile(moba_k_gain[l], MOBA_HEADS), jnp.tile(ones, MOBA_HEADS),
             jnp.tile(mem_q_gain[l] * q_scale, MEM_HEADS)]).astype(F32)
        head_flag = jnp.concatenate(
            [jnp.ones((2 * fox_w,), F32), jnp.zeros((fox_w,), F32), jnp.ones((2 * moba_w,), F32),
             jnp.zeros((moba_w,), F32), jnp.ones((mem_w,), F32)])
        proj, cum = _norm_proj(x2d, mix_norm[l], w_main, head_gain, head_flag, tm=tm, tn=2560,
                               forget=(w_forget, b_pad), seq_len=seq_len)

        kv_gain = jnp.concatenate([jnp.tile(mem_k_gain[l], MEM_HEADS), jnp.tile(ones, MEM_HEADS)])
        kv_flag = jnp.concatenate([jnp.ones((mem_w,), F32), jnp.zeros((mem_w,), F32)])
        mkv = _norm_proj(mem2d, mem_norm[l], w_mem_bf, kv_gain.astype(F32), kv_flag,
                         tm=_tile(batch * n_mem, 512), tn=512)

        o_fox = _fox_attn(proj, cum, batch=batch, seq_len=seq_len)
        o_moba = _moba_attn(proj, rel_bias.astype(F32), batch=batch, seq_len=seq_len,
                            col0=3 * FOX_HEADS)
        o_mem = _mem_attn(proj, mkv, batch=batch, seq_len=seq_len, n_mem=n_mem,
                          col0=3 * FOX_HEADS + 3 * MOBA_HEADS)
        x2d = _out_proj(x2d, o_fox, o_moba, o_mem, w_out_bf, tm=tm_out)

        x2d, _ = _ffn(x2d, ffn2_norm[l], w1_2, w3_2, w2_2, tm=tm, tf=_tile(d_ff2, 512))
    return x2d.reshape(batch, seq_len, d)
```

```python
import functools
import math

import jax
import jax.numpy as jnp
from jax import lax
from jax.experimental import pallas as pl
from jax.experimental.pallas import tpu as pltpu

HEAD_DIM = 128
FOX_HEADS = 8
MOBA_HEADS = 4
MEM_HEADS = 4
MOBA_BLOCK = 256
MOBA_TOPK = 3
REL_BUCKETS = 32
REL_MAX_DIST = 128
EPS = 1e-6
NEG = -1e30
LOG2E = math.log2(math.e)

LANES = 128
BF16_SUBLANES = 16
MXU_WIDTH = 256
ATTN_Q_BLOCK = 256
VMEM_LIMIT_BYTES = 60 * 1024 * 1024

F32 = jnp.float32
BF16 = jnp.bfloat16


def _rms_scale(x):
    return lax.rsqrt(jnp.mean(x * x, axis=-1, keepdims=True) + EPS)


def _dot(a, b):
    return jnp.dot(a, b, preferred_element_type=F32)


def _dot_nt(a, b):
    return lax.dot_general(a, b, (((1,), (1,)), ((), ())), preferred_element_type=F32)


def _split3(x):
    hi = x.astype(BF16)
    r1 = x - hi.astype(F32)
    mid = r1.astype(BF16)
    lo = (r1 - mid.astype(F32)).astype(BF16)
    return hi, mid, lo


def _softmax_pv(logits2, v):
    mx = jnp.max(logits2, axis=-1, keepdims=True)
    p = jnp.exp2(logits2 - mx)
    denom = jnp.sum(p, axis=-1, keepdims=True)
    return _dot(p.astype(BF16), v) / denom


class _SideJob:
    def __init__(self, inputs, outputs, body):
        self.inputs, self.outputs, self.body = inputs, outputs, body


def _ffn_kernel(*refs, jobs):
    x_ref, g_ref, w1_ref, w3_ref, w2_ref = refs[:5]
    n_side_in = sum(len(j.inputs) for j in jobs)
    side_in = refs[5:5 + n_side_in]
    o_ref = refs[5 + n_side_in]
    side_out = refs[6 + n_side_in:-1]
    xn_ref = refs[-1]

    @pl.when(pl.program_id(1) == 0)
    def _():
        x = x_ref[...]
        xn_ref[...] = (x * _rms_scale(x) * g_ref[...]).astype(BF16)
        o_ref[...] = x

    xn = xn_ref[...]
    h1 = _dot(xn, w1_ref[...])
    h3 = _dot(xn, w3_ref[...])
    act = (0.5 * h1 * jax.nn.sigmoid(h1) * h3).astype(BF16)
    o_ref[...] += _dot(act, w2_ref[...])

    for job in jobs:
        ins, side_in = side_in[:len(job.inputs)], side_in[len(job.inputs):]
        outs, side_out = side_out[:len(job.outputs)], side_out[len(job.outputs):]
        job.body(ins, outs)


def _ffn(x2d, gain, w1, w3, w2, *, tm, tf, jobs=()):
    m, d = x2d.shape
    d_ff = w1.shape[1]
    side_in = [io for j in jobs for io in j.inputs]
    side_out = [io for j in jobs for io in j.outputs]
    res = pl.pallas_call(
        functools.partial(_ffn_kernel, jobs=jobs),
        grid=(m // tm, d_ff // tf),
        in_specs=[
            pl.BlockSpec((tm, d), lambda i, f: (i, 0)),
            pl.BlockSpec((1, d), lambda i, f: (0, 0)),
            pl.BlockSpec((d, tf), lambda i, f: (0, f)),
            pl.BlockSpec((d, tf), lambda i, f: (0, f)),
            pl.BlockSpec((tf, d), lambda i, f: (f, 0)),
        ] + [spec for _, spec in side_in],
        out_specs=[pl.BlockSpec((tm, d), lambda i, f: (i, 0))] + [spec for _, spec in side_out],
        out_shape=[jax.ShapeDtypeStruct((m, d), F32)] + [struct for struct, _ in side_out],
        scratch_shapes=[pltpu.VMEM((tm, d), BF16)],
        compiler_params=pltpu.CompilerParams(
            dimension_semantics=("arbitrary", "arbitrary"),
            vmem_limit_bytes=VMEM_LIMIT_BYTES),
        name="ffn",
    )(x2d, gain.reshape(1, d), w1, w3, w2, *[arr for arr, _ in side_in])
    return res[0], res[1:]


def _cast_job(w, block, index_map):
    def body(ins, outs):
        outs[0][...] = ins[0][...].astype(BF16)
    spec = pl.BlockSpec(block, index_map)
    return _SideJob([(w, spec)], [(jax.ShapeDtypeStruct(w.shape, BF16), spec)], body)


def _row_slabs(nrows, n_f, n_steps):
    rows = BF16_SUBLANES
    while nrows % rows or nrows // rows > n_steps:
        rows += BF16_SUBLANES
    last = nrows // rows - 1
    return rows, lambda i, f: (jnp.minimum(i * n_f + f, last), 0)


def _w_in_jobs(w_in_t, n_f, n_steps):
    in_w, d = w_in_t.shape
    f0 = 3 * FOX_HEADS * HEAD_DIM
    main_w = in_w - FOX_HEADS
    rows = BF16_SUBLANES
    while main_w % rows or f0 % rows or main_w // rows > n_steps:
        rows += BF16_SUBLANES
    last = main_w // rows - 1
    step = lambda i, f: jnp.minimum(i * n_f + f, last)
    per_row_block = rows // FOX_HEADS

    def main_body(ins, outs):
        t = jnp.minimum(pl.program_id(0) * n_f + pl.program_id(1), last)
        a = ins[0][...]
        shifted = jnp.concatenate([a[FOX_HEADS:], ins[1][...]], axis=0)
        outs[0][...] = jnp.where(t * rows >= f0, shifted, a).astype(BF16)

    def forget_body(ins, outs):
        pad = jnp.zeros((LANES - FOX_HEADS, d), F32)
        outs[0][...] = jnp.concatenate([ins[0][...], pad], axis=0).astype(BF16)

    main = _SideJob(
        [(w_in_t, pl.BlockSpec((rows, d), lambda i, f: (step(i, f), 0))),
         (w_in_t, pl.BlockSpec((FOX_HEADS, d), lambda i, f: ((step(i, f) + 1) * per_row_block, 0)))],
        [(jax.ShapeDtypeStruct((main_w, d), BF16),
          pl.BlockSpec((rows, d), lambda i, f: (step(i, f), 0)))],
        main_body)
    forget = _SideJob(
        [(w_in_t, pl.BlockSpec((FOX_HEADS, d), lambda i, f: (f0 // FOX_HEADS, 0)))],
        [(jax.ShapeDtypeStruct((LANES, d), BF16), pl.BlockSpec((LANES, d), lambda i, f: (0, 0)))],
        forget_body)
    return main, forget


def _project_heads(xn_ref, w_ref, hg_ref, hflag_ref, p_ref, w_transposed):
    xn = xn_ref[...]
    for sb in range(p_ref.shape[1] // MXU_WIDTH):
        if w_transposed:
            y = _dot_nt(xn, w_ref[sb * MXU_WIDTH:(sb + 1) * MXU_WIDTH, :])
        else:
            y = _dot(xn, w_ref[:, sb * MXU_WIDTH:(sb + 1) * MXU_WIDTH])
        for hh in range(MXU_WIDTH // HEAD_DIM):
            cols = slice(sb * MXU_WIDTH + hh * HEAD_DIM, sb * MXU_WIDTH + (hh + 1) * HEAD_DIM)
            yh = y[:, hh * HEAD_DIM:(hh + 1) * HEAD_DIM]
            normed = yh * _rms_scale(yh) * hg_ref[:, cols]
            p_ref[:, cols] = jnp.where(hflag_ref[:, cols] > 0.0, normed, yh).astype(BF16)


def _proj_kernel(x_ref, g_ref, w_ref, hg_ref, hflag_ref, p_ref, xn_ref):
    @pl.when(pl.program_id(1) == 0)
    def _():
        x = x_ref[...]
        xn_ref[...] = (x * _rms_scale(x) * g_ref[...]).astype(BF16)

    _project_heads(xn_ref, w_ref, hg_ref, hflag_ref, p_ref, w_transposed=False)


def _proj_forget_kernel(x_ref, g_ref, w_ref, hg_ref, hflag_ref, wf_ref, bf_ref,
                        p_ref, c_ref, xn_ref, carry_ref, *, tiles_per_seq, cum_block):
    i = pl.program_id(0)
    j = pl.program_id(1)

    @pl.when(j == 0)
    def _():
        x = x_ref[...]
        xn_ref[...] = (x * _rms_scale(x) * g_ref[...]).astype(BF16)

    def forget_gates():
        @pl.when(i % tiles_per_seq == 0)
        def _():
            carry_ref[...] = jnp.zeros_like(carry_ref)

        z = _dot_nt(xn_ref[...], wf_ref[...]) + bf_ref[...]
        logf = jnp.minimum(z, 0.0) - jnp.log1p(jnp.exp(-jnp.abs(z)))
        r = lax.broadcasted_iota(jnp.int32, (cum_block, cum_block), 0)
        c = lax.broadcasted_iota(jnp.int32, (cum_block, cum_block), 1)
        tril = (c <= r).astype(BF16)
        local = []
        for blk in range(logf.shape[0] // cum_block):
            hi, mid, lo = _split3(logf[blk * cum_block:(blk + 1) * cum_block, :])
            local.append(_dot(tril, hi) + _dot(tril, mid) + _dot(tril, lo))
        carry = carry_ref[...]
        for blk, loc in enumerate(local):
            cum = loc + carry
            c_ref[blk * cum_block:(blk + 1) * cum_block, :] = cum
            carry = cum[cum_block - 1:cum_block, :]
        carry_ref[...] = carry

    heads = functools.partial(_project_heads, xn_ref, w_ref, hg_ref, hflag_ref, p_ref,
                              w_transposed=True)

    @pl.when(j == 1)
    def _():
        forget_gates()
        heads()

    @pl.when(j != 1)
    def _():
        heads()


def _norm_proj(x2d, gain, w, head_gain, head_flag, *, tm, tn, forget=None, seq_len=None):
    m, d = x2d.shape
    n = w.shape[1] if forget is None else w.shape[0]
    w_spec = (pl.BlockSpec((d, tn), lambda i, j: (0, j)) if forget is None
              else pl.BlockSpec((tn, d), lambda i, j: (j, 0)))
    in_specs = [
        pl.BlockSpec((tm, d), lambda i, j: (i, 0)),
        pl.BlockSpec((1, d), lambda i, j: (0, 0)),
        w_spec,
        pl.BlockSpec((1, tn), lambda i, j: (0, j)),
        pl.BlockSpec((1, tn), lambda i, j: (0, j)),
    ]
    args = [x2d, gain.reshape(1, d), w, head_gain.reshape(1, n), head_flag.reshape(1, n)]
    p_spec = pl.BlockSpec((tm, tn), lambda i, j: (i, j))
    p_shape = jax.ShapeDtypeStruct((m, n), BF16)
    params = pltpu.CompilerParams(dimension_semantics=("arbitrary", "arbitrary"),
                                  vmem_limit_bytes=VMEM_LIMIT_BYTES)
    if forget is None:
        return pl.pallas_call(
            _proj_kernel, grid=(m // tm, n // tn), in_specs=in_specs, out_specs=p_spec,
            out_shape=p_shape, scratch_shapes=[pltpu.VMEM((tm, d), BF16)],
            compiler_params=params, name="mem_proj")(*args)
    wf, bf = forget
    assert n // tn >= 2
    in_specs += [pl.BlockSpec((LANES, d), lambda i, j: (0, 0)),
                 pl.BlockSpec((1, LANES), lambda i, j: (0, 0))]
    kern = functools.partial(_proj_forget_kernel, tiles_per_seq=seq_len // tm,
                             cum_block=min(tm, 256))
    return pl.pallas_call(
        kern, grid=(m // tm, n // tn), in_specs=in_specs,
        out_specs=[p_spec, pl.BlockSpec((tm, LANES), lambda i, j: (i, 0))],
        out_shape=[p_shape, jax.ShapeDtypeStruct((m, LANES), F32)],
        scratch_shapes=[pltpu.VMEM((tm, d), BF16), pltpu.VMEM((1, LANES), F32)],
        compiler_params=params, name="mix_proj")(*args, wf, bf)


def _lane_column(x, lane):
    lanes = lax.broadcasted_iota(jnp.int32, x.shape, 1)
    return jnp.sum(jnp.where(lanes == lane, x, 0.0), axis=-1, keepdims=True)


def _fox_kernel(q_ref, k_ref, v_ref, c_ref, o_ref, crow_ref, s_ref, *, seq_len):
    h = pl.program_id(1)
    tq = ATTN_Q_BLOCK
    nq = seq_len // tq
    r = lax.broadcasted_iota(jnp.int32, (tq, tq), 0)
    c = lax.broadcasted_iota(jnp.int32, (tq, tq), 1)
    diag = r == c
    causal = c <= r

    for j in range(nq):
        cj = _lane_column(c_ref[j * tq:(j + 1) * tq, :], h) * LOG2E
        crow_ref[:, j * tq:(j + 1) * tq] = jnp.sum(jnp.where(diag, cj, 0.0), axis=0, keepdims=True)

    def qk(i):
        n = (i + 1) * tq
        s_ref[i % 2, :, 0:n] = _dot_nt(q_ref[i * tq:(i + 1) * tq, :], k_ref[0:n, :])

    qk(nq - 1)
    for i in reversed(range(nq)):
        n = (i + 1) * tq
        if i > 0:
            qk(i - 1)
        c_col = _lane_column(c_ref[i * tq:(i + 1) * tq, :], h) * LOG2E
        logits = s_ref[i % 2, :, 0:n] + (c_col - crow_ref[:, 0:n])
        own = jnp.where(causal, logits[:, i * tq:], NEG)
        if i > 0:
            logits = jnp.concatenate([logits[:, :i * tq], own], axis=1)
        else:
            logits = own
        o_ref[i * tq:(i + 1) * tq, :] = _softmax_pv(logits, v_ref[0:n, :]).astype(BF16)


def _fox_attn(p, cum, *, batch, seq_len):
    m = p.shape[0]
    hd = HEAD_DIM
    return pl.pallas_call(
        functools.partial(_fox_kernel, seq_len=seq_len),
        grid=(batch, FOX_HEADS),
        in_specs=[
            pl.BlockSpec((seq_len, hd), lambda b, h: (b, h)),
            pl.BlockSpec((seq_len, hd), lambda b, h: (b, FOX_HEADS + h)),
            pl.BlockSpec((seq_len, hd), lambda b, h: (b, 2 * FOX_HEADS + h)),
            pl.BlockSpec((seq_len, LANES), lambda b, h: (b, 0)),
        ],
        out_specs=pl.BlockSpec((seq_len, hd), lambda b, h: (b, h)),
        out_shape=jax.ShapeDtypeStruct((m, FOX_HEADS * hd), BF16),
        scratch_shapes=[pltpu.VMEM((1, seq_len), F32),
                        pltpu.VMEM((2, ATTN_Q_BLOCK, seq_len), F32)],
        compiler_params=pltpu.CompilerParams(
            dimension_semantics=("arbitrary", "arbitrary"),
            vmem_limit_bytes=VMEM_LIMIT_BYTES),
        name="fox_attn",
    )(p, p, p, cum)


def _t5_bias(dist, rel_ref, h):
    n = jnp.maximum(dist, 0)
    max_exact = REL_BUCKETS // 2
    nf = jnp.maximum(n, 1).astype(F32)
    large = max_exact + (jnp.log(nf / max_exact) / math.log(REL_MAX_DIST / max_exact)
                         * (REL_BUCKETS - max_exact)).astype(jnp.int32)
    large = jnp.minimum(large, REL_BUCKETS - 1)
    bucket = jnp.where(n < max_exact, n, large)
    bias = jnp.zeros(dist.shape, F32)
    for b in range(REL_BUCKETS):
        bias = jnp.where(bucket == b, rel_ref[b, h], bias)
    return bias


def _moba_kernel(rel_ref, q_ref, k_ref, v_ref, o_ref, town_ref, tprev_ref, kaug_ref, qaug_ref,
                 kmean_ref, s_ref, *, seq_len):
    h = pl.program_id(0)
    blk = MOBA_BLOCK
    nb = seq_len // blk
    r = lax.broadcasted_iota(jnp.int32, (blk, blk), 0)
    c = lax.broadcasted_iota(jnp.int32, (blk, blk), 1)
    lane_grp = lax.broadcasted_iota(jnp.int32, (1, LANES), 1) >> 3

    @pl.when(pl.program_id(1) == 0)
    def _():
        town_ref[...] = jnp.where(c <= r, _t5_bias(r - c, rel_ref, h) * LOG2E, NEG)
        tprev_ref[...] = _t5_bias(r - c + blk, rel_ref, h) * LOG2E
        s = lax.broadcasted_iota(jnp.int32, (seq_len, LANES), 0)
        ln = lax.broadcasted_iota(jnp.int32, (seq_len, LANES), 1)
        first_key = (ln & 7) * blk
        onehot = (ln < 32) & (s >= first_key) & (s < first_key + blk)
        kaug_ref[:, HEAD_DIM:] = jnp.where(onehot, 1.0, 0.0).astype(BF16)

    kaug_ref[:, :HEAD_DIM] = k_ref[...]

    kmean_ref[...] = jnp.zeros_like(kmean_ref)
    for j in range(nb):
        kmean_ref[j:j + 1, :] = jnp.mean(k_ref[j * blk:(j + 1) * blk, :].astype(F32),
                                         axis=0, keepdims=True)
    km_hi, km_mid, km_lo = _split3(kmean_ref[...])

    far = jnp.full((1, LANES), rel_ref[REL_BUCKETS - 1, h] * LOG2E, F32)
    far_hi, far_mid, far_lo = _split3(far)
    far_parts = jnp.where(lane_grp == 1, far_hi.astype(F32),
                          jnp.where(lane_grp == 2, far_mid.astype(F32),
                                    jnp.where(lane_grp == 3, far_lo.astype(F32), 0.0)))

    def far_bias(row0, nrows):
        rows = row0 + lax.broadcasted_iota(jnp.int32, (nrows, LANES), 0)
        lanes = lax.broadcasted_iota(jnp.int32, (nrows, LANES), 1)
        own = rows >> (blk.bit_length() - 1)
        return jnp.where((lanes & 7) <= own - 2, far_parts, 0.0), lanes, own

    qaug_ref[:, :HEAD_DIM] = q_ref[...]
    late = min(seq_len, (MOBA_TOPK + 1) * blk)
    qaug_ref[:late, HEAD_DIM:] = far_bias(0, late)[0].astype(BF16)
    if seq_len > late:
        ql = q_ref[late:, :]
        gate = _dot_nt(ql, km_hi) + _dot_nt(ql, km_mid) + _dot_nt(ql, km_lo)
        aug, lanes_l, own_l = far_bias(late, seq_len - late)
        valid = lanes_l < own_l
        g = jnp.where(valid, gate, NEG)
        sel = jnp.zeros(g.shape, jnp.bool_)
        for _ in range(MOBA_TOPK):
            best = jnp.max(g, axis=-1, keepdims=True)
            first = jnp.min(jnp.where(g == best, lanes_l, LANES), axis=-1, keepdims=True)
            pick = lanes_l == first
            sel = sel | pick
            g = jnp.where(pick, -jnp.inf, g)
        qaug_ref[late:, HEAD_DIM:] = jnp.where(valid & jnp.logical_not(sel), NEG, aug).astype(BF16)

    def qk(i):
        n = (i + 1) * blk
        s_ref[i % 2, :, 0:n] = _dot_nt(qaug_ref[i * blk:(i + 1) * blk, :], kaug_ref[0:n, :])

    n_early = min(nb, MOBA_TOPK + 1)
    order = list(reversed(range(n_early))) + list(reversed(range(n_early, nb)))
    qk(order[0])
    for pos, i in enumerate(order):
        n = (i + 1) * blk
        if pos + 1 < nb:
            qk(order[pos + 1])
        s = s_ref[i % 2, :, 0:n]
        pieces = [s[:, i * blk:] + town_ref[...]]
        if i >= 1:
            pieces.insert(0, s[:, (i - 1) * blk:i * blk] + tprev_ref[...])
        if i >= 2:
            pieces.insert(0, s[:, :(i - 1) * blk])
        logits = jnp.concatenate(pieces, axis=1) if len(pieces) > 1 else pieces[0]
        o_ref[i * blk:(i + 1) * blk, :] = _softmax_pv(logits, v_ref[0:n, :]).astype(BF16)


def _moba_attn(p, rel_bias, *, batch, seq_len, col0):
    m = p.shape[0]
    hd = HEAD_DIM
    blk = MOBA_BLOCK
    assert seq_len % blk == 0 and seq_len // blk <= 8
    return pl.pallas_call(
        functools.partial(_moba_kernel, seq_len=seq_len),
        grid=(MOBA_HEADS, batch),
        in_specs=[
            pl.BlockSpec(memory_space=pltpu.SMEM),
            pl.BlockSpec((seq_len, hd), lambda h, b: (b, col0 + h)),
            pl.BlockSpec((seq_len, hd), lambda h, b: (b, col0 + MOBA_HEADS + h)),
            pl.BlockSpec((seq_len, hd), lambda h, b: (b, col0 + 2 * MOBA_HEADS + h)),
        ],
        out_specs=pl.BlockSpec((seq_len, hd), lambda h, b: (b, h)),
        out_shape=jax.ShapeDtypeStruct((m, MOBA_HEADS * hd), BF16),
        scratch_shapes=[
            pltpu.VMEM((blk, blk), F32),
            pltpu.VMEM((blk, blk), F32),
            pltpu.VMEM((seq_len, hd + LANES), BF16),
            pltpu.VMEM((seq_len, hd + LANES), BF16),
            pltpu.VMEM((LANES, hd), F32),
            pltpu.VMEM((2, blk, seq_len), F32),
        ],
        compiler_params=pltpu.CompilerParams(
            dimension_semantics=("arbitrary", "arbitrary"),
            vmem_limit_bytes=VMEM_LIMIT_BYTES),
        name="moba_attn",
    )(rel_bias, p, p, p)


def _mem_kernel(q_ref, k_ref, v_ref, o_ref, *, seq_len, tq):
    k = k_ref[...]
    v = v_ref[...]
    for i in range(seq_len // tq):
        rows = slice(i * tq, (i + 1) * tq)
        o_ref[rows, :] = _softmax_pv(_dot_nt(q_ref[rows, :], k), v).astype(BF16)


def _mem_attn(p, mkv, *, batch, seq_len, n_mem, col0):
    m = p.shape[0]
    hd = HEAD_DIM
    return pl.pallas_call(
        functools.partial(_mem_kernel, seq_len=seq_len, tq=min(seq_len, 512)),
        grid=(batch, MEM_HEADS),
        in_specs=[
            pl.BlockSpec((seq_len, hd), lambda b, h: (b, col0 + h)),
            pl.BlockSpec((n_mem, hd), lambda b, h: (b, h)),
            pl.BlockSpec((n_mem, hd), lambda b, h: (b, MEM_HEADS + h)),
        ],
        out_specs=pl.BlockSpec((seq_len, hd), lambda b, h: (b, h)),
        out_shape=jax.ShapeDtypeStruct((m, MEM_HEADS * hd), BF16),
        compiler_params=pltpu.CompilerParams(
            dimension_semantics=("arbitrary", "arbitrary"),
            vmem_limit_bytes=VMEM_LIMIT_BYTES),
        name="mem_attn",
    )(p, mkv, mkv)


def _out_proj_kernel(x_ref, of_ref, ob_ref, om_ref, w_ref, o_ref):
    wf = of_ref.shape[1]
    wb = ob_ref.shape[1]
    acc = _dot(of_ref[...], w_ref[0:wf, :])
    acc += _dot(ob_ref[...], w_ref[wf:wf + wb, :])
    acc += _dot(om_ref[...], w_ref[wf + wb:, :])
    o_ref[...] = x_ref[...] + acc


def _out_proj(x2d, o_fox, o_moba, o_mem, w_out, *, tm):
    m, d = x2d.shape
    return pl.pallas_call(
        _out_proj_kernel,
        grid=(m // tm,),
        in_specs=[
            pl.BlockSpec((tm, d), lambda i: (i, 0)),
            pl.BlockSpec((tm, o_fox.shape[1]), lambda i: (i, 0)),
            pl.BlockSpec((tm, o_moba.shape[1]), lambda i: (i, 0)),
            pl.BlockSpec((tm, o_mem.shape[1]), lambda i: (i, 0)),
            pl.BlockSpec(w_out.shape, lambda i: (0, 0)),
        ],
        out_specs=pl.BlockSpec((tm, d), lambda i: (i, 0)),
        out_shape=jax.ShapeDtypeStruct((m, d), F32),
        compiler_params=pltpu.CompilerParams(
            dimension_semantics=("arbitrary",), vmem_limit_bytes=VMEM_LIMIT_BYTES),
        name="out_proj",
    )(x2d, o_fox, o_moba, o_mem, w_out)


def _tile(total, want):
    t = min(total, want)
    assert total % t == 0, (total, want)
    return t


def kernel(x, mem, ffn1_norm, ffn1_w1, ffn1_w3, ffn1_w2, mix_norm, mem_norm, w_in, b_forget,
           w_mem_kv, fox_q_gain, fox_k_gain, moba_q_gain, moba_k_gain, mem_q_gain, mem_k_gain,
           w_out, ffn2_norm, ffn2_w1, ffn2_w3, ffn2_w2, rel_bias):
    batch, seq_len, d = x.shape
    n_mem = mem.shape[1]
    depth = w_in.shape[0]
    fox_w = FOX_HEADS * HEAD_DIM
    moba_w = MOBA_HEADS * HEAD_DIM
    mem_w = MEM_HEADS * HEAD_DIM
    m = batch * seq_len
    tm = _tile(seq_len, 1024)
    tm_out = _tile(seq_len, 512)
    ones = jnp.ones((HEAD_DIM,), F32)
    q_scale = HEAD_DIM ** -0.5 * LOG2E

    x2d = x.reshape(m, d)
    mem2d = mem.reshape(batch * n_mem, d)
    for l in range(depth):
        tf = _tile(ffn1_w1.shape[2], 512)
        n_i, n_f = m // tm, ffn1_w1.shape[2] // tf
        d_ff2 = ffn2_w1.shape[2]
        rows, ff_cols = d // n_i, d_ff2 // n_f
        n_steps = n_i * n_f
        jobs = (
            _cast_job(ffn2_w1[l], (rows, ff_cols), lambda i, f: (i, f)),
            _cast_job(ffn2_w3[l], (rows, ff_cols), lambda i, f: (i, f)),
            _cast_job(ffn2_w2[l], (ff_cols, rows), lambda i, f: (f, i)),
            _cast_job(w_out[l], (_row_slabs(w_out.shape[1], n_f, n_steps)[0], d),
                      _row_slabs(w_out.shape[1], n_f, n_steps)[1]),
            _cast_job(w_mem_kv[l], (_row_slabs(d, n_f, n_steps)[0], w_mem_kv.shape[2]),
                      _row_slabs(d, n_f, n_steps)[1]),
        ) + _w_in_jobs(jnp.swapaxes(w_in[l], 0, 1), n_f, n_steps)
        x2d, (w1_2, w3_2, w2_2, w_out_bf, w_mem_bf, w_main, w_forget) = _ffn(
            x2d, ffn1_norm[l], ffn1_w1[l].astype(BF16), ffn1_w3[l].astype(BF16),
            ffn1_w2[l].astype(BF16), tm=tm, tf=tf, jobs=jobs)

        b_pad = jnp.pad(b_forget[l].astype(F32), (0, LANES - FOX_HEADS)).reshape(1, LANES)
        head_gain = jnp.concatenate(
            [jnp.tile(fox_q_gain[l] * q_scale, FOX_HEADS), jnp.tile(fox_k_gain[l], FOX_HEADS),
             jnp.tile(ones, FOX_HEADS), jnp.tile(moba_q_gain[l] * q_scale, MOBA_HEADS),
             jnp.tile(moba_k_gain[l], MOBA_HEADS), jnp.tile(ones, MOBA_HEADS),
             jnp.tile(mem_q_gain[l] * q_scale, MEM_HEADS)]).astype(F32)
        head_flag = jnp.concatenate(
            [jnp.ones((2 * fox_w,), F32), jnp.zeros((fox_w,), F32), jnp.ones((2 * moba_w,), F32),
             jnp.zeros((moba_w,), F32), jnp.ones((mem_w,), F32)])
        proj, cum = _norm_proj(x2d, mix_norm[l], w_main, head_gain, head_flag, tm=tm, tn=2560,
                               forget=(w_forget, b_pad), seq_len=seq_len)

        kv_gain = jnp.concatenate([jnp.tile(mem_k_gain[l], MEM_HEADS), jnp.tile(ones, MEM_HEADS)])
        kv_flag = jnp.concatenate([jnp.ones((mem_w,), F32), jnp.zeros((mem_w,), F32)])
        mkv = _norm_proj(mem2d, mem_norm[l], w_mem_bf, kv_gain.astype(F32), kv_flag,
                         tm=_tile(batch * n_mem, 512), tn=512)

        o_fox = _fox_attn(proj, cum, batch=batch, seq_len=seq_len)
        o_moba = _moba_attn(proj, rel_bias.astype(F32), batch=batch, seq_len=seq_len,
                            col0=3 * FOX_HEADS)
        o_mem = _mem_attn(proj, mkv, batch=batch, seq_len=seq_len, n_mem=n_mem,
                          col0=3 * FOX_HEADS + 3 * MOBA_HEADS)
        x2d = _out_proj(x2d, o_fox, o_moba, o_mem, w_out_bf, tm=tm_out)

        x2d, _ = _ffn(x2d, ffn2_norm[l], w1_2, w3_2, w2_2, tm=tm, tf=_tile(d_ff2, 512))
    return x2d.reshape(batch, seq_len, d)
```

```python
import functools
import math

import jax
import jax.numpy as jnp
from jax import lax
from jax.experimental import pallas as pl
from jax.experimental.pallas import tpu as pltpu

HEAD_DIM = 128
FOX_HEADS = 8
MOBA_HEADS = 4
MEM_HEADS = 4
MOBA_BLOCK = 256
MOBA_TOPK = 3
REL_BUCKETS = 32
REL_MAX_DIST = 128
EPS = 1e-6
NEG = -1e30
LOG2E = math.log2(math.e)

LANES = 128
BF16_SUBLANES = 16
MXU_WIDTH = 256
ATTN_Q_BLOCK = 256
VMEM_LIMIT_BYTES = 60 * 1024 * 1024

F32 = jnp.float32
BF16 = jnp.bfloat16


def _rms_scale(x):
    return lax.rsqrt(jnp.mean(x * x, axis=-1, keepdims=True) + EPS)


def _dot(a, b):
    return jnp.dot(a, b, preferred_element_type=F32)


def _dot_nt(a, b):
    return lax.dot_general(a, b, (((1,), (1,)), ((), ())), preferred_element_type=F32)


def _split3(x):
    hi = x.astype(BF16)
    r1 = x - hi.astype(F32)
    mid = r1.astype(BF16)
    lo = (r1 - mid.astype(F32)).astype(BF16)
    return hi, mid, lo


def _softmax_pv(logits2, v_ones):
    mx = jnp.max(logits2, axis=-1, keepdims=True)
    p = jnp.exp2(logits2 - mx).astype(BF16)
    pv = _dot(p, v_ones)
    return pv[:, :HEAD_DIM] / pv[:, HEAD_DIM:]


class _SideJob:
    def __init__(self, inputs, outputs, body):
        self.inputs, self.outputs, self.body = inputs, outputs, body


def _ffn_kernel(*refs, jobs):
    x_ref, g_ref, w1_ref, w3_ref, w2_ref = refs[:5]
    n_side_in = sum(len(j.inputs) for j in jobs)
    side_in = refs[5:5 + n_side_in]
    o_ref = refs[5 + n_side_in]
    side_out = refs[6 + n_side_in:-1]
    xn_ref = refs[-1]

    @pl.when(pl.program_id(1) == 0)
    def _():
        x = x_ref[...]
        xn_ref[...] = (x * _rms_scale(x) * g_ref[...]).astype(BF16)
        o_ref[...] = x

    xn = xn_ref[...]
    h1 = _dot(xn, w1_ref[...])
    h3 = _dot(xn, w3_ref[...])
    act = (0.5 * h1 * jax.nn.sigmoid(h1) * h3).astype(BF16)
    o_ref[...] += _dot(act, w2_ref[...])

    for job in jobs:
        ins, side_in = side_in[:len(job.inputs)], side_in[len(job.inputs):]
        outs, side_out = side_out[:len(job.outputs)], side_out[len(job.outputs):]
        job.body(ins, outs)


def _ffn(x2d, gain, w1, w3, w2, *, tm, tf, jobs=()):
    m, d = x2d.shape
    d_ff = w1.shape[1]
    side_in = [io for j in jobs for io in j.inputs]
    side_out = [io for j in jobs for io in j.outputs]
    res = pl.pallas_call(
        functools.partial(_ffn_kernel, jobs=jobs),
        grid=(m // tm, d_ff // tf),
        in_specs=[
            pl.BlockSpec((tm, d), lambda i, f: (i, 0)),
            pl.BlockSpec((1, d), lambda i, f: (0, 0)),
            pl.BlockSpec((d, tf), lambda i, f: (0, f)),
            pl.BlockSpec((d, tf), lambda i, f: (0, f)),
            pl.BlockSpec((tf, d), lambda i, f: (f, 0)),
        ] + [spec for _, spec in side_in],
        out_specs=[pl.BlockSpec((tm, d), lambda i, f: (i, 0))] + [spec for _, spec in side_out],
        out_shape=[jax.ShapeDtypeStruct((m, d), F32)] + [struct for struct, _ in side_out],
        scratch_shapes=[pltpu.VMEM((tm, d), BF16)],
        compiler_params=pltpu.CompilerParams(
            dimension_semantics=("arbitrary", "arbitrary"),
            vmem_limit_bytes=VMEM_LIMIT_BYTES),
        name="ffn",
    )(x2d, gain.reshape(1, d), w1, w3, w2, *[arr for arr, _ in side_in])
    return res[0], res[1:]


def _cast_job(w, block, index_map):
    def body(ins, outs):
        outs[0][...] = ins[0][...].astype(BF16)
    spec = pl.BlockSpec(block, index_map)
    return _SideJob([(w, spec)], [(jax.ShapeDtypeStruct(w.shape, BF16), spec)], body)


def _row_slabs(nrows, n_f, n_steps):
    rows = BF16_SUBLANES
    while nrows % rows or nrows // rows > n_steps:
        rows += BF16_SUBLANES
    last = nrows // rows - 1
    return rows, lambda i, f: (jnp.minimum(i * n_f + f, last), 0)


def _w_in_jobs(w_in_t, n_f, n_steps):
    in_w, d = w_in_t.shape
    f0 = 3 * FOX_HEADS * HEAD_DIM
    main_w = in_w - FOX_HEADS
    rows = BF16_SUBLANES
    while main_w % rows or f0 % rows or main_w // rows > n_steps:
        rows += BF16_SUBLANES
    last = main_w // rows - 1
    step = lambda i, f: jnp.minimum(i * n_f + f, last)
    per_row_block = rows // FOX_HEADS

    def main_body(ins, outs):
        t = jnp.minimum(pl.program_id(0) * n_f + pl.program_id(1), last)
        a = ins[0][...]
        shifted = jnp.concatenate([a[FOX_HEADS:], ins[1][...]], axis=0)
        outs[0][...] = jnp.where(t * rows >= f0, shifted, a).astype(BF16)

    def forget_body(ins, outs):
        pad = jnp.zeros((LANES - FOX_HEADS, d), F32)
        outs[0][...] = jnp.concatenate([ins[0][...], pad], axis=0).astype(BF16)

    main = _SideJob(
        [(w_in_t, pl.BlockSpec((rows, d), lambda i, f: (step(i, f), 0))),
         (w_in_t, pl.BlockSpec((FOX_HEADS, d), lambda i, f: ((step(i, f) + 1) * per_row_block, 0)))],
        [(jax.ShapeDtypeStruct((main_w, d), BF16),
          pl.BlockSpec((rows, d), lambda i, f: (step(i, f), 0)))],
        main_body)
    forget = _SideJob(
        [(w_in_t, pl.BlockSpec((FOX_HEADS, d), lambda i, f: (f0 // FOX_HEADS, 0)))],
        [(jax.ShapeDtypeStruct((LANES, d), BF16), pl.BlockSpec((LANES, d), lambda i, f: (0, 0)))],
        forget_body)
    return main, forget


def _project_heads(xn_ref, w_ref, hg_ref, hflag_ref, p_ref, w_transposed):
    xn = xn_ref[...]
    for sb in range(p_ref.shape[1] // MXU_WIDTH):
        if w_transposed:
            y = _dot_nt(xn, w_ref[sb * MXU_WIDTH:(sb + 1) * MXU_WIDTH, :])
        else:
            y = _dot(xn, w_ref[:, sb * MXU_WIDTH:(sb + 1) * MXU_WIDTH])
        for hh in range(MXU_WIDTH // HEAD_DIM):
            cols = slice(sb * MXU_WIDTH + hh * HEAD_DIM, sb * MXU_WIDTH + (hh + 1) * HEAD_DIM)
            yh = y[:, hh * HEAD_DIM:(hh + 1) * HEAD_DIM]
            normed = yh * _rms_scale(yh) * hg_ref[:, cols]
            p_ref[:, cols] = jnp.where(hflag_ref[:, cols] > 0.0, normed, yh).astype(BF16)


def _proj_kernel(x_ref, g_ref, w_ref, hg_ref, hflag_ref, p_ref, xn_ref):
    @pl.when(pl.program_id(1) == 0)
    def _():
        x = x_ref[...]
        xn_ref[...] = (x * _rms_scale(x) * g_ref[...]).astype(BF16)

    _project_heads(xn_ref, w_ref, hg_ref, hflag_ref, p_ref, w_transposed=False)


def _proj_forget_kernel(x_ref, g_ref, w_ref, hg_ref, hflag_ref, wf_ref, bf_ref,
                        p_ref, c_ref, xn_ref, carry_ref, *, tiles_per_seq, cum_block):
    i = pl.program_id(0)
    j = pl.program_id(1)

    @pl.when(j == 0)
    def _():
        x = x_ref[...]
        xn_ref[...] = (x * _rms_scale(x) * g_ref[...]).astype(BF16)

    def forget_gates():
        @pl.when(i % tiles_per_seq == 0)
        def _():
            carry_ref[...] = jnp.zeros_like(carry_ref)

        z = _dot_nt(xn_ref[...], wf_ref[...]) + bf_ref[...]
        logf = jnp.minimum(z, 0.0) - jnp.log1p(jnp.exp(-jnp.abs(z)))
        r = lax.broadcasted_iota(jnp.int32, (cum_block, cum_block), 0)
        c = lax.broadcasted_iota(jnp.int32, (cum_block, cum_block), 1)
        tril = (c <= r).astype(BF16)
        local = []
        for blk in range(logf.shape[0] // cum_block):
            hi, mid, lo = _split3(logf[blk * cum_block:(blk + 1) * cum_block, :])
            local.append(_dot(tril, hi) + _dot(tril, mid) + _dot(tril, lo))
        carry = carry_ref[...]
        for blk, loc in enumerate(local):
            cum = loc + carry
            c_ref[blk * cum_block:(blk + 1) * cum_block, :] = cum
            carry = cum[cum_block - 1:cum_block, :]
        carry_ref[...] = carry

    heads = functools.partial(_project_heads, xn_ref, w_ref, hg_ref, hflag_ref, p_ref,
                              w_transposed=True)

    @pl.when(j == 1)
    def _():
        forget_gates()
        heads()

    @pl.when(j != 1)
    def _():
        heads()


def _norm_proj(x2d, gain, w, head_gain, head_flag, *, tm, tn, forget=None, seq_len=None):
    m, d = x2d.shape
    n = w.shape[1] if forget is None else w.shape[0]
    w_spec = (pl.BlockSpec((d, tn), lambda i, j: (0, j)) if forget is None
              else pl.BlockSpec((tn, d), lambda i, j: (j, 0)))
    in_specs = [
        pl.BlockSpec((tm, d), lambda i, j: (i, 0)),
        pl.BlockSpec((1, d), lambda i, j: (0, 0)),
        w_spec,
        pl.BlockSpec((1, tn), lambda i, j: (0, j)),
        pl.BlockSpec((1, tn), lambda i, j: (0, j)),
    ]
    args = [x2d, gain.reshape(1, d), w, head_gain.reshape(1, n), head_flag.reshape(1, n)]
    p_spec = pl.BlockSpec((tm, tn), lambda i, j: (i, j))
    p_shape = jax.ShapeDtypeStruct((m, n), BF16)
    params = pltpu.CompilerParams(dimension_semantics=("arbitrary", "arbitrary"),
                                  vmem_limit_bytes=VMEM_LIMIT_BYTES)
    if forget is None:
        return pl.pallas_call(
            _proj_kernel, grid=(m // tm, n // tn), in_specs=in_specs, out_specs=p_spec,
            out_shape=p_shape, scratch_shapes=[pltpu.VMEM((tm, d), BF16)],
            compiler_params=params, name="mem_proj")(*args)
    wf, bf = forget
    assert n // tn >= 2
    in_specs += [pl.BlockSpec((LANES, d), lambda i, j: (0, 0)),
                 pl.BlockSpec((1, LANES), lambda i, j: (0, 0))]
    kern = functools.partial(_proj_forget_kernel, tiles_per_seq=seq_len // tm,
                             cum_block=min(tm, 256))
    return pl.pallas_call(
        kern, grid=(m // tm, n // tn), in_specs=in_specs,
        out_specs=[p_spec, pl.BlockSpec((tm, LANES), lambda i, j: (i, 0))],
        out_shape=[p_shape, jax.ShapeDtypeStruct((m, LANES), F32)],
        scratch_shapes=[pltpu.VMEM((tm, d), BF16), pltpu.VMEM((1, LANES), F32)],
        compiler_params=params, name="mix_proj")(*args, wf, bf)


FORGET_LANES = 6


def _lane_set(lanes, offset):
    hit = lanes < 0
    for hh in range(FOX_HEADS):
        lo = FORGET_LANES * hh + offset
        hit = hit | ((lanes >= lo) & (lanes < lo + 3))
    return hit


def _fox_kernel(q_ref, k_ref, v_ref, c_ref, o_ref, qc_ref, kc_ref, kaug_ref, vaug_ref, s_ref,
                *, seq_len):
    h = pl.program_id(1)
    tq = ATTN_Q_BLOCK
    nq = seq_len // tq
    r = lax.broadcasted_iota(jnp.int32, (tq, tq), 0)
    c = lax.broadcasted_iota(jnp.int32, (tq, tq), 1)
    causal = c <= r

    @pl.when(h == 0)
    def _():
        hi, mid, lo = _split3(c_ref[...] * LOG2E)
        row = lax.broadcasted_iota(jnp.int32, (3 * LANES, 2 * LANES), 0)
        col = lax.broadcasted_iota(jnp.int32, (3 * LANES, 2 * LANES), 1)
        piece, head = row >> 7, row & (LANES - 1)
        slot = FORGET_LANES * head + piece
        route = jnp.where((head < FOX_HEADS) & (col == slot), 1.0,
                          jnp.where((head < FOX_HEADS) & (col == LANES + 3 + slot), -1.0, 0.0))
        routed = _dot(jnp.concatenate([hi, mid, lo], axis=1), route.astype(BF16))
        lane1 = lax.broadcasted_iota(jnp.int32, (1, LANES), 1)
        qc_ref[...] = (routed[:, :LANES] + jnp.where(_lane_set(lane1, 3), 1.0, 0.0)).astype(BF16)
        kc_ref[...] = (routed[:, LANES:] + jnp.where(_lane_set(lane1, 0), 1.0, 0.0)).astype(BF16)
        vaug_ref[:, HEAD_DIM:] = jnp.ones((seq_len, HEAD_DIM), BF16)

    lanes = lax.broadcasted_iota(jnp.int32, (seq_len, LANES), 1)
    mine = (lanes >= FORGET_LANES * h) & (lanes < FORGET_LANES * (h + 1))
    kaug_ref[:, :HEAD_DIM] = k_ref[...]
    kaug_ref[:, HEAD_DIM:] = jnp.where(mine, kc_ref[...], jnp.zeros((), BF16))
    vaug_ref[:, :HEAD_DIM] = v_ref[...]

    def qk(i):
        n = (i + 1) * tq
        rows = slice(i * tq, (i + 1) * tq)
        q_aug = jnp.concatenate([q_ref[rows, :], qc_ref[rows, :]], axis=1)
        s_ref[i % 2, :, 0:n] = _dot_nt(q_aug, kaug_ref[0:n, :])

    qk(nq - 1)
    for i in reversed(range(nq)):
        n = (i + 1) * tq
        if i > 0:
            qk(i - 1)
        logits = s_ref[i % 2, :, 0:n]
        own = jnp.where(causal, logits[:, i * tq:], NEG)
        if i > 0:
            logits = jnp.concatenate([logits[:, :i * tq], own], axis=1)
        else:
            logits = own
        o_ref[i * tq:(i + 1) * tq, :] = _softmax_pv(logits, vaug_ref[0:n, :]).astype(BF16)


def _fox_attn(p, cum, *, batch, seq_len):
    m = p.shape[0]
    hd = HEAD_DIM
    return pl.pallas_call(
        functools.partial(_fox_kernel, seq_len=seq_len),
        grid=(batch, FOX_HEADS),
        in_specs=[
            pl.BlockSpec((seq_len, hd), lambda b, h: (b, h)),
            pl.BlockSpec((seq_len, hd), lambda b, h: (b, FOX_HEADS + h)),
            pl.BlockSpec((seq_len, hd), lambda b, h: (b, 2 * FOX_HEADS + h)),
            pl.BlockSpec((seq_len, LANES), lambda b, h: (b, 0)),
        ],
        out_specs=pl.BlockSpec((seq_len, hd), lambda b, h: (b, h)),
        out_shape=jax.ShapeDtypeStruct((m, FOX_HEADS * hd), BF16),
        scratch_shapes=[pltpu.VMEM((seq_len, LANES), BF16),
                        pltpu.VMEM((seq_len, LANES), BF16),
                        pltpu.VMEM((seq_len, hd + LANES), BF16),
                        pltpu.VMEM((seq_len, 2 * hd), BF16),
                        pltpu.VMEM((2, ATTN_Q_BLOCK, seq_len), F32)],
        compiler_params=pltpu.CompilerParams(
            dimension_semantics=("arbitrary", "arbitrary"),
            vmem_limit_bytes=VMEM_LIMIT_BYTES),
        name="fox_attn",
    )(p, p, p, cum)


def _t5_bias(dist, rel_ref, h):
    n = jnp.maximum(dist, 0)
    max_exact = REL_BUCKETS // 2
    nf = jnp.maximum(n, 1).astype(F32)
    large = max_exact + (jnp.log(nf / max_exact) / math.log(REL_MAX_DIST / max_exact)
                         * (REL_BUCKETS - max_exact)).astype(jnp.int32)
    large = jnp.minimum(large, REL_BUCKETS - 1)
    bucket = jnp.where(n < max_exact, n, large)
    bias = jnp.zeros(dist.shape, F32)
    for b in range(REL_BUCKETS):
        bias = jnp.where(bucket == b, rel_ref[b, h], bias)
    return bias


def _moba_kernel(rel_ref, q_ref, k_ref, v_ref, o_ref, town_ref, tprev_ref, kaug_ref, qaug_ref,
                 vaug_ref, kmean_ref, s_ref, *, seq_len):
    h = pl.program_id(0)
    blk = MOBA_BLOCK
    nb = seq_len // blk
    r = lax.broadcasted_iota(jnp.int32, (blk, blk), 0)
    c = lax.broadcasted_iota(jnp.int32, (blk, blk), 1)
    lane_grp = lax.broadcasted_iota(jnp.int32, (1, LANES), 1) >> 3

    @pl.when(pl.program_id(1) == 0)
    def _():
        town_ref[...] = jnp.where(c <= r, _t5_bias(r - c, rel_ref, h) * LOG2E, NEG)
        tprev_ref[...] = _t5_bias(r - c + blk, rel_ref, h) * LOG2E
        s = lax.broadcasted_iota(jnp.int32, (seq_len, LANES), 0)
        ln = lax.broadcasted_iota(jnp.int32, (seq_len, LANES), 1)
        first_key = (ln & 7) * blk
        onehot = (ln < 32) & (s >= first_key) & (s < first_key + blk)
        kaug_ref[:, HEAD_DIM:] = jnp.where(onehot, 1.0, 0.0).astype(BF16)
        vaug_ref[:, HEAD_DIM:] = jnp.ones((seq_len, HEAD_DIM), BF16)

    kaug_ref[:, :HEAD_DIM] = k_ref[...]
    vaug_ref[:, :HEAD_DIM] = v_ref[...]

    kmean_ref[...] = jnp.zeros_like(kmean_ref)
    for j in range(nb):
        kmean_ref[j:j + 1, :] = jnp.mean(k_ref[j * blk:(j + 1) * blk, :].astype(F32),
                                         axis=0, keepdims=True)
    km_hi, km_mid, km_lo = _split3(kmean_ref[...])

    far = jnp.full((1, LANES), rel_ref[REL_BUCKETS - 1, h] * LOG2E, F32)
    far_hi, far_mid, far_lo = _split3(far)
    far_parts = jnp.where(lane_grp == 1, far_hi.astype(F32),
                          jnp.where(lane_grp == 2, far_mid.astype(F32),
                                    jnp.where(lane_grp == 3, far_lo.astype(F32), 0.0)))

    def far_bias(row0, nrows):
        rows = row0 + lax.broadcasted_iota(jnp.int32, (nrows, LANES), 0)
        lanes = lax.broadcasted_iota(jnp.int32, (nrows, LANES), 1)
        own = rows >> (blk.bit_length() - 1)
        return jnp.where((lanes & 7) <= own - 2, far_parts, 0.0), lanes, own

    qaug_ref[:, :HEAD_DIM] = q_ref[...]
    late = min(seq_len, (MOBA_TOPK + 1) * blk)
    qaug_ref[:late, HEAD_DIM:] = far_bias(0, late)[0].astype(BF16)
    if seq_len > late:
        ql = q_ref[late:, :]
        gate = _dot_nt(ql, km_hi) + _dot_nt(ql, km_mid) + _dot_nt(ql, km_lo)
        aug, lanes_l, own_l = far_bias(late, seq_len - late)
        valid = lanes_l < own_l
        g = jnp.where(valid, gate, NEG)
        sel = jnp.zeros(g.shape, jnp.bool_)
        lane_f = lanes_l.astype(F32)
        for _ in range(MOBA_TOPK):
            best = jnp.max(g, axis=-1, keepdims=True)
            first = jnp.min(jnp.where(g == best, lane_f, float(LANES)), axis=-1, keepdims=True)
            pick = lane_f == first
            sel = sel | pick
            g = jnp.where(pick, -jnp.inf, g)
        qaug_ref[late:, HEAD_DIM:] = jnp.where(valid & jnp.logical_not(sel), NEG, aug).astype(BF16)

    def qk(i):
        n = (i + 1) * blk
        s_ref[i % 2, :, 0:n] = _dot_nt(qaug_ref[i * blk:(i + 1) * blk, :], kaug_ref[0:n, :])

    n_early = min(nb, MOBA_TOPK + 1)
    order = list(reversed(range(n_early))) + list(reversed(range(n_early, nb)))
    qk(order[0])
    for pos, i in enumerate(order):
        n = (i + 1) * blk
        if pos + 1 < nb:
            qk(order[pos + 1])
        s = s_ref[i % 2, :, 0:n]
        pieces = [s[:, i * blk:] + town_ref[...]]
        if i >= 1:
            pieces.insert(0, s[:, (i - 1) * blk:i * blk] + tprev_ref[...])
        if i >= 2:
            pieces.insert(0, s[:, :(i - 1) * blk])
        logits = jnp.concatenate(pieces, axis=1) if len(pieces) > 1 else pieces[0]
        o_ref[i * blk:(i + 1) * blk, :] = _softmax_pv(logits, vaug_ref[0:n, :]).astype(BF16)


def _moba_attn(p, rel_bias, *, batch, seq_len, col0):
    m = p.shape[0]
    hd = HEAD_DIM
    blk = MOBA_BLOCK
    assert seq_len % blk == 0 and seq_len // blk <= 8
    return pl.pallas_call(
        functools.partial(_moba_kernel, seq_len=seq_len),
        grid=(MOBA_HEADS, batch),
        in_specs=[
            pl.BlockSpec(memory_space=pltpu.SMEM),
            pl.BlockSpec((seq_len, hd), lambda h, b: (b, col0 + h)),
            pl.BlockSpec((seq_len, hd), lambda h, b: (b, col0 + MOBA_HEADS + h)),
            pl.BlockSpec((seq_len, hd), lambda h, b: (b, col0 + 2 * MOBA_HEADS + h)),
        ],
        out_specs=pl.BlockSpec((seq_len, hd), lambda h, b: (b, h)),
        out_shape=jax.ShapeDtypeStruct((m, MOBA_HEADS * hd), BF16),
        scratch_shapes=[
            pltpu.VMEM((blk, blk), F32),
            pltpu.VMEM((blk, blk), F32),
            pltpu.VMEM((seq_len, hd + LANES), BF16),
            pltpu.VMEM((seq_len, hd + LANES), BF16),
            pltpu.VMEM((seq_len, 2 * hd), BF16),
            pltpu.VMEM((LANES, hd), F32),
            pltpu.VMEM((2, blk, seq_len), F32),
        ],
        compiler_params=pltpu.CompilerParams(
            dimension_semantics=("arbitrary", "arbitrary"),
            vmem_limit_bytes=VMEM_LIMIT_BYTES),
        name="moba_attn",
    )(rel_bias, p, p, p)


def _mem_kernel(q_ref, k_ref, v_ref, o_ref, *, seq_len, tq):
    k = k_ref[...]
    v_ones = jnp.concatenate([v_ref[...], jnp.ones(v_ref.shape, BF16)], axis=1)
    for i in range(seq_len // tq):
        rows = slice(i * tq, (i + 1) * tq)
        o_ref[rows, :] = _softmax_pv(_dot_nt(q_ref[rows, :], k), v_ones).astype(BF16)


def _mem_attn(p, mkv, *, batch, seq_len, n_mem, col0):
    m = p.shape[0]
    hd = HEAD_DIM
    return pl.pallas_call(
        functools.partial(_mem_kernel, seq_len=seq_len, tq=min(seq_len, 512)),
        grid=(batch, MEM_HEADS),
        in_specs=[
            pl.BlockSpec((seq_len, hd), lambda b, h: (b, col0 + h)),
            pl.BlockSpec((n_mem, hd), lambda b, h: (b, h)),
            pl.BlockSpec((n_mem, hd), lambda b, h: (b, MEM_HEADS + h)),
        ],
        out_specs=pl.BlockSpec((seq_len, hd), lambda b, h: (b, h)),
        out_shape=jax.ShapeDtypeStruct((m, MEM_HEADS * hd), BF16),
        compiler_params=pltpu.CompilerParams(
            dimension_semantics=("arbitrary", "arbitrary"),
            vmem_limit_bytes=VMEM_LIMIT_BYTES),
        name="mem_attn",
    )(p, mkv, mkv)


def _out_proj_kernel(x_ref, of_ref, ob_ref, om_ref, w_ref, o_ref):
    wf = of_ref.shape[1]
    wb = ob_ref.shape[1]
    acc = _dot(of_ref[...], w_ref[0:wf, :])
    acc += _dot(ob_ref[...], w_ref[wf:wf + wb, :])
    acc += _dot(om_ref[...], w_ref[wf + wb:, :])
    o_ref[...] = x_ref[...] + acc


def _out_proj(x2d, o_fox, o_moba, o_mem, w_out, *, tm):
    m, d = x2d.shape
    return pl.pallas_call(
        _out_proj_kernel,
        grid=(m // tm,),
        in_specs=[
            pl.BlockSpec((tm, d), lambda i: (i, 0)),
            pl.BlockSpec((tm, o_fox.shape[1]), lambda i: (i, 0)),
            pl.BlockSpec((tm, o_moba.shape[1]), lambda i: (i, 0)),
            pl.BlockSpec((tm, o_mem.shape[1]), lambda i: (i, 0)),
            pl.BlockSpec(w_out.shape, lambda i: (0, 0)),
        ],
        out_specs=pl.BlockSpec((tm, d), lambda i: (i, 0)),
        out_shape=jax.ShapeDtypeStruct((m, d), F32),
        compiler_params=pltpu.CompilerParams(
            dimension_semantics=("arbitrary",), vmem_limit_bytes=VMEM_LIMIT_BYTES),
        name="out_proj",
    )(x2d, o_fox, o_moba, o_mem, w_out)


def _tile(total, want):
    t = min(total, want)
    assert total % t == 0, (total, want)
    return t


def kernel(x, mem, ffn1_norm, ffn1_w1, ffn1_w3, ffn1_w2, mix_norm, mem_norm, w_in, b_forget,
           w_mem_kv, fox_q_gain, fox_k_gain, moba_q_gain, moba_k_gain, mem_q_gain, mem_k_gain,
           w_out, ffn2_norm, ffn2_w1, ffn2_w3, ffn2_w2, rel_bias):
    batch, seq_len, d = x.shape
    n_mem = mem.shape[1]
    depth = w_in.shape[0]
    fox_w = FOX_HEADS * HEAD_DIM
    moba_w = MOBA_HEADS * HEAD_DIM
    mem_w = MEM_HEADS * HEAD_DIM
    m = batch * seq_len
    tm = _tile(seq_len, 1024)
    tm_out = _tile(seq_len, 512)
    ones = jnp.ones((HEAD_DIM,), F32)
    q_scale = HEAD_DIM ** -0.5 * LOG2E

    x2d = x.reshape(m, d)
    mem2d = mem.reshape(batch * n_mem, d)
    for l in range(depth):
        tf = _tile(ffn1_w1.shape[2], 512)
        n_i, n_f = m // tm, ffn1_w1.shape[2] // tf
        d_ff2 = ffn2_w1.shape[2]
        rows, ff_cols = d // n_i, d_ff2 // n_f
        n_steps = n_i * n_f
        jobs = (
            _cast_job(ffn2_w1[l], (rows, ff_cols), lambda i, f: (i, f)),
            _cast_job(ffn2_w3[l], (rows, ff_cols), lambda i, f: (i, f)),
            _cast_job(ffn2_w2[l], (ff_cols, rows), lambda i, f: (f, i)),
            _cast_job(w_out[l], (_row_slabs(w_out.shape[1], n_f, n_steps)[0], d),
                      _row_slabs(w_out.shape[1], n_f, n_steps)[1]),
            _cast_job(w_mem_kv[l], (_row_slabs(d, n_f, n_steps)[0], w_mem_kv.shape[2]),
                      _row_slabs(d, n_f, n_steps)[1]),
        ) + _w_in_jobs(jnp.swapaxes(w_in[l], 0, 1), n_f, n_steps)
        x2d, (w1_2, w3_2, w2_2, w_out_bf, w_mem_bf, w_main, w_forget) = _ffn(
            x2d, ffn1_norm[l], ffn1_w1[l].astype(BF16), ffn1_w3[l].astype(BF16),
            ffn1_w2[l].astype(BF16), tm=tm, tf=tf, jobs=jobs)

        b_pad = jnp.pad(b_forget[l].astype(F32), (0, LANES - FOX_HEADS)).reshape(1, LANES)
        head_gain = jnp.concatenate(
            [jnp.tile(fox_q_gain[l] * q_scale, FOX_HEADS), jnp.tile(fox_k_gain[l], FOX_HEADS),
             jnp.tile(ones, FOX_HEADS), jnp.tile(moba_q_gain[l] * q_scale, MOBA_HEADS),
             jnp.tile(moba_k_gain[l], MOBA_HEADS), jnp.tile(ones, MOBA_HEADS),
             jnp.tile(mem_q_gain[l] * q_scale, MEM_HEADS)]).astype(F32)
        head_flag = jnp.concatenate(
            [jnp.ones((2 * fox_w,), F32), jnp.zeros((fox_w,), F32), jnp.ones((2 * moba_w,), F32),
             jnp.zeros((moba_w,), F32), jnp.ones((mem_w,), F32)])
        proj, cum = _norm_proj(x2d, mix_norm[l], w_main, head_gain, head_flag, tm=tm, tn=2560,
                               forget=(w_forget, b_pad), seq_len=seq_len)

        kv_gain = jnp.concatenate([jnp.tile(mem_k_gain[l], MEM_HEADS), jnp.tile(ones, MEM_HEADS)])
        kv_flag = jnp.concatenate([jnp.ones((mem_w,), F32), jnp.zeros((mem_w,), F32)])
        mkv = _norm_proj(mem2d, mem_norm[l], w_mem_bf, kv_gain.astype(F32), kv_flag,
                         tm=_tile(batch * n_mem, 512), tn=512)

        o_fox = _fox_attn(proj, cum, batch=batch, seq_len=seq_len)
        o_moba = _moba_attn(proj, rel_bias.astype(F32), batch=batch, seq_len=seq_len,
                            col0=3 * FOX_HEADS)
        o_mem = _mem_attn(proj, mkv, batch=batch, seq_len=seq_len, n_mem=n_mem,
                          col0=3 * FOX_HEADS + 3 * MOBA_HEADS)
        x2d = _out_proj(x2d, o_fox, o_moba, o_mem, w_out_bf, tm=tm_out)

        x2d, _ = _ffn(x2d, ffn2_norm[l], w1_2, w3_2, w2_2, tm=tm, tf=_tile(d_ff2, 512))
    return x2d.reshape(batch, seq_len, d)
```

```python
import functools
import math

import jax
import jax.numpy as jnp
from jax import lax
from jax.experimental import pallas as pl
from jax.experimental.pallas import tpu as pltpu

HEAD_DIM = 128
FOX_HEADS = 8
MOBA_HEADS = 4
MEM_HEADS = 4
MOBA_BLOCK = 256
MOBA_TOPK = 3
REL_BUCKETS = 32
REL_MAX_DIST = 128
EPS = 1e-6
NEG = -1e30
LOG2E = math.log2(math.e)

LANES = 128
BF16_SUBLANES = 16
MXU_WIDTH = 256
ATTN_Q_BLOCK = 256
VMEM_LIMIT_BYTES = 60 * 1024 * 1024

F32 = jnp.float32
BF16 = jnp.bfloat16


def _rms_scale(x):
    return lax.rsqrt(jnp.mean(x * x, axis=-1, keepdims=True) + EPS)


def _dot(a, b):
    return jnp.dot(a, b, preferred_element_type=F32)


def _dot_nt(a, b):
    return lax.dot_general(a, b, (((1,), (1,)), ((), ())), preferred_element_type=F32)


def _split3(x):
    hi = x.astype(BF16)
    r1 = x - hi.astype(F32)
    mid = r1.astype(BF16)
    lo = (r1 - mid.astype(F32)).astype(BF16)
    return hi, mid, lo


def _softmax_pv(logits2, v_ones):
    mx = jnp.max(logits2, axis=-1, keepdims=True)
    p = jnp.exp2(logits2 - mx).astype(BF16)
    pv = _dot(p, v_ones)
    return pv[:, :HEAD_DIM] / pv[:, HEAD_DIM:]


class _SideJob:
    def __init__(self, inputs, outputs, body):
        self.inputs, self.outputs, self.body = inputs, outputs, body


def _norm_to(x_ref, g_ref, xn_ref, o_ref):
    x = x_ref[...]
    xn_ref[...] = (x * _rms_scale(x) * g_ref[...]).astype(BF16)
    o_ref[...] = x


def _swiglu_half(xn, w1, w3, w2):
    h1 = _dot(xn, w1)
    h3 = _dot(xn, w3)
    act = (0.5 * h1 * jax.nn.sigmoid(h1) * h3).astype(BF16)
    return _dot(act, w2)


def _ffn_head_kernel(x_ref, g_ref, w1_ref, w3_ref, w2_ref, o_ref, w1b_ref, w3b_ref, w2b_ref, xn_ref):
    pl.when(pl.program_id(0) == 0)(functools.partial(_norm_to, x_ref, g_ref, xn_ref, o_ref))
    w1b_ref[...] = w1_ref[...].astype(BF16)
    w3b_ref[...] = w3_ref[...].astype(BF16)
    w2b_ref[...] = w2_ref[...].astype(BF16)
    o_ref[...] += _swiglu_half(xn_ref[...], w1b_ref[...], w3b_ref[...], w2b_ref[...])


def _ffn_head(x2d, gain, w1, w3, w2, *, tm, tf):
    d = x2d.shape[1]
    d_ff = w1.shape[1]
    up = pl.BlockSpec((d, tf), lambda f: (0, f))
    down = pl.BlockSpec((tf, d), lambda f: (f, 0))
    row = pl.BlockSpec((tm, d), lambda f: (0, 0))
    return pl.pallas_call(
        _ffn_head_kernel,
        grid=(d_ff // tf,),
        in_specs=[row, pl.BlockSpec((1, d), lambda f: (0, 0)), up, up, down],
        out_specs=[row, up, up, down],
        out_shape=[jax.ShapeDtypeStruct((tm, d), F32), jax.ShapeDtypeStruct(w1.shape, BF16),
                   jax.ShapeDtypeStruct(w3.shape, BF16), jax.ShapeDtypeStruct(w2.shape, BF16)],
        scratch_shapes=[pltpu.VMEM((tm, d), BF16)],
        compiler_params=pltpu.CompilerParams(
            dimension_semantics=("arbitrary",), vmem_limit_bytes=VMEM_LIMIT_BYTES),
        name="ffn_head",
    )(x2d, gain.reshape(1, d), w1, w3, w2)


def _ffn_kernel(*refs, jobs, has_head):
    x_ref, g_ref, w1_ref, w3_ref, w2_ref = refs[:5]
    n_in = 5 + has_head + sum(len(j.inputs) for j in jobs)
    side_in = refs[5 + has_head:n_in]
    o_ref = refs[n_in]
    side_out = refs[n_in + 1:-1]
    xn_ref = refs[-1]
    i = pl.program_id(0)
    first = pl.program_id(1) == 0

    def tile():
        pl.when(first)(functools.partial(_norm_to, x_ref, g_ref, xn_ref, o_ref))
        o_ref[...] += _swiglu_half(xn_ref[...], w1_ref[...], w3_ref[...], w2_ref[...])

    if has_head:
        head_ref = refs[5]
        pl.when((i == 0) & first)(lambda: pltpu.sync_copy(head_ref, o_ref))
        pl.when(i > 0)(tile)
    else:
        tile()

    for job in jobs:
        ins, side_in = side_in[:len(job.inputs)], side_in[len(job.inputs):]
        outs, side_out = side_out[:len(job.outputs)], side_out[len(job.outputs):]
        job.body(ins, outs)


def _ffn(x2d, gain, w1, w3, w2, *, tm, tf, jobs=(), head=None):
    m, d = x2d.shape
    d_ff = w1.shape[1]
    has_head = head is not None
    col = (lambda i, f: jnp.where(i == 0, 0, f)) if has_head else (lambda i, f: f)
    side_in = [io for j in jobs for io in j.inputs]
    side_out = [io for j in jobs for io in j.outputs]
    res = pl.pallas_call(
        functools.partial(_ffn_kernel, jobs=jobs, has_head=has_head),
        grid=(m // tm, d_ff // tf),
        in_specs=[
            pl.BlockSpec((tm, d), lambda i, f: (i, 0)),
            pl.BlockSpec((1, d), lambda i, f: (0, 0)),
            pl.BlockSpec((d, tf), lambda i, f: (0, col(i, f))),
            pl.BlockSpec((d, tf), lambda i, f: (0, col(i, f))),
            pl.BlockSpec((tf, d), lambda i, f: (col(i, f), 0)),
        ] + [pl.BlockSpec(memory_space=pl.ANY)] * has_head + [spec for _, spec in side_in],
        out_specs=[pl.BlockSpec((tm, d), lambda i, f: (i, 0))] + [spec for _, spec in side_out],
        out_shape=[jax.ShapeDtypeStruct((m, d), F32)] + [struct for struct, _ in side_out],
        scratch_shapes=[pltpu.VMEM((tm, d), BF16)],
        compiler_params=pltpu.CompilerParams(
            dimension_semantics=("arbitrary", "arbitrary"),
            vmem_limit_bytes=VMEM_LIMIT_BYTES),
        name="ffn",
    )(x2d, gain.reshape(1, d), w1, w3, w2, *([head] if has_head else []),
      *[arr for arr, _ in side_in])
    return res[0], res[1:]


def _cast_job(w, block, index_map):
    def body(ins, outs):
        outs[0][...] = ins[0][...].astype(BF16)
    spec = pl.BlockSpec(block, index_map)
    return _SideJob([(w, spec)], [(jax.ShapeDtypeStruct(w.shape, BF16), spec)], body)


def _row_slabs(nrows, n_f, n_steps):
    rows = BF16_SUBLANES
    while nrows % rows or nrows // rows > n_steps:
        rows += BF16_SUBLANES
    last = nrows // rows - 1
    return rows, lambda i, f: (jnp.minimum(i * n_f + f, last), 0)


def _w_in_jobs(w_in_t, n_f, n_steps):
    in_w, d = w_in_t.shape
    f0 = 3 * FOX_HEADS * HEAD_DIM
    main_w = in_w - FOX_HEADS
    rows = BF16_SUBLANES
    while main_w % rows or f0 % rows or main_w // rows > n_steps:
        rows += BF16_SUBLANES
    last = main_w // rows - 1
    step = lambda i, f: jnp.minimum(i * n_f + f, last)
    per_row_block = rows // FOX_HEADS

    def main_body(ins, outs):
        t = jnp.minimum(pl.program_id(0) * n_f + pl.program_id(1), last)
        a = ins[0][...]
        shifted = jnp.concatenate([a[FOX_HEADS:], ins[1][...]], axis=0)
        outs[0][...] = jnp.where(t * rows >= f0, shifted, a).astype(BF16)

    def forget_body(ins, outs):
        pad = jnp.zeros((LANES - FOX_HEADS, d), F32)
        outs[0][...] = jnp.concatenate([ins[0][...], pad], axis=0).astype(BF16)

    main = _SideJob(
        [(w_in_t, pl.BlockSpec((rows, d), lambda i, f: (step(i, f), 0))),
         (w_in_t, pl.BlockSpec((FOX_HEADS, d), lambda i, f: ((step(i, f) + 1) * per_row_block, 0)))],
        [(jax.ShapeDtypeStruct((main_w, d), BF16),
          pl.BlockSpec((rows, d), lambda i, f: (step(i, f), 0)))],
        main_body)
    forget = _SideJob(
        [(w_in_t, pl.BlockSpec((FOX_HEADS, d), lambda i, f: (f0 // FOX_HEADS, 0)))],
        [(jax.ShapeDtypeStruct((LANES, d), BF16), pl.BlockSpec((LANES, d), lambda i, f: (0, 0)))],
        forget_body)
    return main, forget


def _project_heads(xn_ref, w_ref, hg_ref, hflag_ref, p_ref, w_transposed):
    xn = xn_ref[...]
    for sb in range(p_ref.shape[1] // MXU_WIDTH):
        if w_transposed:
            y = _dot_nt(xn, w_ref[sb * MXU_WIDTH:(sb + 1) * MXU_WIDTH, :])
        else:
            y = _dot(xn, w_ref[:, sb * MXU_WIDTH:(sb + 1) * MXU_WIDTH])
        for hh in range(MXU_WIDTH // HEAD_DIM):
            cols = slice(sb * MXU_WIDTH + hh * HEAD_DIM, sb * MXU_WIDTH + (hh + 1) * HEAD_DIM)
            yh = y[:, hh * HEAD_DIM:(hh + 1) * HEAD_DIM]
            normed = yh * _rms_scale(yh) * hg_ref[:, cols]
            p_ref[:, cols] = jnp.where(hflag_ref[:, cols] > 0.0, normed, yh).astype(BF16)


def _proj_kernel(x_ref, g_ref, w_ref, hg_ref, hflag_ref, p_ref, xn_ref):
    @pl.when(pl.program_id(1) == 0)
    def _():
        x = x_ref[...]
        xn_ref[...] = (x * _rms_scale(x) * g_ref[...]).astype(BF16)

    _project_heads(xn_ref, w_ref, hg_ref, hflag_ref, p_ref, w_transposed=False)


def _proj_forget_kernel(x_ref, g_ref, w_ref, hg_ref, hflag_ref, wf_ref, bf_ref,
                        p_ref, c_ref, xn_ref, carry_ref, *, tiles_per_seq, cum_block):
    i = pl.program_id(0)
    j = pl.program_id(1)

    @pl.when(j == 0)
    def _():
        x = x_ref[...]
        xn_ref[...] = (x * _rms_scale(x) * g_ref[...]).astype(BF16)

    def forget_gates():
        @pl.when(i % tiles_per_seq == 0)
        def _():
            carry_ref[...] = jnp.zeros_like(carry_ref)

        z = _dot_nt(xn_ref[...], wf_ref[...]) + bf_ref[...]
        logf = jnp.minimum(z, 0.0) - jnp.log1p(jnp.exp(-jnp.abs(z)))
        r = lax.broadcasted_iota(jnp.int32, (cum_block, cum_block), 0)
        c = lax.broadcasted_iota(jnp.int32, (cum_block, cum_block), 1)
        tril = (c <= r).astype(BF16)
        local = []
        for blk in range(logf.shape[0] // cum_block):
            hi, mid, lo = _split3(logf[blk * cum_block:(blk + 1) * cum_block, :])
            local.append(_dot(tril, hi) + _dot(tril, mid) + _dot(tril, lo))
        carry = carry_ref[...]
        for blk, loc in enumerate(local):
            cum = loc + carry
            c_ref[blk * cum_block:(blk + 1) * cum_block, :] = cum
            carry = cum[cum_block - 1:cum_block, :]
        carry_ref[...] = carry

    heads = functools.partial(_project_heads, xn_ref, w_ref, hg_ref, hflag_ref, p_ref,
                              w_transposed=True)

    @pl.when(j == 1)
    def _():
        forget_gates()
        heads()

    @pl.when(j != 1)
    def _():
        heads()


def _norm_proj(x2d, gain, w, head_gain, head_flag, *, tm, tn, forget=None, seq_len=None):
    m, d = x2d.shape
    n = w.shape[1] if forget is None else w.shape[0]
    w_spec = (pl.BlockSpec((d, tn), lambda i, j: (0, j)) if forget is None
              else pl.BlockSpec((tn, d), lambda i, j: (j, 0)))
    in_specs = [
        pl.BlockSpec((tm, d), lambda i, j: (i, 0)),
        pl.BlockSpec((1, d), lambda i, j: (0, 0)),
        w_spec,
        pl.BlockSpec((1, tn), lambda i, j: (0, j)),
        pl.BlockSpec((1, tn), lambda i, j: (0, j)),
    ]
    args = [x2d, gain.reshape(1, d), w, head_gain.reshape(1, n), head_flag.reshape(1, n)]
    p_spec = pl.BlockSpec((tm, tn), lambda i, j: (i, j))
    p_shape = jax.ShapeDtypeStruct((m, n), BF16)
    params = pltpu.CompilerParams(dimension_semantics=("arbitrary", "arbitrary"),
                                  vmem_limit_bytes=VMEM_LIMIT_BYTES)
    if forget is None:
        return pl.pallas_call(
            _proj_kernel, grid=(m // tm, n // tn), in_specs=in_specs, out_specs=p_spec,
            out_shape=p_shape, scratch_shapes=[pltpu.VMEM((tm, d), BF16)],
            compiler_params=params, name="mem_proj")(*args)
    wf, bf = forget
    assert n // tn >= 2
    in_specs += [pl.BlockSpec((LANES, d), lambda i, j: (0, 0)),
                 pl.BlockSpec((1, LANES), lambda i, j: (0, 0))]
    kern = functools.partial(_proj_forget_kernel, tiles_per_seq=seq_len // tm,
                             cum_block=min(tm, 256))
    return pl.pallas_call(
        kern, grid=(m // tm, n // tn), in_specs=in_specs,
        out_specs=[p_spec, pl.BlockSpec((tm, LANES), lambda i, j: (i, 0))],
        out_shape=[p_shape, jax.ShapeDtypeStruct((m, LANES), F32)],
        scratch_shapes=[pltpu.VMEM((tm, d), BF16), pltpu.VMEM((1, LANES), F32)],
        compiler_params=params, name="mix_proj")(*args, wf, bf)


FORGET_LANES = 6


def _lane_set(lanes, offset):
    hit = lanes < 0
    for hh in range(FOX_HEADS):
        lo = FORGET_LANES * hh + offset
        hit = hit | ((lanes >= lo) & (lanes < lo + 3))
    return hit


def _fox_kernel(q_ref, k_ref, v_ref, c_ref, o_ref, qc_ref, kc_ref, kaug_ref, vaug_ref, s_ref,
                *, seq_len):
    h = pl.program_id(1)
    tq = ATTN_Q_BLOCK
    nq = seq_len // tq
    r = lax.broadcasted_iota(jnp.int32, (tq, tq), 0)
    c = lax.broadcasted_iota(jnp.int32, (tq, tq), 1)
    causal = c <= r

    @pl.when(h == 0)
    def _():
        hi, mid, lo = _split3(c_ref[...] * LOG2E)
        row = lax.broadcasted_iota(jnp.int32, (3 * LANES, 2 * LANES), 0)
        col = lax.broadcasted_iota(jnp.int32, (3 * LANES, 2 * LANES), 1)
        piece, head = row >> 7, row & (LANES - 1)
        slot = FORGET_LANES * head + piece
        route = jnp.where((head < FOX_HEADS) & (col == slot), 1.0,
                          jnp.where((head < FOX_HEADS) & (col == LANES + 3 + slot), -1.0, 0.0))
        routed = _dot(jnp.concatenate([hi, mid, lo], axis=1), route.astype(BF16))
        lane1 = lax.broadcasted_iota(jnp.int32, (1, LANES), 1)
        qc_ref[...] = (routed[:, :LANES] + jnp.where(_lane_set(lane1, 3), 1.0, 0.0)).astype(BF16)
        kc_ref[...] = (routed[:, LANES:] + jnp.where(_lane_set(lane1, 0), 1.0, 0.0)).astype(BF16)
        vaug_ref[:, HEAD_DIM:] = jnp.ones((seq_len, HEAD_DIM), BF16)

    lanes = lax.broadcasted_iota(jnp.int32, (seq_len, LANES), 1)
    mine = (lanes >= FORGET_LANES * h) & (lanes < FORGET_LANES * (h + 1))
    kaug_ref[:, :HEAD_DIM] = k_ref[...]
    kaug_ref[:, HEAD_DIM:] = jnp.where(mine, kc_ref[...], jnp.zeros((), BF16))
    vaug_ref[:, :HEAD_DIM] = v_ref[...]

    def qk(i):
        n = (i + 1) * tq
        rows = slice(i * tq, (i + 1) * tq)
        q_aug = jnp.concatenate([q_ref[rows, :], qc_ref[rows, :]], axis=1)
        s_ref[i % 2, :, 0:n] = _dot_nt(q_aug, kaug_ref[0:n, :])

    qk(nq - 1)
    for i in reversed(range(nq)):
        n = (i + 1) * tq
        if i > 0:
            qk(i - 1)
        logits = s_ref[i % 2, :, 0:n]
        own = jnp.where(causal, logits[:, i * tq:], NEG)
        if i > 0:
            logits = jnp.concatenate([logits[:, :i * tq], own], axis=1)
        else:
            logits = own
        o_ref[i * tq:(i + 1) * tq, :] = _softmax_pv(logits, vaug_ref[0:n, :]).astype(BF16)


def _fox_attn(p, cum, *, batch, seq_len):
    m = p.shape[0]
    hd = HEAD_DIM
    return pl.pallas_call(
        functools.partial(_fox_kernel, seq_len=seq_len),
        grid=(batch, FOX_HEADS),
        in_specs=[
            pl.BlockSpec((seq_len, hd), lambda b, h: (b, h)),
            pl.BlockSpec((seq_len, hd), lambda b, h: (b, FOX_HEADS + h)),
            pl.BlockSpec((seq_len, hd), lambda b, h: (b, 2 * FOX_HEADS + h)),
            pl.BlockSpec((seq_len, LANES), lambda b, h: (b, 0)),
        ],
        out_specs=pl.BlockSpec((seq_len, hd), lambda b, h: (b, h)),
        out_shape=jax.ShapeDtypeStruct((m, FOX_HEADS * hd), BF16),
        scratch_shapes=[pltpu.VMEM((seq_len, LANES), BF16),
                        pltpu.VMEM((seq_len, LANES), BF16),
                        pltpu.VMEM((seq_len, hd + LANES), BF16),
                        pltpu.VMEM((seq_len, 2 * hd), BF16),
                        pltpu.VMEM((2, ATTN_Q_BLOCK, seq_len), F32)],
        compiler_params=pltpu.CompilerParams(
            dimension_semantics=("arbitrary", "arbitrary"),
            vmem_limit_bytes=VMEM_LIMIT_BYTES),
        name="fox_attn",
    )(p, p, p, cum)


def _t5_bias(dist, rel_ref, h):
    n = jnp.maximum(dist, 0)
    max_exact = REL_BUCKETS // 2
    nf = jnp.maximum(n, 1).astype(F32)
    large = max_exact + (jnp.log(nf / max_exact) / math.log(REL_MAX_DIST / max_exact)
                         * (REL_BUCKETS - max_exact)).astype(jnp.int32)
    large = jnp.minimum(large, REL_BUCKETS - 1)
    bucket = jnp.where(n < max_exact, n, large)
    bias = jnp.zeros(dist.shape, F32)
    for b in range(REL_BUCKETS):
        bias = jnp.where(bucket == b, rel_ref[b, h], bias)
    return bias


def _moba_kernel(rel_ref, q_ref, k_ref, v_ref, o_ref, town_ref, tprev_ref, kaug_ref, qaug_ref,
                 vaug_ref, kmean_ref, s_ref, *, seq_len):
    h = pl.program_id(0)
    blk = MOBA_BLOCK
    nb = seq_len // blk
    r = lax.broadcasted_iota(jnp.int32, (blk, blk), 0)
    c = lax.broadcasted_iota(jnp.int32, (blk, blk), 1)
    lane_grp = lax.broadcasted_iota(jnp.int32, (1, LANES), 1) >> 3

    @pl.when(pl.program_id(1) == 0)
    def _():
        town_ref[...] = jnp.where(c <= r, _t5_bias(r - c, rel_ref, h) * LOG2E, NEG)
        tprev_ref[...] = _t5_bias(r - c + blk, rel_ref, h) * LOG2E
        s = lax.broadcasted_iota(jnp.int32, (seq_len, LANES), 0)
        ln = lax.broadcasted_iota(jnp.int32, (seq_len, LANES), 1)
        first_key = (ln & 7) * blk
        onehot = (ln < 32) & (s >= first_key) & (s < first_key + blk)
        kaug_ref[:, HEAD_DIM:] = jnp.where(onehot, 1.0, 0.0).astype(BF16)
        vaug_ref[:, HEAD_DIM:] = jnp.ones((seq_len, HEAD_DIM), BF16)

    kaug_ref[:, :HEAD_DIM] = k_ref[...]
    vaug_ref[:, :HEAD_DIM] = v_ref[...]

    kmean_ref[...] = jnp.zeros_like(kmean_ref)
    for j in range(nb):
        kmean_ref[j:j + 1, :] = jnp.mean(k_ref[j * blk:(j + 1) * blk, :].astype(F32),
                                         axis=0, keepdims=True)
    km_hi, km_mid, km_lo = _split3(kmean_ref[...])

    far = jnp.full((1, LANES), rel_ref[REL_BUCKETS - 1, h] * LOG2E, F32)
    far_hi, far_mid, far_lo = _split3(far)
    far_parts = jnp.where(lane_grp == 1, far_hi.astype(F32),
                          jnp.where(lane_grp == 2, far_mid.astype(F32),
                                    jnp.where(lane_grp == 3, far_lo.astype(F32), 0.0)))

    def far_bias(row0, nrows):
        rows = row0 + lax.broadcasted_iota(jnp.int32, (nrows, LANES), 0)
        lanes = lax.broadcasted_iota(jnp.int32, (nrows, LANES), 1)
        own = rows >> (blk.bit_length() - 1)
        return jnp.where((lanes & 7) <= own - 2, far_parts, 0.0), lanes, own

    qaug_ref[:, :HEAD_DIM] = q_ref[...]
    late = min(seq_len, (MOBA_TOPK + 1) * blk)
    qaug_ref[:late, HEAD_DIM:] = far_bias(0, late)[0].astype(BF16)
    if seq_len > late:
        ql = q_ref[late:, :]
        gate = _dot_nt(ql, km_hi) + _dot_nt(ql, km_mid) + _dot_nt(ql, km_lo)
        aug, lanes_l, own_l = far_bias(late, seq_len - late)
        valid = lanes_l < own_l
        g = jnp.where(valid, gate, NEG)
        sel = jnp.zeros(g.shape, jnp.bool_)
        lane_f = lanes_l.astype(F32)
        for _ in range(MOBA_TOPK):
            best = jnp.max(g, axis=-1, keepdims=True)
            first = jnp.min(jnp.where(g == best, lane_f, float(LANES)), axis=-1, keepdims=True)
            pick = lane_f == first
            sel = sel | pick
            g = jnp.where(pick, -jnp.inf, g)
        qaug_ref[late:, HEAD_DIM:] = jnp.where(valid & jnp.logical_not(sel), NEG, aug).astype(BF16)

    def qk(i):
        n = (i + 1) * blk
        s_ref[i % 2, :, 0:n] = _dot_nt(qaug_ref[i * blk:(i + 1) * blk, :], kaug_ref[0:n, :])

    n_early = min(nb, MOBA_TOPK + 1)
    order = list(reversed(range(n_early))) + list(reversed(range(n_early, nb)))
    qk(order[0])
    for pos, i in enumerate(order):
        n = (i + 1) * blk
        if pos + 1 < nb:
            qk(order[pos + 1])
        s = s_ref[i % 2, :, 0:n]
        pieces = [s[:, i * blk:] + town_ref[...]]
        if i >= 1:
            pieces.insert(0, s[:, (i - 1) * blk:i * blk] + tprev_ref[...])
        if i >= 2:
            pieces.insert(0, s[:, :(i - 1) * blk])
        logits = jnp.concatenate(pieces, axis=1) if len(pieces) > 1 else pieces[0]
        o_ref[i * blk:(i + 1) * blk, :] = _softmax_pv(logits, vaug_ref[0:n, :]).astype(BF16)


def _moba_attn(p, rel_bias, *, batch, seq_len, col0):
    m = p.shape[0]
    hd = HEAD_DIM
    blk = MOBA_BLOCK
    assert seq_len % blk == 0 and seq_len // blk <= 8
    return pl.pallas_call(
        functools.partial(_moba_kernel, seq_len=seq_len),
        grid=(MOBA_HEADS, batch),
        in_specs=[
            pl.BlockSpec(memory_space=pltpu.SMEM),
            pl.BlockSpec((seq_len, hd), lambda h, b: (b, col0 + h)),
            pl.BlockSpec((seq_len, hd), lambda h, b: (b, col0 + MOBA_HEADS + h)),
            pl.BlockSpec((seq_len, hd), lambda h, b: (b, col0 + 2 * MOBA_HEADS + h)),
        ],
        out_specs=pl.BlockSpec((seq_len, hd), lambda h, b: (b, h)),
        out_shape=jax.ShapeDtypeStruct((m, MOBA_HEADS * hd), BF16),
        scratch_shapes=[
            pltpu.VMEM((blk, blk), F32),
            pltpu.VMEM((blk, blk), F32),
            pltpu.VMEM((seq_len, hd + LANES), BF16),
            pltpu.VMEM((seq_len, hd + LANES), BF16),
            pltpu.VMEM((seq_len, 2 * hd), BF16),
            pltpu.VMEM((LANES, hd), F32),
            pltpu.VMEM((2, blk, seq_len), F32),
        ],
        compiler_params=pltpu.CompilerParams(
            dimension_semantics=("arbitrary", "arbitrary"),
            vmem_limit_bytes=VMEM_LIMIT_BYTES),
        name="moba_attn",
    )(rel_bias, p, p, p)


def _mem_kernel(q_ref, k_ref, v_ref, o_ref, *, seq_len, tq):
    k = k_ref[...]
    v_ones = jnp.concatenate([v_ref[...], jnp.ones(v_ref.shape, BF16)], axis=1)
    for i in range(seq_len // tq):
        rows = slice(i * tq, (i + 1) * tq)
        o_ref[rows, :] = _softmax_pv(_dot_nt(q_ref[rows, :], k), v_ones).astype(BF16)


def _mem_attn(p, mkv, *, batch, seq_len, n_mem, col0):
    m = p.shape[0]
    hd = HEAD_DIM
    return pl.pallas_call(
        functools.partial(_mem_kernel, seq_len=seq_len, tq=min(seq_len, 512)),
        grid=(batch, MEM_HEADS),
        in_specs=[
            pl.BlockSpec((seq_len, hd), lambda b, h: (b, col0 + h)),
            pl.BlockSpec((n_mem, hd), lambda b, h: (b, h)),
            pl.BlockSpec((n_mem, hd), lambda b, h: (b, MEM_HEADS + h)),
        ],
        out_specs=pl.BlockSpec((seq_len, hd), lambda b, h: (b, h)),
        out_shape=jax.ShapeDtypeStruct((m, MEM_HEADS * hd), BF16),
        compiler_params=pltpu.CompilerParams(
            dimension_semantics=("arbitrary", "arbitrary"),
            vmem_limit_bytes=VMEM_LIMIT_BYTES),
        name="mem_attn",
    )(p, mkv, mkv)


def _out_proj_kernel(x_ref, of_ref, ob_ref, om_ref, w_ref, o_ref):
    wf = of_ref.shape[1]
    wb = ob_ref.shape[1]
    acc = _dot(of_ref[...], w_ref[0:wf, :])
    acc += _dot(ob_ref[...], w_ref[wf:wf + wb, :])
    acc += _dot(om_ref[...], w_ref[wf + wb:, :])
    o_ref[...] = x_ref[...] + acc


def _out_proj(x2d, o_fox, o_moba, o_mem, w_out, *, tm):
    m, d = x2d.shape
    return pl.pallas_call(
        _out_proj_kernel,
        grid=(m // tm,),
        in_specs=[
            pl.BlockSpec((tm, d), lambda i: (i, 0)),
            pl.BlockSpec((tm, o_fox.shape[1]), lambda i: (i, 0)),
            pl.BlockSpec((tm, o_moba.shape[1]), lambda i: (i, 0)),
            pl.BlockSpec((tm, o_mem.shape[1]), lambda i: (i, 0)),
            pl.BlockSpec(w_out.shape, lambda i: (0, 0)),
        ],
        out_specs=pl.BlockSpec((tm, d), lambda i: (i, 0)),
        out_shape=jax.ShapeDtypeStruct((m, d), F32),
        compiler_params=pltpu.CompilerParams(
            dimension_semantics=("arbitrary",), vmem_limit_bytes=VMEM_LIMIT_BYTES),
        name="out_proj",
    )(x2d, o_fox, o_moba, o_mem, w_out)


def _tile(total, want):
    t = min(total, want)
    assert total % t == 0, (total, want)
    return t


def kernel(x, mem, ffn1_norm, ffn1_w1, ffn1_w3, ffn1_w2, mix_norm, mem_norm, w_in, b_forget,
           w_mem_kv, fox_q_gain, fox_k_gain, moba_q_gain, moba_k_gain, mem_q_gain, mem_k_gain,
           w_out, ffn2_norm, ffn2_w1, ffn2_w3, ffn2_w2, rel_bias):
    batch, seq_len, d = x.shape
    n_mem = mem.shape[1]
    depth = w_in.shape[0]
    fox_w = FOX_HEADS * HEAD_DIM
    moba_w = MOBA_HEADS * HEAD_DIM
    mem_w = MEM_HEADS * HEAD_DIM
    m = batch * seq_len
    tm = _tile(seq_len, 1024)
    tm_out = _tile(seq_len, 512)
    ones = jnp.ones((HEAD_DIM,), F32)
    q_scale = HEAD_DIM ** -0.5 * LOG2E

    x2d = x.reshape(m, d)
    mem2d = mem.reshape(batch * n_mem, d)
    for l in range(depth):
        tf = _tile(ffn1_w1.shape[2], 512)
        n_i, n_f = m // tm, ffn1_w1.shape[2] // tf
        d_ff2 = ffn2_w1.shape[2]
        rows, ff_cols = d // n_i, d_ff2 // n_f
        n_steps = n_i * n_f
        jobs = (
            _cast_job(ffn2_w1[l], (rows, ff_cols), lambda i, f: (i, f)),
            _cast_job(ffn2_w3[l], (rows, ff_cols), lambda i, f: (i, f)),
            _cast_job(ffn2_w2[l], (ff_cols, rows), lambda i, f: (f, i)),
            _cast_job(w_out[l], (_row_slabs(w_out.shape[1], n_f, n_steps)[0], d),
                      _row_slabs(w_out.shape[1], n_f, n_steps)[1]),
            _cast_job(w_mem_kv[l], (_row_slabs(d, n_f, n_steps)[0], w_mem_kv.shape[2]),
                      _row_slabs(d, n_f, n_steps)[1]),
        ) + _w_in_jobs(jnp.swapaxes(w_in[l], 0, 1), n_f, n_steps)
        head, w1_1, w3_1, w2_1 = _ffn_head(x2d, ffn1_norm[l], ffn1_w1[l], ffn1_w3[l], ffn1_w2[l],
                                           tm=tm, tf=_tile(ffn1_w1.shape[2], 256))
        x2d, (w1_2, w3_2, w2_2, w_out_bf, w_mem_bf, w_main, w_forget) = _ffn(
            x2d, ffn1_norm[l], w1_1, w3_1, w2_1, tm=tm, tf=tf, jobs=jobs, head=head)

        b_pad = jnp.pad(b_forget[l].astype(F32), (0, LANES - FOX_HEADS)).reshape(1, LANES)
        head_gain = jnp.concatenate(
            [jnp.tile(fox_q_gain[l] * q_scale, FOX_HEADS), jnp.tile(fox_k_gain[l], FOX_HEADS),
             jnp.tile(ones, FOX_HEADS), jnp.tile(moba_q_gain[l] * q_scale, MOBA_HEADS),
             jnp.tile(moba_k_gain[l], MOBA_HEADS), jnp.tile(ones, MOBA_HEADS),
             jnp.tile(mem_q_gain[l] * q_scale, MEM_HEADS)]).astype(F32)
        head_flag = jnp.concatenate(
            [jnp.ones((2 * fox_w,), F32), jnp.zeros((fox_w,), F32), jnp.ones((2 * moba_w,), F32),
             jnp.zeros((moba_w,), F32), jnp.ones((mem_w,), F32)])
        proj, cum = _norm_proj(x2d, mix_norm[l], w_main, head_gain, head_flag, tm=tm, tn=2560,
                               forget=(w_forget, b_pad), seq_len=seq_len)

        kv_gain = jnp.concatenate([jnp.tile(mem_k_gain[l], MEM_HEADS), jnp.tile(ones, MEM_HEADS)])
        kv_flag = jnp.concatenate([jnp.ones((mem_w,), F32), jnp.zeros((mem_w,), F32)])
        mkv = _norm_proj(mem2d, mem_norm[l], w_mem_bf, kv_gain.astype(F32), kv_flag,
                         tm=_tile(batch * n_mem, 512), tn=512)

        o_fox = _fox_attn(proj, cum, batch=batch, seq_len=seq_len)
        o_moba = _moba_attn(proj, rel_bias.astype(F32), batch=batch, seq_len=seq_len,
                            col0=3 * FOX_HEADS)
        o_mem = _mem_attn(proj, mkv, batch=batch, seq_len=seq_len, n_mem=n_mem,
                          col0=3 * FOX_HEADS + 3 * MOBA_HEADS)
        x2d = _out_proj(x2d, o_fox, o_moba, o_mem, w_out_bf, tm=tm_out)

        x2d, _ = _ffn(x2d, ffn2_norm[l], w1_2, w3_2, w2_2, tm=tm, tf=_tile(d_ff2, 512))
    return x2d.reshape(batch, seq_len, d)
```

```python
import functools
import math

import jax
import jax.numpy as jnp
from jax import lax
from jax.experimental import pallas as pl
from jax.experimental.pallas import tpu as pltpu

HEAD_DIM = 128
FOX_HEADS = 8
MOBA_HEADS = 4
MEM_HEADS = 4
MOBA_BLOCK = 256
MOBA_TOPK = 3
REL_BUCKETS = 32
REL_MAX_DIST = 128
EPS = 1e-6
NEG = -1e30
LOG2E = math.log2(math.e)

LANES = 128
BF16_SUBLANES = 16
MXU_WIDTH = 256
ATTN_Q_BLOCK = 256
VMEM_LIMIT_BYTES = 60 * 1024 * 1024

F32 = jnp.float32
BF16 = jnp.bfloat16


def _rms_scale(x):
    return lax.rsqrt(jnp.mean(x * x, axis=-1, keepdims=True) + EPS)


def _dot(a, b):
    return jnp.dot(a, b, preferred_element_type=F32)


def _dot_nt(a, b):
    return lax.dot_general(a, b, (((1,), (1,)), ((), ())), preferred_element_type=F32)


def _split3(x):
    hi = x.astype(BF16)
    r1 = x - hi.astype(F32)
    mid = r1.astype(BF16)
    lo = (r1 - mid.astype(F32)).astype(BF16)
    return hi, mid, lo


def _softmax_pv(logits2, v_ones):
    mx = jnp.max(logits2, axis=-1, keepdims=True)
    p = jnp.exp2(logits2 - mx).astype(BF16)
    pv = _dot(p, v_ones)
    return pv[:, :HEAD_DIM] / pv[:, HEAD_DIM:]


class _SideJob:
    def __init__(self, inputs, outputs, body):
        self.inputs, self.outputs, self.body = inputs, outputs, body


def _norm_to(x_ref, g_ref, xn_ref, o_ref):
    x = x_ref[...]
    xn_ref[...] = (x * _rms_scale(x) * g_ref[...]).astype(BF16)
    o_ref[...] = x


def _swiglu_half(xn, w1, w3, w2):
    chunks = [slice(c, c + MXU_WIDTH) for c in range(0, w1.shape[1], MXU_WIDTH)]
    ups = [(_dot(xn, w1[:, c]), _dot(xn, w3[:, c])) for c in chunks]
    out = None
    for c, (h1, h3) in zip(chunks, ups):
        act = (0.5 * h1 * jax.nn.sigmoid(h1) * h3).astype(BF16)
        part = _dot(act, w2[c, :])
        out = part if out is None else out + part
    return out


def _ffn_head_kernel(x_ref, g_ref, w1_ref, w3_ref, w2_ref, o_ref, w1b_ref, w3b_ref, w2b_ref, xn_ref):
    pl.when(pl.program_id(0) == 0)(functools.partial(_norm_to, x_ref, g_ref, xn_ref, o_ref))
    w1b_ref[...] = w1_ref[...].astype(BF16)
    w3b_ref[...] = w3_ref[...].astype(BF16)
    w2b_ref[...] = w2_ref[...].astype(BF16)
    o_ref[...] += _swiglu_half(xn_ref[...], w1b_ref[...], w3b_ref[...], w2b_ref[...])


def _ffn_head(x2d, gain, w1, w3, w2, *, tm, tf):
    d = x2d.shape[1]
    d_ff = w1.shape[1]
    up = pl.BlockSpec((d, tf), lambda f: (0, f))
    down = pl.BlockSpec((tf, d), lambda f: (f, 0))
    row = pl.BlockSpec((tm, d), lambda f: (0, 0))
    return pl.pallas_call(
        _ffn_head_kernel,
        grid=(d_ff // tf,),
        in_specs=[row, pl.BlockSpec((1, d), lambda f: (0, 0)), up, up, down],
        out_specs=[row, up, up, down],
        out_shape=[jax.ShapeDtypeStruct((tm, d), F32), jax.ShapeDtypeStruct(w1.shape, BF16),
                   jax.ShapeDtypeStruct(w3.shape, BF16), jax.ShapeDtypeStruct(w2.shape, BF16)],
        scratch_shapes=[pltpu.VMEM((tm, d), BF16)],
        compiler_params=pltpu.CompilerParams(
            dimension_semantics=("arbitrary",), vmem_limit_bytes=VMEM_LIMIT_BYTES),
        name="ffn_head",
    )(x2d, gain.reshape(1, d), w1, w3, w2)


def _ffn_kernel(*refs, jobs, has_head):
    x_ref, g_ref, w1_ref, w3_ref, w2_ref = refs[:5]
    n_in = 5 + has_head + sum(len(j.inputs) for j in jobs)
    side_in = refs[5 + has_head:n_in]
    o_ref = refs[n_in]
    side_out = refs[n_in + 1:-1]
    xn_ref = refs[-1]
    i = pl.program_id(0)
    first = pl.program_id(1) == 0

    def tile():
        pl.when(first)(functools.partial(_norm_to, x_ref, g_ref, xn_ref, o_ref))
        o_ref[...] += _swiglu_half(xn_ref[...], w1_ref[...], w3_ref[...], w2_ref[...])

    if has_head:
        head_ref = refs[5]
        pl.when((i == 0) & first)(lambda: pltpu.sync_copy(head_ref, o_ref))
        pl.when(i > 0)(tile)
    else:
        tile()

    for job in jobs:
        ins, side_in = side_in[:len(job.inputs)], side_in[len(job.inputs):]
        outs, side_out = side_out[:len(job.outputs)], side_out[len(job.outputs):]
        job.body(ins, outs)


def _ffn(x2d, gain, w1, w3, w2, *, tm, tf, jobs=(), head=None):
    m, d = x2d.shape
    d_ff = w1.shape[1]
    has_head = head is not None
    col = (lambda i, f: jnp.where(i == 0, 0, f)) if has_head else (lambda i, f: f)
    side_in = [io for j in jobs for io in j.inputs]
    side_out = [io for j in jobs for io in j.outputs]
    res = pl.pallas_call(
        functools.partial(_ffn_kernel, jobs=jobs, has_head=has_head),
        grid=(m // tm, d_ff // tf),
        in_specs=[
            pl.BlockSpec((tm, d), lambda i, f: (i, 0)),
            pl.BlockSpec((1, d), lambda i, f: (0, 0)),
            pl.BlockSpec((d, tf), lambda i, f: (0, col(i, f))),
            pl.BlockSpec((d, tf), lambda i, f: (0, col(i, f))),
            pl.BlockSpec((tf, d), lambda i, f: (col(i, f), 0)),
        ] + [pl.BlockSpec(memory_space=pl.ANY)] * has_head + [spec for _, spec in side_in],
        out_specs=[pl.BlockSpec((tm, d), lambda i, f: (i, 0))] + [spec for _, spec in side_out],
        out_shape=[jax.ShapeDtypeStruct((m, d), F32)] + [struct for struct, _ in side_out],
        scratch_shapes=[pltpu.VMEM((tm, d), BF16)],
        compiler_params=pltpu.CompilerParams(
            dimension_semantics=("arbitrary", "arbitrary"),
            vmem_limit_bytes=VMEM_LIMIT_BYTES),
        name="ffn",
    )(x2d, gain.reshape(1, d), w1, w3, w2, *([head] if has_head else []),
      *[arr for arr, _ in side_in])
    return res[0], res[1:]


def _cast_job(w, block, index_map):
    def body(ins, outs):
        outs[0][...] = ins[0][...].astype(BF16)
    spec = pl.BlockSpec(block, index_map)
    return _SideJob([(w, spec)], [(jax.ShapeDtypeStruct(w.shape, BF16), spec)], body)


def _row_slabs(nrows, n_f, n_steps):
    rows = BF16_SUBLANES
    while nrows % rows or nrows // rows > n_steps:
        rows += BF16_SUBLANES
    last = nrows // rows - 1
    return rows, lambda i, f: (jnp.minimum(i * n_f + f, last), 0)


def _w_in_jobs(w_in_t, n_f, n_steps):
    in_w, d = w_in_t.shape
    f0 = 3 * FOX_HEADS * HEAD_DIM
    main_w = in_w - FOX_HEADS
    rows = BF16_SUBLANES
    while main_w % rows or f0 % rows or main_w // rows > n_steps:
        rows += BF16_SUBLANES
    last = main_w // rows - 1
    step = lambda i, f: jnp.minimum(i * n_f + f, last)
    per_row_block = rows // FOX_HEADS

    def main_body(ins, outs):
        t = jnp.minimum(pl.program_id(0) * n_f + pl.program_id(1), last)
        a = ins[0][...]
        shifted = jnp.concatenate([a[FOX_HEADS:], ins[1][...]], axis=0)
        outs[0][...] = jnp.where(t * rows >= f0, shifted, a).astype(BF16)

    def forget_body(ins, outs):
        pad = jnp.zeros((LANES - FOX_HEADS, d), F32)
        outs[0][...] = jnp.concatenate([ins[0][...], pad], axis=0).astype(BF16)

    main = _SideJob(
        [(w_in_t, pl.BlockSpec((rows, d), lambda i, f: (step(i, f), 0))),
         (w_in_t, pl.BlockSpec((FOX_HEADS, d), lambda i, f: ((step(i, f) + 1) * per_row_block, 0)))],
        [(jax.ShapeDtypeStruct((main_w, d), BF16),
          pl.BlockSpec((rows, d), lambda i, f: (step(i, f), 0)))],
        main_body)
    forget = _SideJob(
        [(w_in_t, pl.BlockSpec((FOX_HEADS, d), lambda i, f: (f0 // FOX_HEADS, 0)))],
        [(jax.ShapeDtypeStruct((LANES, d), BF16), pl.BlockSpec((LANES, d), lambda i, f: (0, 0)))],
        forget_body)
    return main, forget


def _project_heads(xn_ref, w_ref, hg_ref, hflag_ref, p_ref, w_transposed):
    xn = xn_ref[...]
    for sb in range(p_ref.shape[1] // MXU_WIDTH):
        if w_transposed:
            y = _dot_nt(xn, w_ref[sb * MXU_WIDTH:(sb + 1) * MXU_WIDTH, :])
        else:
            y = _dot(xn, w_ref[:, sb * MXU_WIDTH:(sb + 1) * MXU_WIDTH])
        for hh in range(MXU_WIDTH // HEAD_DIM):
            cols = slice(sb * MXU_WIDTH + hh * HEAD_DIM, sb * MXU_WIDTH + (hh + 1) * HEAD_DIM)
            yh = y[:, hh * HEAD_DIM:(hh + 1) * HEAD_DIM]
            normed = yh * _rms_scale(yh) * hg_ref[:, cols]
            p_ref[:, cols] = jnp.where(hflag_ref[:, cols] > 0.0, normed, yh).astype(BF16)


def _proj_kernel(x_ref, g_ref, w_ref, hg_ref, hflag_ref, p_ref, xn_ref):
    @pl.when(pl.program_id(1) == 0)
    def _():
        x = x_ref[...]
        xn_ref[...] = (x * _rms_scale(x) * g_ref[...]).astype(BF16)

    _project_heads(xn_ref, w_ref, hg_ref, hflag_ref, p_ref, w_transposed=False)


def _proj_forget_kernel(x_ref, g_ref, w_ref, hg_ref, hflag_ref, wf_ref, bf_ref,
                        p_ref, c_ref, xn_ref, carry_ref, *, tiles_per_seq, cum_block):
    i = pl.program_id(0)
    j = pl.program_id(1)

    @pl.when(j == 0)
    def _():
        x = x_ref[...]
        xn_ref[...] = (x * _rms_scale(x) * g_ref[...]).astype(BF16)

    def forget_gates():
        @pl.when(i % tiles_per_seq == 0)
        def _():
            carry_ref[...] = jnp.zeros_like(carry_ref)

        z = _dot_nt(xn_ref[...], wf_ref[...]) + bf_ref[...]
        logf = jnp.minimum(z, 0.0) - jnp.log1p(jnp.exp(-jnp.abs(z)))
        r = lax.broadcasted_iota(jnp.int32, (cum_block, cum_block), 0)
        c = lax.broadcasted_iota(jnp.int32, (cum_block, cum_block), 1)
        tril = (c <= r).astype(BF16)
        local = []
        for blk in range(logf.shape[0] // cum_block):
            hi, mid, lo = _split3(logf[blk * cum_block:(blk + 1) * cum_block, :])
            local.append(_dot(tril, hi) + _dot(tril, mid) + _dot(tril, lo))
        carry = carry_ref[...]
        for blk, loc in enumerate(local):
            cum = loc + carry
            c_ref[blk * cum_block:(blk + 1) * cum_block, :] = cum
            carry = cum[cum_block - 1:cum_block, :]
        carry_ref[...] = carry

    heads = functools.partial(_project_heads, xn_ref, w_ref, hg_ref, hflag_ref, p_ref,
                              w_transposed=True)

    @pl.when(j == 1)
    def _():
        forget_gates()
        heads()

    @pl.when(j != 1)
    def _():
        heads()


def _norm_proj(x2d, gain, w, head_gain, head_flag, *, tm, tn, forget=None, seq_len=None):
    m, d = x2d.shape
    n = w.shape[1] if forget is None else w.shape[0]
    w_spec = (pl.BlockSpec((d, tn), lambda i, j: (0, j)) if forget is None
              else pl.BlockSpec((tn, d), lambda i, j: (j, 0)))
    in_specs = [
        pl.BlockSpec((tm, d), lambda i, j: (i, 0)),
        pl.BlockSpec((1, d), lambda i, j: (0, 0)),
        w_spec,
        pl.BlockSpec((1, tn), lambda i, j: (0, j)),
        pl.BlockSpec((1, tn), lambda i, j: (0, j)),
    ]
    args = [x2d, gain.reshape(1, d), w, head_gain.reshape(1, n), head_flag.reshape(1, n)]
    p_spec = pl.BlockSpec((tm, tn), lambda i, j: (i, j))
    p_shape = jax.ShapeDtypeStruct((m, n), BF16)
    params = pltpu.CompilerParams(dimension_semantics=("arbitrary", "arbitrary"),
                                  vmem_limit_bytes=VMEM_LIMIT_BYTES)
    if forget is None:
        return pl.pallas_call(
            _proj_kernel, grid=(m // tm, n // tn), in_specs=in_specs, out_specs=p_spec,
            out_shape=p_shape, scratch_shapes=[pltpu.VMEM((tm, d), BF16)],
            compiler_params=params, name="mem_proj")(*args)
    wf, bf = forget
    assert n // tn >= 2
    in_specs += [pl.BlockSpec((LANES, d), lambda i, j: (0, 0)),
                 pl.BlockSpec((1, LANES), lambda i, j: (0, 0))]
    kern = functools.partial(_proj_forget_kernel, tiles_per_seq=seq_len // tm,
                             cum_block=min(tm, 256))
    return pl.pallas_call(
        kern, grid=(m // tm, n // tn), in_specs=in_specs,
        out_specs=[p_spec, pl.BlockSpec((tm, LANES), lambda i, j: (i, 0))],
        out_shape=[p_shape, jax.ShapeDtypeStruct((m, LANES), F32)],
        scratch_shapes=[pltpu.VMEM((tm, d), BF16), pltpu.VMEM((1, LANES), F32)],
        compiler_params=params, name="mix_proj")(*args, wf, bf)


FORGET_LANES = 6
FOX_HEADS_PER_STEP = 2
MOBA_ROWS_PER_STEP = 2


def _lane_set(lanes, offset):
    hit = lanes < 0
    for hh in range(FOX_HEADS):
        lo = FORGET_LANES * hh + offset
        hit = hit | ((lanes >= lo) & (lanes < lo + 3))
    return hit


def _fox_kernel(q_ref, k_ref, v_ref, c_ref, o_ref, qc_ref, kc_ref, kaug_ref, vaug_ref, s_ref,
                *, seq_len):
    step = pl.program_id(1)
    tq = ATTN_Q_BLOCK
    nq = seq_len // tq
    r = lax.broadcasted_iota(jnp.int32, (tq, tq), 0)
    c = lax.broadcasted_iota(jnp.int32, (tq, tq), 1)
    causal = c <= r

    @pl.when(step == 0)
    def _():
        hi, mid, lo = _split3(c_ref[...] * LOG2E)
        row = lax.broadcasted_iota(jnp.int32, (3 * LANES, 2 * LANES), 0)
        col = lax.broadcasted_iota(jnp.int32, (3 * LANES, 2 * LANES), 1)
        piece, head = row >> 7, row & (LANES - 1)
        slot = FORGET_LANES * head + piece
        route = jnp.where((head < FOX_HEADS) & (col == slot), 1.0,
                          jnp.where((head < FOX_HEADS) & (col == LANES + 3 + slot), -1.0, 0.0))
        routed = _dot(jnp.concatenate([hi, mid, lo], axis=1), route.astype(BF16))
        lane1 = lax.broadcasted_iota(jnp.int32, (1, LANES), 1)
        qc_ref[...] = (routed[:, :LANES] + jnp.where(_lane_set(lane1, 3), 1.0, 0.0)).astype(BF16)
        kc_ref[...] = (routed[:, LANES:] + jnp.where(_lane_set(lane1, 0), 1.0, 0.0)).astype(BF16)
        for hh in range(FOX_HEADS_PER_STEP):
            vaug_ref[hh, :, HEAD_DIM:] = jnp.ones((seq_len, HEAD_DIM), BF16)

    lanes = lax.broadcasted_iota(jnp.int32, (seq_len, LANES), 1)
    for hh in range(FOX_HEADS_PER_STEP):
        h = step * FOX_HEADS_PER_STEP + hh
        cols = slice(hh * HEAD_DIM, (hh + 1) * HEAD_DIM)
        mine = (lanes >= FORGET_LANES * h) & (lanes < FORGET_LANES * (h + 1))
        kaug_ref[hh, :, :HEAD_DIM] = k_ref[:, cols]
        kaug_ref[hh, :, HEAD_DIM:] = jnp.where(mine, kc_ref[...], jnp.zeros((), BF16))
        vaug_ref[hh, :, :HEAD_DIM] = v_ref[:, cols]

    work = [(hh, i) for hh in range(FOX_HEADS_PER_STEP) for i in reversed(range(nq))]

    def qk(pos):
        hh, i = work[pos]
        n = (i + 1) * tq
        rows = slice(i * tq, (i + 1) * tq)
        q_aug = jnp.concatenate([q_ref[rows, hh * HEAD_DIM:(hh + 1) * HEAD_DIM], qc_ref[rows, :]],
                                axis=1)
        s_ref[pos % 2, :, 0:n] = _dot_nt(q_aug, kaug_ref[hh, 0:n, :])

    qk(0)
    for pos, (hh, i) in enumerate(work):
        n = (i + 1) * tq
        if pos + 1 < len(work):
            qk(pos + 1)
        logits = s_ref[pos % 2, :, 0:n]
        own = jnp.where(causal, logits[:, i * tq:], NEG)
        if i > 0:
            logits = jnp.concatenate([logits[:, :i * tq], own], axis=1)
        else:
            logits = own
        o_ref[i * tq:(i + 1) * tq, hh * HEAD_DIM:(hh + 1) * HEAD_DIM] = _softmax_pv(
            logits, vaug_ref[hh, 0:n, :]).astype(BF16)


def _fox_attn(p, cum, *, batch, seq_len):
    m = p.shape[0]
    hd = HEAD_DIM
    per = FOX_HEADS_PER_STEP
    steps = FOX_HEADS // per
    return pl.pallas_call(
        functools.partial(_fox_kernel, seq_len=seq_len),
        grid=(batch, steps),
        in_specs=[
            pl.BlockSpec((seq_len, per * hd), lambda b, s: (b, s)),
            pl.BlockSpec((seq_len, per * hd), lambda b, s: (b, steps + s)),
            pl.BlockSpec((seq_len, per * hd), lambda b, s: (b, 2 * steps + s)),
            pl.BlockSpec((seq_len, LANES), lambda b, s: (b, 0)),
        ],
        out_specs=pl.BlockSpec((seq_len, per * hd), lambda b, s: (b, s)),
        out_shape=jax.ShapeDtypeStruct((m, FOX_HEADS * hd), BF16),
        scratch_shapes=[pltpu.VMEM((seq_len, LANES), BF16),
                        pltpu.VMEM((seq_len, LANES), BF16),
                        pltpu.VMEM((per, seq_len, hd + LANES), BF16),
                        pltpu.VMEM((per, seq_len, 2 * hd), BF16),
                        pltpu.VMEM((2, ATTN_Q_BLOCK, seq_len), F32)],
        compiler_params=pltpu.CompilerParams(
            dimension_semantics=("arbitrary", "arbitrary"),
            vmem_limit_bytes=VMEM_LIMIT_BYTES),
        name="fox_attn",
    )(p, p, p, cum)


def _t5_bias(dist, rel_ref, h):
    n = jnp.maximum(dist, 0)
    max_exact = REL_BUCKETS // 2
    nf = jnp.maximum(n, 1).astype(F32)
    large = max_exact + (jnp.log(nf / max_exact) / math.log(REL_MAX_DIST / max_exact)
                         * (REL_BUCKETS - max_exact)).astype(jnp.int32)
    large = jnp.minimum(large, REL_BUCKETS - 1)
    bucket = jnp.where(n < max_exact, n, large)
    bias = jnp.zeros(dist.shape, F32)
    for b in range(REL_BUCKETS):
        bias = jnp.where(bucket == b, rel_ref[b, h], bias)
    return bias


def _moba_kernel(rel_ref, q_ref, k_ref, v_ref, o_ref, town_ref, tprev_ref, kaug_ref, qaug_ref,
                 vaug_ref, kmean_ref, s_ref, *, seq_len):
    h = pl.program_id(0)
    blk = MOBA_BLOCK
    nb = seq_len // blk
    per = MOBA_ROWS_PER_STEP
    r = lax.broadcasted_iota(jnp.int32, (blk, blk), 0)
    c = lax.broadcasted_iota(jnp.int32, (blk, blk), 1)
    lane_grp = lax.broadcasted_iota(jnp.int32, (1, LANES), 1) >> 3

    @pl.when(pl.program_id(1) == 0)
    def _():
        town_ref[...] = jnp.where(c <= r, _t5_bias(r - c, rel_ref, h) * LOG2E, NEG)
        tprev_ref[...] = _t5_bias(r - c + blk, rel_ref, h) * LOG2E
        s = lax.broadcasted_iota(jnp.int32, (seq_len, LANES), 0)
        ln = lax.broadcasted_iota(jnp.int32, (seq_len, LANES), 1)
        first_key = (ln & 7) * blk
        onehot = (ln < 32) & (s >= first_key) & (s < first_key + blk)
        for bb in range(per):
            kaug_ref[bb, :, HEAD_DIM:] = jnp.where(onehot, 1.0, 0.0).astype(BF16)
            vaug_ref[bb, :, HEAD_DIM:] = jnp.ones((seq_len, HEAD_DIM), BF16)

    far = jnp.full((1, LANES), rel_ref[REL_BUCKETS - 1, h] * LOG2E, F32)
    far_hi, far_mid, far_lo = _split3(far)
    far_parts = jnp.where(lane_grp == 1, far_hi.astype(F32),
                          jnp.where(lane_grp == 2, far_mid.astype(F32),
                                    jnp.where(lane_grp == 3, far_lo.astype(F32), 0.0)))

    def far_bias(row0, nrows):
        rows = row0 + lax.broadcasted_iota(jnp.int32, (nrows, LANES), 0)
        lanes = lax.broadcasted_iota(jnp.int32, (nrows, LANES), 1)
        own = rows >> (blk.bit_length() - 1)
        return jnp.where((lanes & 7) <= own - 2, far_parts, 0.0), lanes, own

    late = min(seq_len, (MOBA_TOPK + 1) * blk)
    for bb in range(per):
        base = bb * seq_len
        kaug_ref[bb, :, :HEAD_DIM] = k_ref[base:base + seq_len, :]
        vaug_ref[bb, :, :HEAD_DIM] = v_ref[base:base + seq_len, :]
        qaug_ref[bb, :, :HEAD_DIM] = q_ref[base:base + seq_len, :]
        qaug_ref[bb, :late, HEAD_DIM:] = far_bias(0, late)[0].astype(BF16)
        if seq_len > late:
            kmean_ref[bb] = jnp.zeros(kmean_ref.shape[1:], F32)
            for j in range(nb):
                kmean_ref[bb, j:j + 1, :] = jnp.mean(
                    k_ref[base + j * blk:base + (j + 1) * blk, :].astype(F32), axis=0, keepdims=True)
            km_hi, km_mid, km_lo = _split3(kmean_ref[bb])
            ql = q_ref[base + late:base + seq_len, :]
            gate = _dot_nt(ql, km_hi) + _dot_nt(ql, km_mid) + _dot_nt(ql, km_lo)
            aug, lanes_l, own_l = far_bias(late, seq_len - late)
            valid = lanes_l < own_l
            g = jnp.where(valid, gate, NEG)
            sel = jnp.zeros(g.shape, jnp.bool_)
            lane_f = lanes_l.astype(F32)
            for _ in range(MOBA_TOPK):
                best = jnp.max(g, axis=-1, keepdims=True)
                first = jnp.min(jnp.where(g == best, lane_f, float(LANES)), axis=-1, keepdims=True)
                pick = lane_f == first
                sel = sel | pick
                g = jnp.where(pick, -jnp.inf, g)
            qaug_ref[bb, late:, HEAD_DIM:] = jnp.where(valid & jnp.logical_not(sel), NEG,
                                                       aug).astype(BF16)

    n_early = min(nb, MOBA_TOPK + 1)
    order = list(reversed(range(n_early))) + list(reversed(range(n_early, nb)))
    work = [(bb, i) for bb in range(per) for i in order]

    def qk(pos):
        bb, i = work[pos]
        n = (i + 1) * blk
        s_ref[pos % 2, :, 0:n] = _dot_nt(qaug_ref[bb, i * blk:(i + 1) * blk, :], kaug_ref[bb, 0:n, :])

    qk(0)
    for pos, (bb, i) in enumerate(work):
        n = (i + 1) * blk
        if pos + 1 < len(work):
            qk(pos + 1)
        s = s_ref[pos % 2, :, 0:n]
        pieces = [s[:, i * blk:] + town_ref[...]]
        if i >= 1:
            pieces.insert(0, s[:, (i - 1) * blk:i * blk] + tprev_ref[...])
        if i >= 2:
            pieces.insert(0, s[:, :(i - 1) * blk])
        logits = jnp.concatenate(pieces, axis=1) if len(pieces) > 1 else pieces[0]
        rows = slice(bb * seq_len + i * blk, bb * seq_len + (i + 1) * blk)
        o_ref[rows, :] = _softmax_pv(logits, vaug_ref[bb, 0:n, :]).astype(BF16)


def _moba_attn(p, rel_bias, *, batch, seq_len, col0):
    m = p.shape[0]
    hd = HEAD_DIM
    blk = MOBA_BLOCK
    per = MOBA_ROWS_PER_STEP
    assert seq_len % blk == 0 and seq_len // blk <= 8 and batch % per == 0
    rows = per * seq_len
    return pl.pallas_call(
        functools.partial(_moba_kernel, seq_len=seq_len),
        grid=(MOBA_HEADS, batch // per),
        in_specs=[
            pl.BlockSpec(memory_space=pltpu.SMEM),
            pl.BlockSpec((rows, hd), lambda h, b: (b, col0 + h)),
            pl.BlockSpec((rows, hd), lambda h, b: (b, col0 + MOBA_HEADS + h)),
            pl.BlockSpec((rows, hd), lambda h, b: (b, col0 + 2 * MOBA_HEADS + h)),
        ],
        out_specs=pl.BlockSpec((rows, hd), lambda h, b: (b, h)),
        out_shape=jax.ShapeDtypeStruct((m, MOBA_HEADS * hd), BF16),
        scratch_shapes=[
            pltpu.VMEM((blk, blk), F32),
            pltpu.VMEM((blk, blk), F32),
            pltpu.VMEM((per, seq_len, hd + LANES), BF16),
            pltpu.VMEM((per, seq_len, hd + LANES), BF16),
            pltpu.VMEM((per, seq_len, 2 * hd), BF16),
            pltpu.VMEM((per, LANES, hd), F32),
            pltpu.VMEM((2, blk, seq_len), F32),
        ],
        compiler_params=pltpu.CompilerParams(
            dimension_semantics=("arbitrary", "arbitrary"),
            vmem_limit_bytes=VMEM_LIMIT_BYTES),
        name="moba_attn",
    )(rel_bias, p, p, p)


def _mem_kernel(q_ref, k_ref, v_ref, o_ref, *, seq_len, tq):
    k = k_ref[...]
    v_ones = jnp.concatenate([v_ref[...], jnp.ones(v_ref.shape, BF16)], axis=1)
    for i in range(seq_len // tq):
        rows = slice(i * tq, (i + 1) * tq)
        o_ref[rows, :] = _softmax_pv(_dot_nt(q_ref[rows, :], k), v_ones).astype(BF16)


def _mem_attn(p, mkv, *, batch, seq_len, n_mem, col0):
    m = p.shape[0]
    hd = HEAD_DIM
    return pl.pallas_call(
        functools.partial(_mem_kernel, seq_len=seq_len, tq=min(seq_len, 512)),
        grid=(batch, MEM_HEADS),
        in_specs=[
            pl.BlockSpec((seq_len, hd), lambda b, h: (b, col0 + h)),
            pl.BlockSpec((n_mem, hd), lambda b, h: (b, h)),
            pl.BlockSpec((n_mem, hd), lambda b, h: (b, MEM_HEADS + h)),
        ],
        out_specs=pl.BlockSpec((seq_len, hd), lambda b, h: (b, h)),
        out_shape=jax.ShapeDtypeStruct((m, MEM_HEADS * hd), BF16),
        compiler_params=pltpu.CompilerParams(
            dimension_semantics=("arbitrary", "arbitrary"),
            vmem_limit_bytes=VMEM_LIMIT_BYTES),
        name="mem_attn",
    )(p, mkv, mkv)


def _out_proj_kernel(x_ref, of_ref, ob_ref, om_ref, w_ref, o_ref):
    wf = of_ref.shape[1]
    wb = ob_ref.shape[1]
    acc = _dot(of_ref[...], w_ref[0:wf, :])
    acc += _dot(ob_ref[...], w_ref[wf:wf + wb, :])
    acc += _dot(om_ref[...], w_ref[wf + wb:, :])
    o_ref[...] = x_ref[...] + acc


def _out_proj(x2d, o_fox, o_moba, o_mem, w_out, *, tm):
    m, d = x2d.shape
    return pl.pallas_call(
        _out_proj_kernel,
        grid=(m // tm,),
        in_specs=[
            pl.BlockSpec((tm, d), lambda i: (i, 0)),
            pl.BlockSpec((tm, o_fox.shape[1]), lambda i: (i, 0)),
            pl.BlockSpec((tm, o_moba.shape[1]), lambda i: (i, 0)),
            pl.BlockSpec((tm, o_mem.shape[1]), lambda i: (i, 0)),
            pl.BlockSpec(w_out.shape, lambda i: (0, 0)),
        ],
        out_specs=pl.BlockSpec((tm, d), lambda i: (i, 0)),
        out_shape=jax.ShapeDtypeStruct((m, d), F32),
        compiler_params=pltpu.CompilerParams(
            dimension_semantics=("arbitrary",), vmem_limit_bytes=VMEM_LIMIT_BYTES),
        name="out_proj",
    )(x2d, o_fox, o_moba, o_mem, w_out)


def _tile(total, want):
    t = min(total, want)
    assert total % t == 0, (total, want)
    return t


def kernel(x, mem, ffn1_norm, ffn1_w1, ffn1_w3, ffn1_w2, mix_norm, mem_norm, w_in, b_forget,
           w_mem_kv, fox_q_gain, fox_k_gain, moba_q_gain, moba_k_gain, mem_q_gain, mem_k_gain,
           w_out, ffn2_norm, ffn2_w1, ffn2_w3, ffn2_w2, rel_bias):
    batch, seq_len, d = x.shape
    n_mem = mem.shape[1]
    depth = w_in.shape[0]
    fox_w = FOX_HEADS * HEAD_DIM
    moba_w = MOBA_HEADS * HEAD_DIM
    mem_w = MEM_HEADS * HEAD_DIM
    m = batch * seq_len
    tm = _tile(seq_len, 1024)
    tm_out = _tile(seq_len, 512)
    ones = jnp.ones((HEAD_DIM,), F32)
    q_scale = HEAD_DIM ** -0.5 * LOG2E

    x2d = x.reshape(m, d)
    mem2d = mem.reshape(batch * n_mem, d)
    for l in range(depth):
        tf = _tile(ffn1_w1.shape[2], 512)
        n_i, n_f = m // tm, ffn1_w1.shape[2] // tf
        d_ff2 = ffn2_w1.shape[2]
        rows, ff_cols = d // n_i, d_ff2 // n_f
        n_steps = n_i * n_f
        jobs = (
            _cast_job(ffn2_w1[l], (rows, ff_cols), lambda i, f: (i, f)),
            _cast_job(ffn2_w3[l], (rows, ff_cols), lambda i, f: (i, f)),
            _cast_job(ffn2_w2[l], (ff_cols, rows), lambda i, f: (f, i)),
            _cast_job(w_out[l], (_row_slabs(w_out.shape[1], n_f, n_steps)[0], d),
                      _row_slabs(w_out.shape[1], n_f, n_steps)[1]),
            _cast_job(w_mem_kv[l], (_row_slabs(d, n_f, n_steps)[0], w_mem_kv.shape[2]),
                      _row_slabs(d, n_f, n_steps)[1]),
        ) + _w_in_jobs(jnp.swapaxes(w_in[l], 0, 1), n_f, n_steps)
        head, w1_1, w3_1, w2_1 = _ffn_head(x2d, ffn1_norm[l], ffn1_w1[l], ffn1_w3[l], ffn1_w2[l],
                                           tm=tm, tf=_tile(ffn1_w1.shape[2], 256))
        x2d, (w1_2, w3_2, w2_2, w_out_bf, w_mem_bf, w_main, w_forget) = _ffn(
            x2d, ffn1_norm[l], w1_1, w3_1, w2_1, tm=tm, tf=tf, jobs=jobs, head=head)

        b_pad = jnp.pad(b_forget[l].astype(F32), (0, LANES - FOX_HEADS)).reshape(1, LANES)
        head_gain = jnp.concatenate(
            [jnp.tile(fox_q_gain[l] * q_scale, FOX_HEADS), jnp.tile(fox_k_gain[l], FOX_HEADS),
             jnp.tile(ones, FOX_HEADS), jnp.tile(moba_q_gain[l] * q_scale, MOBA_HEADS),
             jnp.tile(moba_k_gain[l], MOBA_HEADS), jnp.tile(ones, MOBA_HEADS),
             jnp.tile(mem_q_gain[l] * q_scale, MEM_HEADS)]).astype(F32)
        head_flag = jnp.concatenate(
            [jnp.ones((2 * fox_w,), F32), jnp.zeros((fox_w,), F32), jnp.ones((2 * moba_w,), F32),
             jnp.zeros((moba_w,), F32), jnp.ones((mem_w,), F32)])
        proj, cum = _norm_proj(x2d, mix_norm[l], w_main, head_gain, head_flag, tm=tm, tn=2560,
                               forget=(w_forget, b_pad), seq_len=seq_len)

        kv_gain = jnp.concatenate([jnp.tile(mem_k_gain[l], MEM_HEADS), jnp.tile(ones, MEM_HEADS)])
        kv_flag = jnp.concatenate([jnp.ones((mem_w,), F32), jnp.zeros((mem_w,), F32)])
        mkv = _norm_proj(mem2d, mem_norm[l], w_mem_bf, kv_gain.astype(F32), kv_flag,
                         tm=_tile(batch * n_mem, 512), tn=512)

        o_fox = _fox_attn(proj, cum, batch=batch, seq_len=seq_len)
        o_moba = _moba_attn(proj, rel_bias.astype(F32), batch=batch, seq_len=seq_len,
                            col0=3 * FOX_HEADS)
        o_mem = _mem_attn(proj, mkv, batch=batch, seq_len=seq_len, n_mem=n_mem,
                          col0=3 * FOX_HEADS + 3 * MOBA_HEADS)
        x2d = _out_proj(x2d, o_fox, o_moba, o_mem, w_out_bf, tm=tm_out)

        x2d, _ = _ffn(x2d, ffn2_norm[l], w1_2, w3_2, w2_2, tm=tm, tf=_tile(d_ff2, 512))
    return x2d.reshape(batch, seq_len, d)
```

```python
import functools
import math

import jax
import jax.numpy as jnp
from jax import lax
from jax.experimental import pallas as pl
from jax.experimental.pallas import tpu as pltpu

HEAD_DIM = 128
FOX_HEADS = 8
MOBA_HEADS = 4
MEM_HEADS = 4
MOBA_BLOCK = 256
MOBA_TOPK = 3
REL_BUCKETS = 32
REL_MAX_DIST = 128
EPS = 1e-6
NEG = -1e30
LOG2E = math.log2(math.e)

LANES = 128
BF16_SUBLANES = 16
MXU_WIDTH = 256
ATTN_Q_BLOCK = 256
VMEM_LIMIT_BYTES = 60 * 1024 * 1024

F32 = jnp.float32
BF16 = jnp.bfloat16


def _rms_scale(x):
    return lax.rsqrt(jnp.mean(x * x, axis=-1, keepdims=True) + EPS)


def _dot(a, b):
    return jnp.dot(a, b, preferred_element_type=F32)


def _dot_nt(a, b):
    return lax.dot_general(a, b, (((1,), (1,)), ((), ())), preferred_element_type=F32)


def _split3(x):
    hi = x.astype(BF16)
    r1 = x - hi.astype(F32)
    mid = r1.astype(BF16)
    lo = (r1 - mid.astype(F32)).astype(BF16)
    return hi, mid, lo


def _softmax_pv(logits2, v_ones):
    mx = jnp.max(logits2, axis=-1, keepdims=True)
    p = jnp.exp2(logits2 - mx).astype(BF16)
    pv = _dot(p, v_ones)
    return pv[:, :HEAD_DIM] / pv[:, HEAD_DIM:]


class _SideJob:
    def __init__(self, inputs, outputs, body):
        self.inputs, self.outputs, self.body = inputs, outputs, body


def _norm_to(x_ref, g_ref, xn_ref, o_ref):
    x = x_ref[...]
    xn_ref[...] = (x * _rms_scale(x) * g_ref[...]).astype(BF16)
    o_ref[...] = x


def _swiglu_half(xn, w1, w3, w2):
    chunks = [slice(c, c + MXU_WIDTH) for c in range(0, w1.shape[1], MXU_WIDTH)]
    ups = [(_dot(xn, w1[:, c]), _dot(xn, w3[:, c])) for c in chunks]
    out = None
    for c, (h1, h3) in zip(chunks, ups):
        act = (0.5 * h1 * jax.nn.sigmoid(h1) * h3).astype(BF16)
        part = _dot(act, w2[c, :])
        out = part if out is None else out + part
    return out


def _ffn_head_kernel(x_ref, g_ref, w1_ref, w3_ref, w2_ref, o_ref, w1b_ref, w3b_ref, w2b_ref, xn_ref):
    pl.when(pl.program_id(0) == 0)(functools.partial(_norm_to, x_ref, g_ref, xn_ref, o_ref))
    w1b_ref[...] = w1_ref[...].astype(BF16)
    w3b_ref[...] = w3_ref[...].astype(BF16)
    w2b_ref[...] = w2_ref[...].astype(BF16)
    o_ref[...] += _swiglu_half(xn_ref[...], w1b_ref[...], w3b_ref[...], w2b_ref[...])


def _ffn_head(x2d, gain, w1, w3, w2, *, tm, tf):
    d = x2d.shape[1]
    d_ff = w1.shape[1]
    up = pl.BlockSpec((d, tf), lambda f: (0, f))
    down = pl.BlockSpec((tf, d), lambda f: (f, 0))
    row = pl.BlockSpec((tm, d), lambda f: (0, 0))
    return pl.pallas_call(
        _ffn_head_kernel,
        grid=(d_ff // tf,),
        in_specs=[row, pl.BlockSpec((1, d), lambda f: (0, 0)), up, up, down],
        out_specs=[row, up, up, down],
        out_shape=[jax.ShapeDtypeStruct((tm, d), F32), jax.ShapeDtypeStruct(w1.shape, BF16),
                   jax.ShapeDtypeStruct(w3.shape, BF16), jax.ShapeDtypeStruct(w2.shape, BF16)],
        scratch_shapes=[pltpu.VMEM((tm, d), BF16)],
        compiler_params=pltpu.CompilerParams(
            dimension_semantics=("arbitrary",), vmem_limit_bytes=VMEM_LIMIT_BYTES),
        name="ffn_head",
    )(x2d, gain.reshape(1, d), w1, w3, w2)


def _ffn_kernel(*refs, jobs, has_head):
    x_ref, g_ref, w1_ref, w3_ref, w2_ref = refs[:5]
    n_in = 5 + has_head + sum(len(j.inputs) for j in jobs)
    side_in = refs[5 + has_head:n_in]
    o_ref = refs[n_in]
    side_out = refs[n_in + 1:-1]
    xn_ref = refs[-1]
    i = pl.program_id(0)
    first = pl.program_id(1) == 0

    def tile():
        pl.when(first)(functools.partial(_norm_to, x_ref, g_ref, xn_ref, o_ref))
        o_ref[...] += _swiglu_half(xn_ref[...], w1_ref[...], w3_ref[...], w2_ref[...])

    if has_head:
        head_ref = refs[5]
        pl.when((i == 0) & first)(lambda: pltpu.sync_copy(head_ref, o_ref))
        pl.when(i > 0)(tile)
    else:
        tile()

    for job in jobs:
        ins, side_in = side_in[:len(job.inputs)], side_in[len(job.inputs):]
        outs, side_out = side_out[:len(job.outputs)], side_out[len(job.outputs):]
        job.body(ins, outs)


def _ffn(x2d, gain, w1, w3, w2, *, tm, tf, jobs=(), head=None):
    m, d = x2d.shape
    d_ff = w1.shape[1]
    has_head = head is not None
    col = (lambda i, f: jnp.where(i == 0, 0, f)) if has_head else (lambda i, f: f)
    side_in = [io for j in jobs for io in j.inputs]
    side_out = [io for j in jobs for io in j.outputs]
    res = pl.pallas_call(
        functools.partial(_ffn_kernel, jobs=jobs, has_head=has_head),
        grid=(m // tm, d_ff // tf),
        in_specs=[
            pl.BlockSpec((tm, d), lambda i, f: (i, 0)),
            pl.BlockSpec((1, d), lambda i, f: (0, 0)),
            pl.BlockSpec((d, tf), lambda i, f: (0, col(i, f))),
            pl.BlockSpec((d, tf), lambda i, f: (0, col(i, f))),
            pl.BlockSpec((tf, d), lambda i, f: (col(i, f), 0)),
        ] + [pl.BlockSpec(memory_space=pl.ANY)] * has_head + [spec for _, spec in side_in],
        out_specs=[pl.BlockSpec((tm, d), lambda i, f: (i, 0))] + [spec for _, spec in side_out],
        out_shape=[jax.ShapeDtypeStruct((m, d), F32)] + [struct for struct, _ in side_out],
        scratch_shapes=[pltpu.VMEM((tm, d), BF16)],
        compiler_params=pltpu.CompilerParams(
            dimension_semantics=("arbitrary", "arbitrary"),
            vmem_limit_bytes=VMEM_LIMIT_BYTES),
        name="ffn",
    )(x2d, gain.reshape(1, d), w1, w3, w2, *([head] if has_head else []),
      *[arr for arr, _ in side_in])
    return res[0], res[1:]


def _cast_job(w, block, index_map):
    def body(ins, outs):
        outs[0][...] = ins[0][...].astype(BF16)
    spec = pl.BlockSpec(block, index_map)
    return _SideJob([(w, spec)], [(jax.ShapeDtypeStruct(w.shape, BF16), spec)], body)


def _row_slabs(nrows, n_f, n_steps):
    rows = BF16_SUBLANES
    while nrows % rows or nrows // rows > n_steps:
        rows += BF16_SUBLANES
    last = nrows // rows - 1
    return rows, lambda i, f: (jnp.minimum(i * n_f + f, last), 0)


def _w_in_jobs(w_in_t, n_f, n_steps):
    in_w, d = w_in_t.shape
    f0 = 3 * FOX_HEADS * HEAD_DIM
    main_w = in_w - FOX_HEADS
    rows = BF16_SUBLANES
    while main_w % rows or f0 % rows or main_w // rows > n_steps:
        rows += BF16_SUBLANES
    last = main_w // rows - 1
    step = lambda i, f: jnp.minimum(i * n_f + f, last)
    per_row_block = rows // FOX_HEADS

    def main_body(ins, outs):
        t = jnp.minimum(pl.program_id(0) * n_f + pl.program_id(1), last)
        a = ins[0][...]
        shifted = jnp.concatenate([a[FOX_HEADS:], ins[1][...]], axis=0)
        outs[0][...] = jnp.where(t * rows >= f0, shifted, a).astype(BF16)

    def forget_body(ins, outs):
        pad = jnp.zeros((LANES - FOX_HEADS, d), F32)
        outs[0][...] = jnp.concatenate([ins[0][...], pad], axis=0).astype(BF16)

    main = _SideJob(
        [(w_in_t, pl.BlockSpec((rows, d), lambda i, f: (step(i, f), 0))),
         (w_in_t, pl.BlockSpec((FOX_HEADS, d), lambda i, f: ((step(i, f) + 1) * per_row_block, 0)))],
        [(jax.ShapeDtypeStruct((main_w, d), BF16),
          pl.BlockSpec((rows, d), lambda i, f: (step(i, f), 0)))],
        main_body)
    forget = _SideJob(
        [(w_in_t, pl.BlockSpec((FOX_HEADS, d), lambda i, f: (f0 // FOX_HEADS, 0)))],
        [(jax.ShapeDtypeStruct((LANES, d), BF16), pl.BlockSpec((LANES, d), lambda i, f: (0, 0)))],
        forget_body)
    return main, forget


def _project_heads(xn_ref, w_ref, hg_ref, hflag_ref, p_ref, w_transposed):
    xn = xn_ref[...]
    for sb in range(p_ref.shape[1] // MXU_WIDTH):
        if w_transposed:
            y = _dot_nt(xn, w_ref[sb * MXU_WIDTH:(sb + 1) * MXU_WIDTH, :])
        else:
            y = _dot(xn, w_ref[:, sb * MXU_WIDTH:(sb + 1) * MXU_WIDTH])
        for hh in range(MXU_WIDTH // HEAD_DIM):
            cols = slice(sb * MXU_WIDTH + hh * HEAD_DIM, sb * MXU_WIDTH + (hh + 1) * HEAD_DIM)
            yh = y[:, hh * HEAD_DIM:(hh + 1) * HEAD_DIM]
            normed = yh * _rms_scale(yh) * hg_ref[:, cols]
            p_ref[:, cols] = jnp.where(hflag_ref[:, cols] > 0.0, normed, yh).astype(BF16)


def _proj_kernel(x_ref, g_ref, w_ref, hg_ref, hflag_ref, p_ref, xn_ref):
    @pl.when(pl.program_id(1) == 0)
    def _():
        x = x_ref[...]
        xn_ref[...] = (x * _rms_scale(x) * g_ref[...]).astype(BF16)

    _project_heads(xn_ref, w_ref, hg_ref, hflag_ref, p_ref, w_transposed=False)


def _proj_forget_kernel(x_ref, g_ref, w_ref, hg_ref, hflag_ref, wf_ref, bf_ref,
                        p_ref, c_ref, xn_ref, carry_ref, *, tiles_per_seq, cum_block):
    i = pl.program_id(0)
    j = pl.program_id(1)

    @pl.when(j == 0)
    def _():
        x = x_ref[...]
        xn_ref[...] = (x * _rms_scale(x) * g_ref[...]).astype(BF16)

    def forget_gates():
        @pl.when(i % tiles_per_seq == 0)
        def _():
            carry_ref[...] = jnp.zeros_like(carry_ref)

        z = _dot_nt(xn_ref[...], wf_ref[...]) + bf_ref[...]
        logf = jnp.minimum(z, 0.0) - jnp.log1p(jnp.exp(-jnp.abs(z)))
        r = lax.broadcasted_iota(jnp.int32, (cum_block, cum_block), 0)
        c = lax.broadcasted_iota(jnp.int32, (cum_block, cum_block), 1)
        tril = (c <= r).astype(BF16)
        local = []
        for blk in range(logf.shape[0] // cum_block):
            hi, mid, lo = _split3(logf[blk * cum_block:(blk + 1) * cum_block, :])
            local.append(_dot(tril, hi) + _dot(tril, mid) + _dot(tril, lo))
        carry = carry_ref[...]
        for blk, loc in enumerate(local):
            cum = loc + carry
            c_ref[blk * cum_block:(blk + 1) * cum_block, :] = cum
            carry = cum[cum_block - 1:cum_block, :]
        carry_ref[...] = carry

    heads = functools.partial(_project_heads, xn_ref, w_ref, hg_ref, hflag_ref, p_ref,
                              w_transposed=True)

    @pl.when(j == 1)
    def _():
        forget_gates()
        heads()

    @pl.when(j != 1)
    def _():
        heads()


def _norm_proj(x2d, gain, w, head_gain, head_flag, *, tm, tn, forget=None, seq_len=None):
    m, d = x2d.shape
    n = w.shape[1] if forget is None else w.shape[0]
    w_spec = (pl.BlockSpec((d, tn), lambda i, j: (0, j)) if forget is None
              else pl.BlockSpec((tn, d), lambda i, j: (j, 0)))
    in_specs = [
        pl.BlockSpec((tm, d), lambda i, j: (i, 0)),
        pl.BlockSpec((1, d), lambda i, j: (0, 0)),
        w_spec,
        pl.BlockSpec((1, tn), lambda i, j: (0, j)),
        pl.BlockSpec((1, tn), lambda i, j: (0, j)),
    ]
    args = [x2d, gain.reshape(1, d), w, head_gain.reshape(1, n), head_flag.reshape(1, n)]
    p_spec = pl.BlockSpec((tm, tn), lambda i, j: (i, j))
    p_shape = jax.ShapeDtypeStruct((m, n), BF16)
    params = pltpu.CompilerParams(dimension_semantics=("arbitrary", "arbitrary"),
                                  vmem_limit_bytes=VMEM_LIMIT_BYTES)
    if forget is None:
        return pl.pallas_call(
            _proj_kernel, grid=(m // tm, n // tn), in_specs=in_specs, out_specs=p_spec,
            out_shape=p_shape, scratch_shapes=[pltpu.VMEM((tm, d), BF16)],
            compiler_params=params, name="mem_proj")(*args)
    wf, bf = forget
    assert n // tn >= 2
    in_specs += [pl.BlockSpec((LANES, d), lambda i, j: (0, 0)),
                 pl.BlockSpec((1, LANES), lambda i, j: (0, 0))]
    kern = functools.partial(_proj_forget_kernel, tiles_per_seq=seq_len // tm,
                             cum_block=min(tm, 256))
    return pl.pallas_call(
        kern, grid=(m // tm, n // tn), in_specs=in_specs,
        out_specs=[p_spec, pl.BlockSpec((tm, LANES), lambda i, j: (i, 0))],
        out_shape=[p_shape, jax.ShapeDtypeStruct((m, LANES), F32)],
        scratch_shapes=[pltpu.VMEM((tm, d), BF16), pltpu.VMEM((1, LANES), F32)],
        compiler_params=params, name="mix_proj")(*args, wf, bf)


FORGET_LANES = 6
FOX_HEADS_PER_STEP = 4
MOBA_ROWS_PER_STEP = 4


def _lane_set(lanes, offset):
    hit = lanes < 0
    for hh in range(FOX_HEADS):
        lo = FORGET_LANES * hh + offset
        hit = hit | ((lanes >= lo) & (lanes < lo + 3))
    return hit


def _fox_kernel(q_ref, k_ref, v_ref, c_ref, o_ref, qc_ref, kc_ref, kaug_ref, vaug_ref, s_ref,
                *, seq_len):
    step = pl.program_id(1)
    tq = ATTN_Q_BLOCK
    nq = seq_len // tq
    r = lax.broadcasted_iota(jnp.int32, (tq, tq), 0)
    c = lax.broadcasted_iota(jnp.int32, (tq, tq), 1)
    causal = c <= r

    @pl.when(step == 0)
    def _():
        hi, mid, lo = _split3(c_ref[...] * LOG2E)
        row = lax.broadcasted_iota(jnp.int32, (3 * LANES, 2 * LANES), 0)
        col = lax.broadcasted_iota(jnp.int32, (3 * LANES, 2 * LANES), 1)
        piece, head = row >> 7, row & (LANES - 1)
        slot = FORGET_LANES * head + piece
        route = jnp.where((head < FOX_HEADS) & (col == slot), 1.0,
                          jnp.where((head < FOX_HEADS) & (col == LANES + 3 + slot), -1.0, 0.0))
        routed = _dot(jnp.concatenate([hi, mid, lo], axis=1), route.astype(BF16))
        lane1 = lax.broadcasted_iota(jnp.int32, (1, LANES), 1)
        qc_ref[...] = (routed[:, :LANES] + jnp.where(_lane_set(lane1, 3), 1.0, 0.0)).astype(BF16)
        kc_ref[...] = (routed[:, LANES:] + jnp.where(_lane_set(lane1, 0), 1.0, 0.0)).astype(BF16)
        for hh in range(FOX_HEADS_PER_STEP):
            vaug_ref[hh, :, HEAD_DIM:] = jnp.ones((seq_len, HEAD_DIM), BF16)

    lanes = lax.broadcasted_iota(jnp.int32, (seq_len, LANES), 1)
    for hh in range(FOX_HEADS_PER_STEP):
        h = step * FOX_HEADS_PER_STEP + hh
        cols = slice(hh * HEAD_DIM, (hh + 1) * HEAD_DIM)
        mine = (lanes >= FORGET_LANES * h) & (lanes < FORGET_LANES * (h + 1))
        kaug_ref[hh, :, :HEAD_DIM] = k_ref[:, cols]
        kaug_ref[hh, :, HEAD_DIM:] = jnp.where(mine, kc_ref[...], jnp.zeros((), BF16))
        vaug_ref[hh, :, :HEAD_DIM] = v_ref[:, cols]

    work = [(hh, i) for hh in range(FOX_HEADS_PER_STEP) for i in reversed(range(nq))]

    def qk(pos):
        hh, i = work[pos]
        n = (i + 1) * tq
        rows = slice(i * tq, (i + 1) * tq)
        q_aug = jnp.concatenate([q_ref[rows, hh * HEAD_DIM:(hh + 1) * HEAD_DIM], qc_ref[rows, :]],
                                axis=1)
        s_ref[pos % 2, :, 0:n] = _dot_nt(q_aug, kaug_ref[hh, 0:n, :])

    qk(0)
    for pos, (hh, i) in enumerate(work):
        n = (i + 1) * tq
        if pos + 1 < len(work):
            qk(pos + 1)
        logits = s_ref[pos % 2, :, 0:n]
        own = jnp.where(causal, logits[:, i * tq:], NEG)
        if i > 0:
            logits = jnp.concatenate([logits[:, :i * tq], own], axis=1)
        else:
            logits = own
        o_ref[i * tq:(i + 1) * tq, hh * HEAD_DIM:(hh + 1) * HEAD_DIM] = _softmax_pv(
            logits, vaug_ref[hh, 0:n, :]).astype(BF16)


def _fox_attn(p, cum, *, batch, seq_len):
    m = p.shape[0]
    hd = HEAD_DIM
    per = FOX_HEADS_PER_STEP
    steps = FOX_HEADS // per
    return pl.pallas_call(
        functools.partial(_fox_kernel, seq_len=seq_len),
        grid=(batch, steps),
        in_specs=[
            pl.BlockSpec((seq_len, per * hd), lambda b, s: (b, s)),
            pl.BlockSpec((seq_len, per * hd), lambda b, s: (b, steps + s)),
            pl.BlockSpec((seq_len, per * hd), lambda b, s: (b, 2 * steps + s)),
            pl.BlockSpec((seq_len, LANES), lambda b, s: (b, 0)),
        ],
        out_specs=pl.BlockSpec((seq_len, per * hd), lambda b, s: (b, s)),
        out_shape=jax.ShapeDtypeStruct((m, FOX_HEADS * hd), BF16),
        scratch_shapes=[pltpu.VMEM((seq_len, LANES), BF16),
                        pltpu.VMEM((seq_len, LANES), BF16),
                        pltpu.VMEM((per, seq_len, hd + LANES), BF16),
                        pltpu.VMEM((per, seq_len, 2 * hd), BF16),
                        pltpu.VMEM((2, ATTN_Q_BLOCK, seq_len), F32)],
        compiler_params=pltpu.CompilerParams(
            dimension_semantics=("arbitrary", "arbitrary"),
            vmem_limit_bytes=VMEM_LIMIT_BYTES),
        name="fox_attn",
    )(p, p, p, cum)


def _t5_bias(dist, rel_ref, h):
    n = jnp.maximum(dist, 0)
    max_exact = REL_BUCKETS // 2
    nf = jnp.maximum(n, 1).astype(F32)
    large = max_exact + (jnp.log(nf / max_exact) / math.log(REL_MAX_DIST / max_exact)
                         * (REL_BUCKETS - max_exact)).astype(jnp.int32)
    large = jnp.minimum(large, REL_BUCKETS - 1)
    bucket = jnp.where(n < max_exact, n, large)
    bias = jnp.zeros(dist.shape, F32)
    for b in range(REL_BUCKETS):
        bias = jnp.where(bucket == b, rel_ref[b, h], bias)
    return bias


def _moba_kernel(rel_ref, q_ref, k_ref, v_ref, o_ref, town_ref, tprev_ref, kaug_ref, qaug_ref,
                 vaug_ref, kmean_ref, s_ref, *, seq_len):
    h = pl.program_id(0)
    blk = MOBA_BLOCK
    nb = seq_len // blk
    per = MOBA_ROWS_PER_STEP
    r = lax.broadcasted_iota(jnp.int32, (blk, blk), 0)
    c = lax.broadcasted_iota(jnp.int32, (blk, blk), 1)
    lane_grp = lax.broadcasted_iota(jnp.int32, (1, LANES), 1) >> 3

    @pl.when(pl.program_id(1) == 0)
    def _():
        town_ref[...] = jnp.where(c <= r, _t5_bias(r - c, rel_ref, h) * LOG2E, NEG)
        tprev_ref[...] = _t5_bias(r - c + blk, rel_ref, h) * LOG2E
        s = lax.broadcasted_iota(jnp.int32, (seq_len, LANES), 0)
        ln = lax.broadcasted_iota(jnp.int32, (seq_len, LANES), 1)
        first_key = (ln & 7) * blk
        onehot = (ln < 32) & (s >= first_key) & (s < first_key + blk)
        for bb in range(per):
            kaug_ref[bb, :, HEAD_DIM:] = jnp.where(onehot, 1.0, 0.0).astype(BF16)
            vaug_ref[bb, :, HEAD_DIM:] = jnp.ones((seq_len, HEAD_DIM), BF16)

    far = jnp.full((1, LANES), rel_ref[REL_BUCKETS - 1, h] * LOG2E, F32)
    far_hi, far_mid, far_lo = _split3(far)
    far_parts = jnp.where(lane_grp == 1, far_hi.astype(F32),
                          jnp.where(lane_grp == 2, far_mid.astype(F32),
                                    jnp.where(lane_grp == 3, far_lo.astype(F32), 0.0)))

    def far_bias(row0, nrows):
        rows = row0 + lax.broadcasted_iota(jnp.int32, (nrows, LANES), 0)
        lanes = lax.broadcasted_iota(jnp.int32, (nrows, LANES), 1)
        own = rows >> (blk.bit_length() - 1)
        return jnp.where((lanes & 7) <= own - 2, far_parts, 0.0), lanes, own

    late = min(seq_len, (MOBA_TOPK + 1) * blk)
    for bb in range(per):
        base = bb * seq_len
        kaug_ref[bb, :, :HEAD_DIM] = k_ref[base:base + seq_len, :]
        vaug_ref[bb, :, :HEAD_DIM] = v_ref[base:base + seq_len, :]
        qaug_ref[bb, :, :HEAD_DIM] = q_ref[base:base + seq_len, :]
        qaug_ref[bb, :late, HEAD_DIM:] = far_bias(0, late)[0].astype(BF16)
        if seq_len > late:
            kmean_ref[bb] = jnp.zeros(kmean_ref.shape[1:], F32)
            for j in range(nb):
                kmean_ref[bb, j:j + 1, :] = jnp.mean(
                    k_ref[base + j * blk:base + (j + 1) * blk, :].astype(F32), axis=0, keepdims=True)
            km_hi, km_mid, km_lo = _split3(kmean_ref[bb])
            ql = q_ref[base + late:base + seq_len, :]
            gate = _dot_nt(ql, km_hi) + _dot_nt(ql, km_mid) + _dot_nt(ql, km_lo)
            aug, lanes_l, own_l = far_bias(late, seq_len - late)
            valid = lanes_l < own_l
            g = jnp.where(valid, gate, NEG)
            sel = jnp.zeros(g.shape, jnp.bool_)
            lane_f = lanes_l.astype(F32)
            for _ in range(MOBA_TOPK):
                best = jnp.max(g, axis=-1, keepdims=True)
                first = jnp.min(jnp.where(g == best, lane_f, float(LANES)), axis=-1, keepdims=True)
                pick = lane_f == first
                sel = sel | pick
                g = jnp.where(pick, -jnp.inf, g)
            qaug_ref[bb, late:, HEAD_DIM:] = jnp.where(valid & jnp.logical_not(sel), NEG,
                                                       aug).astype(BF16)

    n_early = min(nb, MOBA_TOPK + 1)
    order = list(reversed(range(n_early))) + list(reversed(range(n_early, nb)))
    work = [(bb, i) for bb in range(per) for i in order]

    def qk(pos):
        bb, i = work[pos]
        n = (i + 1) * blk
        s_ref[pos % 2, :, 0:n] = _dot_nt(qaug_ref[bb, i * blk:(i + 1) * blk, :], kaug_ref[bb, 0:n, :])

    qk(0)
    for pos, (bb, i) in enumerate(work):
        n = (i + 1) * blk
        if pos + 1 < len(work):
            qk(pos + 1)
        s = s_ref[pos % 2, :, 0:n]
        pieces = [s[:, i * blk:] + town_ref[...]]
        if i >= 1:
            pieces.insert(0, s[:, (i - 1) * blk:i * blk] + tprev_ref[...])
        if i >= 2:
            pieces.insert(0, s[:, :(i - 1) * blk])
        logits = jnp.concatenate(pieces, axis=1) if len(pieces) > 1 else pieces[0]
        rows = slice(bb * seq_len + i * blk, bb * seq_len + (i + 1) * blk)
        o_ref[rows, :] = _softmax_pv(logits, vaug_ref[bb, 0:n, :]).astype(BF16)


def _moba_attn(p, rel_bias, *, batch, seq_len, col0):
    m = p.shape[0]
    hd = HEAD_DIM
    blk = MOBA_BLOCK
    per = MOBA_ROWS_PER_STEP
    assert seq_len % blk == 0 and seq_len // blk <= 8 and batch % per == 0
    rows = per * seq_len
    return pl.pallas_call(
        functools.partial(_moba_kernel, seq_len=seq_len),
        grid=(MOBA_HEADS, batch // per),
        in_specs=[
            pl.BlockSpec(memory_space=pltpu.SMEM),
            pl.BlockSpec((rows, hd), lambda h, b: (b, col0 + h)),
            pl.BlockSpec((rows, hd), lambda h, b: (b, col0 + MOBA_HEADS + h)),
            pl.BlockSpec((rows, hd), lambda h, b: (b, col0 + 2 * MOBA_HEADS + h)),
        ],
        out_specs=pl.BlockSpec((rows, hd), lambda h, b: (b, h)),
        out_shape=jax.ShapeDtypeStruct((m, MOBA_HEADS * hd), BF16),
        scratch_shapes=[
            pltpu.VMEM((blk, blk), F32),
            pltpu.VMEM((blk, blk), F32),
            pltpu.VMEM((per, seq_len, hd + LANES), BF16),
            pltpu.VMEM((per, seq_len, hd + LANES), BF16),
            pltpu.VMEM((per, seq_len, 2 * hd), BF16),
            pltpu.VMEM((per, LANES, hd), F32),
            pltpu.VMEM((2, blk, seq_len), F32),
        ],
        compiler_params=pltpu.CompilerParams(
            dimension_semantics=("arbitrary", "arbitrary"),
            vmem_limit_bytes=VMEM_LIMIT_BYTES),
        name="moba_attn",
    )(rel_bias, p, p, p)


def _mem_kernel(q_ref, k_ref, v_ref, o_ref, s_ref, *, seq_len, tq):
    heads = q_ref.shape[1] // HEAD_DIM
    cols = [slice(h * HEAD_DIM, (h + 1) * HEAD_DIM) for h in range(heads)]
    ones = jnp.ones((v_ref.shape[0], HEAD_DIM), BF16)
    v_ones = [jnp.concatenate([v_ref[:, c], ones], axis=1) for c in cols]
    work = [(h, i) for h in range(heads) for i in range(seq_len // tq)]

    def qk(pos):
        h, i = work[pos]
        s_ref[pos % 2] = _dot_nt(q_ref[i * tq:(i + 1) * tq, cols[h]], k_ref[:, cols[h]])

    qk(0)
    for pos, (h, i) in enumerate(work):
        if pos + 1 < len(work):
            qk(pos + 1)
        o_ref[i * tq:(i + 1) * tq, cols[h]] = _softmax_pv(s_ref[pos % 2], v_ones[h]).astype(BF16)


def _mem_attn(p, mkv, *, batch, seq_len, n_mem, col0):
    m = p.shape[0]
    width = MEM_HEADS * HEAD_DIM
    tq = min(seq_len, 512)
    assert col0 % MEM_HEADS == 0
    return pl.pallas_call(
        functools.partial(_mem_kernel, seq_len=seq_len, tq=tq),
        grid=(batch,),
        in_specs=[
            pl.BlockSpec((seq_len, width), lambda b: (b, col0 // MEM_HEADS)),
            pl.BlockSpec((n_mem, width), lambda b: (b, 0)),
            pl.BlockSpec((n_mem, width), lambda b: (b, 1)),
        ],
        out_specs=pl.BlockSpec((seq_len, width), lambda b: (b, 0)),
        out_shape=jax.ShapeDtypeStruct((m, width), BF16),
        scratch_shapes=[pltpu.VMEM((2, tq, n_mem), F32)],
        compiler_params=pltpu.CompilerParams(
            dimension_semantics=("arbitrary",), vmem_limit_bytes=VMEM_LIMIT_BYTES),
        name="mem_attn",
    )(p, mkv, mkv)


def _out_proj_kernel(x_ref, of_ref, ob_ref, om_ref, w_ref, o_ref):
    wf = of_ref.shape[1]
    wb = ob_ref.shape[1]
    acc = _dot(of_ref[...], w_ref[0:wf, :])
    acc += _dot(ob_ref[...], w_ref[wf:wf + wb, :])
    acc += _dot(om_ref[...], w_ref[wf + wb:, :])
    o_ref[...] = x_ref[...] + acc


def _out_proj(x2d, o_fox, o_moba, o_mem, w_out, *, tm):
    m, d = x2d.shape
    return pl.pallas_call(
        _out_proj_kernel,
        grid=(m // tm,),
        in_specs=[
            pl.BlockSpec((tm, d), lambda i: (i, 0)),
            pl.BlockSpec((tm, o_fox.shape[1]), lambda i: (i, 0)),
            pl.BlockSpec((tm, o_moba.shape[1]), lambda i: (i, 0)),
            pl.BlockSpec((tm, o_mem.shape[1]), lambda i: (i, 0)),
            pl.BlockSpec(w_out.shape, lambda i: (0, 0), pipeline_mode=pl.Buffered(1)),
        ],
        out_specs=pl.BlockSpec((tm, d), lambda i: (i, 0)),
        out_shape=jax.ShapeDtypeStruct((m, d), F32),
        compiler_params=pltpu.CompilerParams(
            dimension_semantics=("arbitrary",), vmem_limit_bytes=VMEM_LIMIT_BYTES),
        name="out_proj",
    )(x2d, o_fox, o_moba, o_mem, w_out)


def _tile(total, want):
    t = min(total, want)
    assert total % t == 0, (total, want)
    return t


def kernel(x, mem, ffn1_norm, ffn1_w1, ffn1_w3, ffn1_w2, mix_norm, mem_norm, w_in, b_forget,
           w_mem_kv, fox_q_gain, fox_k_gain, moba_q_gain, moba_k_gain, mem_q_gain, mem_k_gain,
           w_out, ffn2_norm, ffn2_w1, ffn2_w3, ffn2_w2, rel_bias):
    batch, seq_len, d = x.shape
    n_mem = mem.shape[1]
    depth = w_in.shape[0]
    fox_w = FOX_HEADS * HEAD_DIM
    moba_w = MOBA_HEADS * HEAD_DIM
    mem_w = MEM_HEADS * HEAD_DIM
    m = batch * seq_len
    tm = _tile(seq_len, 1024)
    ones = jnp.ones((HEAD_DIM,), F32)
    q_scale = HEAD_DIM ** -0.5 * LOG2E

    x2d = x.reshape(m, d)
    mem2d = mem.reshape(batch * n_mem, d)
    for l in range(depth):
        tf = _tile(ffn1_w1.shape[2], 512)
        n_i, n_f = m // tm, ffn1_w1.shape[2] // tf
        d_ff2 = ffn2_w1.shape[2]
        rows, ff_cols = d // n_i, d_ff2 // n_f
        n_steps = n_i * n_f
        jobs = (
            _cast_job(ffn2_w1[l], (rows, ff_cols), lambda i, f: (i, f)),
            _cast_job(ffn2_w3[l], (rows, ff_cols), lambda i, f: (i, f)),
            _cast_job(ffn2_w2[l], (ff_cols, rows), lambda i, f: (f, i)),
            _cast_job(w_out[l], (_row_slabs(w_out.shape[1], n_f, n_steps)[0], d),
                      _row_slabs(w_out.shape[1], n_f, n_steps)[1]),
            _cast_job(w_mem_kv[l], (_row_slabs(d, n_f, n_steps)[0], w_mem_kv.shape[2]),
                      _row_slabs(d, n_f, n_steps)[1]),
        ) + _w_in_jobs(jnp.swapaxes(w_in[l], 0, 1), n_f, n_steps)
        head, w1_1, w3_1, w2_1 = _ffn_head(x2d, ffn1_norm[l], ffn1_w1[l], ffn1_w3[l], ffn1_w2[l],
                                           tm=tm, tf=_tile(ffn1_w1.shape[2], 256))
        x2d, (w1_2, w3_2, w2_2, w_out_bf, w_mem_bf, w_main, w_forget) = _ffn(
            x2d, ffn1_norm[l], w1_1, w3_1, w2_1, tm=tm, tf=tf, jobs=jobs, head=head)

        b_pad = jnp.pad(b_forget[l].astype(F32), (0, LANES - FOX_HEADS)).reshape(1, LANES)
        head_gain = jnp.concatenate(
            [jnp.tile(fox_q_gain[l] * q_scale, FOX_HEADS), jnp.tile(fox_k_gain[l], FOX_HEADS),
             jnp.tile(ones, FOX_HEADS), jnp.tile(moba_q_gain[l] * q_scale, MOBA_HEADS),
             jnp.tile(moba_k_gain[l], MOBA_HEADS), jnp.tile(ones, MOBA_HEADS),
             jnp.tile(mem_q_gain[l] * q_scale, MEM_HEADS)]).astype(F32)
        head_flag = jnp.concatenate(
            [jnp.ones((2 * fox_w,), F32), jnp.zeros((fox_w,), F32), jnp.ones((2 * moba_w,), F32),
             jnp.zeros((moba_w,), F32), jnp.ones((mem_w,), F32)])
        proj, cum = _norm_proj(x2d, mix_norm[l], w_main, head_gain, head_flag, tm=tm, tn=2560,
                               forget=(w_forget, b_pad), seq_len=seq_len)

        kv_gain = jnp.concatenate([jnp.tile(mem_k_gain[l], MEM_HEADS), jnp.tile(ones, MEM_HEADS)])
        kv_flag = jnp.concatenate([jnp.ones((mem_w,), F32), jnp.zeros((mem_w,), F32)])
        mkv = _norm_proj(mem2d, mem_norm[l], w_mem_bf, kv_gain.astype(F32), kv_flag,
                         tm=_tile(batch * n_mem, 512), tn=512)

        o_fox = _fox_attn(proj, cum, batch=batch, seq_len=seq_len)
        o_moba = _moba_attn(proj, rel_bias.astype(F32), batch=batch, seq_len=seq_len,
                            col0=3 * FOX_HEADS)
        o_mem = _mem_attn(proj, mkv, batch=batch, seq_len=seq_len, n_mem=n_mem,
                          col0=3 * FOX_HEADS + 3 * MOBA_HEADS)
        x2d = _out_proj(x2d, o_fox, o_moba, o_mem, w_out_bf, tm=tm)

        x2d, _ = _ffn(x2d, ffn2_norm[l], w1_2, w3_2, w2_2, tm=tm, tf=_tile(d_ff2, 512))
    return x2d.reshape(batch, seq_len, d)
```

```python
import functools
import math

import jax
import jax.numpy as jnp
from jax import lax
from jax.experimental import pallas as pl
from jax.experimental.pallas import tpu as pltpu

HEAD_DIM = 128
FOX_HEADS = 8
MOBA_HEADS = 4
MEM_HEADS = 4
MOBA_BLOCK = 256
MOBA_TOPK = 3
REL_BUCKETS = 32
REL_MAX_DIST = 128
EPS = 1e-6
NEG = -1e30
LOG2E = math.log2(math.e)

LANES = 128
BF16_SUBLANES = 16
MXU_WIDTH = 256
ATTN_Q_BLOCK = 256
NORM_ROW_CHUNKS = 4
TOKEN_TILE = 1024
FF_TILE = 512
HEAD_FF_TILE = 256
PROJ_COL_TILE = 2560
MEM_PROJ_TILE = 1024
VMEM_LIMIT_BYTES = 60 * 1024 * 1024

F32 = jnp.float32
BF16 = jnp.bfloat16


def _rms_scale(x):
    return lax.rsqrt(jnp.mean(x * x, axis=-1, keepdims=True) + EPS)


def _dot(a, b):
    return jnp.dot(a, b, preferred_element_type=F32)


def _dot_nt(a, b):
    return lax.dot_general(a, b, (((1,), (1,)), ((), ())), preferred_element_type=F32)


def _split3(x):
    hi = x.astype(BF16)
    r1 = x - hi.astype(F32)
    mid = r1.astype(BF16)
    lo = (r1 - mid.astype(F32)).astype(BF16)
    return hi, mid, lo


def _softmax_pv(logits2, v_ones):
    mx = jnp.max(logits2, axis=-1, keepdims=True)
    p = jnp.exp2(logits2 - mx).astype(BF16)
    pv = _dot(p, v_ones)
    return pv[:, :HEAD_DIM] / pv[:, HEAD_DIM:]


class _SideJob:
    def __init__(self, inputs, outputs, body):
        self.inputs, self.outputs, self.body = inputs, outputs, body


def _norm_to(x_ref, g_ref, xn_ref, o_ref):
    x = x_ref[...]
    xn_ref[...] = (x * _rms_scale(x) * g_ref[...]).astype(BF16)
    o_ref[...] = x


def _swiglu_half(xn, w1, w3, w2):
    chunks = [slice(c, c + MXU_WIDTH) for c in range(0, w1.shape[1], MXU_WIDTH)]
    ups = [(_dot(xn, w1[:, c]), _dot(xn, w3[:, c])) for c in chunks]
    out = None
    for c, (h1, h3) in zip(chunks, ups):
        act = (0.5 * h1 * jax.nn.sigmoid(h1) * h3).astype(BF16)
        part = _dot(act, w2[c, :])
        out = part if out is None else out + part
    return out


def _ffn_head_kernel(x_ref, g_ref, w1_ref, w3_ref, w2_ref, o_ref, w1b_ref, w3b_ref, w2b_ref, xn_ref):
    pl.when(pl.program_id(0) == 0)(functools.partial(_norm_to, x_ref, g_ref, xn_ref, o_ref))
    w1b_ref[...] = w1_ref[...].astype(BF16)
    w3b_ref[...] = w3_ref[...].astype(BF16)
    w2b_ref[...] = w2_ref[...].astype(BF16)
    o_ref[...] += _swiglu_half(xn_ref[...], w1b_ref[...], w3b_ref[...], w2b_ref[...])


def _ffn_head(x2d, gain, w1, w3, w2, *, tm, tf):
    d = x2d.shape[1]
    d_ff = w1.shape[1]
    up = pl.BlockSpec((d, tf), lambda f: (0, f))
    down = pl.BlockSpec((tf, d), lambda f: (f, 0))
    row = pl.BlockSpec((tm, d), lambda f: (0, 0))
    return pl.pallas_call(
        _ffn_head_kernel,
        grid=(d_ff // tf,),
        in_specs=[row, pl.BlockSpec((1, d), lambda f: (0, 0)), up, up, down],
        out_specs=[row, up, up, down],
        out_shape=[jax.ShapeDtypeStruct((tm, d), F32), jax.ShapeDtypeStruct(w1.shape, BF16),
                   jax.ShapeDtypeStruct(w3.shape, BF16), jax.ShapeDtypeStruct(w2.shape, BF16)],
        scratch_shapes=[pltpu.VMEM((tm, d), BF16)],
        compiler_params=pltpu.CompilerParams(
            dimension_semantics=("arbitrary",), vmem_limit_bytes=VMEM_LIMIT_BYTES),
        name="ffn_head",
    )(x2d, gain.reshape(1, d), w1, w3, w2)


def _ffn_kernel(*refs, jobs, has_head):
    x_ref, g_ref, w1_ref, w3_ref, w2_ref = refs[:5]
    n_in = 5 + has_head + sum(len(j.inputs) for j in jobs)
    side_in = refs[5 + has_head:n_in]
    o_ref = refs[n_in]
    side_out = refs[n_in + 1:-1]
    xn_ref = refs[-1]
    i = pl.program_id(0)
    first = pl.program_id(1) == 0

    def first_step():
        chunk = x_ref.shape[0] // NORM_ROW_CHUNKS
        for rc in range(NORM_ROW_CHUNKS):
            rows = slice(rc * chunk, (rc + 1) * chunk)
            x = x_ref[rows, :]
            xn = (x * _rms_scale(x) * g_ref[...]).astype(BF16)
            xn_ref[rows, :] = xn
            o_ref[rows, :] = x + _swiglu_half(xn, w1_ref[...], w3_ref[...], w2_ref[...])

    def later_step():
        o_ref[...] += _swiglu_half(xn_ref[...], w1_ref[...], w3_ref[...], w2_ref[...])

    def tile():
        pl.when(first)(first_step)
        pl.when(jnp.logical_not(first))(later_step)

    if has_head:
        head_ref = refs[5]
        pl.when((i == 0) & first)(lambda: pltpu.sync_copy(head_ref, o_ref))
        pl.when(i > 0)(tile)
    else:
        tile()

    for job in jobs:
        ins, side_in = side_in[:len(job.inputs)], side_in[len(job.inputs):]
        outs, side_out = side_out[:len(job.outputs)], side_out[len(job.outputs):]
        job.body(ins, outs)


def _ffn(x2d, gain, w1, w3, w2, *, tm, tf, jobs=(), head=None):
    m, d = x2d.shape
    d_ff = w1.shape[1]
    has_head = head is not None
    col = (lambda i, f: jnp.where(i == 0, 0, f)) if has_head else (lambda i, f: f)
    side_in = [io for j in jobs for io in j.inputs]
    side_out = [io for j in jobs for io in j.outputs]
    res = pl.pallas_call(
        functools.partial(_ffn_kernel, jobs=jobs, has_head=has_head),
        grid=(m // tm, d_ff // tf),
        in_specs=[
            pl.BlockSpec((tm, d), lambda i, f: (i, 0)),
            pl.BlockSpec((1, d), lambda i, f: (0, 0)),
            pl.BlockSpec((d, tf), lambda i, f: (0, col(i, f))),
            pl.BlockSpec((d, tf), lambda i, f: (0, col(i, f))),
            pl.BlockSpec((tf, d), lambda i, f: (col(i, f), 0)),
        ] + [pl.BlockSpec(memory_space=pl.ANY)] * has_head + [spec for _, spec in side_in],
        out_specs=[pl.BlockSpec((tm, d), lambda i, f: (i, 0))] + [spec for _, spec in side_out],
        out_shape=[jax.ShapeDtypeStruct((m, d), F32)] + [struct for struct, _ in side_out],
        scratch_shapes=[pltpu.VMEM((tm, d), BF16)],
        compiler_params=pltpu.CompilerParams(
            dimension_semantics=("arbitrary", "arbitrary"),
            vmem_limit_bytes=VMEM_LIMIT_BYTES),
        name="ffn",
    )(x2d, gain.reshape(1, d), w1, w3, w2, *([head] if has_head else []),
      *[arr for arr, _ in side_in])
    return res[0], res[1:]


def _cast_job(w, block, index_map):
    def body(ins, outs):
        outs[0][...] = ins[0][...].astype(BF16)
    spec = pl.BlockSpec(block, index_map)
    return _SideJob([(w, spec)], [(jax.ShapeDtypeStruct(w.shape, BF16), spec)], body)


def _row_slabs(nrows, n_f, n_steps):
    rows = BF16_SUBLANES
    while nrows % rows or nrows // rows > n_steps:
        rows += BF16_SUBLANES
    last = nrows // rows - 1
    return rows, lambda i, f: (jnp.minimum(i * n_f + f, last), 0)


def _w_in_jobs(w_in_t, n_f, n_steps):
    in_w, d = w_in_t.shape
    f0 = 3 * FOX_HEADS * HEAD_DIM
    main_w = in_w - FOX_HEADS
    rows = BF16_SUBLANES
    while main_w % rows or f0 % rows or main_w // rows > n_steps:
        rows += BF16_SUBLANES
    last = main_w // rows - 1
    step = lambda i, f: jnp.minimum(i * n_f + f, last)
    per_row_block = rows // FOX_HEADS

    def main_body(ins, outs):
        t = jnp.minimum(pl.program_id(0) * n_f + pl.program_id(1), last)
        a = ins[0][...]
        shifted = jnp.concatenate([a[FOX_HEADS:], ins[1][...]], axis=0)
        outs[0][...] = jnp.where(t * rows >= f0, shifted, a).astype(BF16)

    def forget_body(ins, outs):
        pad = jnp.zeros((LANES - FOX_HEADS, d), F32)
        outs[0][...] = jnp.concatenate([ins[0][...], pad], axis=0).astype(BF16)

    main = _SideJob(
        [(w_in_t, pl.BlockSpec((rows, d), lambda i, f: (step(i, f), 0))),
         (w_in_t, pl.BlockSpec((FOX_HEADS, d), lambda i, f: ((step(i, f) + 1) * per_row_block, 0)))],
        [(jax.ShapeDtypeStruct((main_w, d), BF16),
          pl.BlockSpec((rows, d), lambda i, f: (step(i, f), 0)))],
        main_body)
    forget = _SideJob(
        [(w_in_t, pl.BlockSpec((FOX_HEADS, d), lambda i, f: (f0 // FOX_HEADS, 0)))],
        [(jax.ShapeDtypeStruct((LANES, d), BF16), pl.BlockSpec((LANES, d), lambda i, f: (0, 0)))],
        forget_body)
    return main, forget


def _project_heads(xn_ref, w_ref, hg_ref, hflag_ref, p_ref, w_transposed):
    xn = xn_ref[...]
    for sb in range(p_ref.shape[1] // MXU_WIDTH):
        if w_transposed:
            y = _dot_nt(xn, w_ref[sb * MXU_WIDTH:(sb + 1) * MXU_WIDTH, :])
        else:
            y = _dot(xn, w_ref[:, sb * MXU_WIDTH:(sb + 1) * MXU_WIDTH])
        for hh in range(MXU_WIDTH // HEAD_DIM):
            cols = slice(sb * MXU_WIDTH + hh * HEAD_DIM, sb * MXU_WIDTH + (hh + 1) * HEAD_DIM)
            yh = y[:, hh * HEAD_DIM:(hh + 1) * HEAD_DIM]
            normed = yh * _rms_scale(yh) * hg_ref[:, cols]
            p_ref[:, cols] = jnp.where(hflag_ref[:, cols] > 0.0, normed, yh).astype(BF16)


def _proj_kernel(x_ref, g_ref, w_ref, hg_ref, hflag_ref, p_ref, xn_ref):
    @pl.when(pl.program_id(1) == 0)
    def _():
        x = x_ref[...]
        xn_ref[...] = (x * _rms_scale(x) * g_ref[...]).astype(BF16)

    _project_heads(xn_ref, w_ref, hg_ref, hflag_ref, p_ref, w_transposed=False)


def _proj_forget_kernel(x_ref, g_ref, w_ref, hg_ref, hflag_ref, wf_ref, bf_ref,
                        p_ref, c_ref, xn_ref, carry_ref, *, tiles_per_seq, cum_block):
    i = pl.program_id(0)
    j = pl.program_id(1)

    @pl.when(j == 0)
    def _():
        x = x_ref[...]
        xn_ref[...] = (x * _rms_scale(x) * g_ref[...]).astype(BF16)

    def forget_gates():
        @pl.when(i % tiles_per_seq == 0)
        def _():
            carry_ref[...] = jnp.zeros_like(carry_ref)

        z = _dot_nt(xn_ref[...], wf_ref[...]) + bf_ref[...]
        logf = jnp.minimum(z, 0.0) - jnp.log1p(jnp.exp(-jnp.abs(z)))
        r = lax.broadcasted_iota(jnp.int32, (cum_block, cum_block), 0)
        c = lax.broadcasted_iota(jnp.int32, (cum_block, cum_block), 1)
        tril = (c <= r).astype(BF16)
        local = []
        for blk in range(logf.shape[0] // cum_block):
            hi, mid, lo = _split3(logf[blk * cum_block:(blk + 1) * cum_block, :])
            local.append(_dot(tril, hi) + _dot(tril, mid) + _dot(tril, lo))
        carry = carry_ref[...]
        for blk, loc in enumerate(local):
            cum = loc + carry
            c_ref[blk * cum_block:(blk + 1) * cum_block, :] = cum
            carry = cum[cum_block - 1:cum_block, :]
        carry_ref[...] = carry

    heads = functools.partial(_project_heads, xn_ref, w_ref, hg_ref, hflag_ref, p_ref,
                              w_transposed=True)

    @pl.when(j == 1)
    def _():
        forget_gates()
        heads()

    @pl.when(j != 1)
    def _():
        heads()


def _norm_proj(x2d, gain, w, head_gain, head_flag, *, tm, tn, forget=None, seq_len=None):
    m, d = x2d.shape
    n = w.shape[1] if forget is None else w.shape[0]
    w_spec = (pl.BlockSpec((d, tn), lambda i, j: (0, j)) if forget is None
              else pl.BlockSpec((tn, d), lambda i, j: (j, 0)))
    in_specs = [
        pl.BlockSpec((tm, d), lambda i, j: (i, 0)),
        pl.BlockSpec((1, d), lambda i, j: (0, 0)),
        w_spec,
        pl.BlockSpec((1, tn), lambda i, j: (0, j)),
        pl.BlockSpec((1, tn), lambda i, j: (0, j)),
    ]
    args = [x2d, gain.reshape(1, d), w, head_gain.reshape(1, n), head_flag.reshape(1, n)]
    p_spec = pl.BlockSpec((tm, tn), lambda i, j: (i, j))
    p_shape = jax.ShapeDtypeStruct((m, n), BF16)
    params = pltpu.CompilerParams(dimension_semantics=("arbitrary", "arbitrary"),
                                  vmem_limit_bytes=VMEM_LIMIT_BYTES)
    if forget is None:
        return pl.pallas_call(
            _proj_kernel, grid=(m // tm, n // tn), in_specs=in_specs, out_specs=p_spec,
            out_shape=p_shape, scratch_shapes=[pltpu.VMEM((tm, d), BF16)],
            compiler_params=params, name="mem_proj")(*args)
    wf, bf = forget
    assert n // tn >= 2
    in_specs += [pl.BlockSpec((LANES, d), lambda i, j: (0, 0)),
                 pl.BlockSpec((1, LANES), lambda i, j: (0, 0))]
    kern = functools.partial(_proj_forget_kernel, tiles_per_seq=seq_len // tm,
                             cum_block=min(tm, 256))
    return pl.pallas_call(
        kern, grid=(m // tm, n // tn), in_specs=in_specs,
        out_specs=[p_spec, pl.BlockSpec((tm, LANES), lambda i, j: (i, 0))],
        out_shape=[p_shape, jax.ShapeDtypeStruct((m, LANES), F32)],
        scratch_shapes=[pltpu.VMEM((tm, d), BF16), pltpu.VMEM((1, LANES), F32)],
        compiler_params=params, name="mix_proj")(*args, wf, bf)


FORGET_LANES = 6
FOX_HEADS_PER_STEP = 4
MOBA_ROWS_PER_STEP = 4


def _lane_set(lanes, offset):
    hit = lanes < 0
    for hh in range(FOX_HEADS):
        lo = FORGET_LANES * hh + offset
        hit = hit | ((lanes >= lo) & (lanes < lo + 3))
    return hit


def _fox_kernel(q_ref, k_ref, v_ref, c_ref, o_ref, qc_ref, kc_ref, kaug_ref, vaug_ref, s_ref,
                *, seq_len):
    step = pl.program_id(1)
    tq = ATTN_Q_BLOCK
    nq = seq_len // tq
    r = lax.broadcasted_iota(jnp.int32, (tq, tq), 0)
    c = lax.broadcasted_iota(jnp.int32, (tq, tq), 1)
    causal = c <= r

    @pl.when(step == 0)
    def _():
        hi, mid, lo = _split3(c_ref[...] * LOG2E)
        row = lax.broadcasted_iota(jnp.int32, (3 * LANES, 2 * LANES), 0)
        col = lax.broadcasted_iota(jnp.int32, (3 * LANES, 2 * LANES), 1)
        piece, head = row >> 7, row & (LANES - 1)
        slot = FORGET_LANES * head + piece
        route = jnp.where((head < FOX_HEADS) & (col == slot), 1.0,
                          jnp.where((head < FOX_HEADS) & (col == LANES + 3 + slot), -1.0, 0.0))
        routed = _dot(jnp.concatenate([hi, mid, lo], axis=1), route.astype(BF16))
        lane1 = lax.broadcasted_iota(jnp.int32, (1, LANES), 1)
        qc_ref[...] = (routed[:, :LANES] + jnp.where(_lane_set(lane1, 3), 1.0, 0.0)).astype(BF16)
        kc_ref[...] = (routed[:, LANES:] + jnp.where(_lane_set(lane1, 0), 1.0, 0.0)).astype(BF16)
        for hh in range(FOX_HEADS_PER_STEP):
            vaug_ref[hh, :, HEAD_DIM:] = jnp.ones((seq_len, HEAD_DIM), BF16)

    lanes = lax.broadcasted_iota(jnp.int32, (seq_len, LANES), 1)
    for hh in range(FOX_HEADS_PER_STEP):
        h = step * FOX_HEADS_PER_STEP + hh
        cols = slice(hh * HEAD_DIM, (hh + 1) * HEAD_DIM)
        mine = (lanes >= FORGET_LANES * h) & (lanes < FORGET_LANES * (h + 1))
        kaug_ref[hh, :, :HEAD_DIM] = k_ref[:, cols]
        kaug_ref[hh, :, HEAD_DIM:] = jnp.where(mine, kc_ref[...], jnp.zeros((), BF16))
        vaug_ref[hh, :, :HEAD_DIM] = v_ref[:, cols]

    work = [(hh, i) for hh in range(FOX_HEADS_PER_STEP) for i in reversed(range(nq))]

    def qk(pos):
        hh, i = work[pos]
        n = (i + 1) * tq
        rows = slice(i * tq, (i + 1) * tq)
        q_aug = jnp.concatenate([q_ref[rows, hh * HEAD_DIM:(hh + 1) * HEAD_DIM], qc_ref[rows, :]],
                                axis=1)
        s_ref[pos % 2, :, 0:n] = _dot_nt(q_aug, kaug_ref[hh, 0:n, :])

    qk(0)
    for pos, (hh, i) in enumerate(work):
        n = (i + 1) * tq
        if pos + 1 < len(work):
            qk(pos + 1)
        logits = s_ref[pos % 2, :, 0:n]
        own = jnp.where(causal, logits[:, i * tq:], NEG)
        if i > 0:
            logits = jnp.concatenate([logits[:, :i * tq], own], axis=1)
        else:
            logits = own
        o_ref[i * tq:(i + 1) * tq, hh * HEAD_DIM:(hh + 1) * HEAD_DIM] = _softmax_pv(
            logits, vaug_ref[hh, 0:n, :]).astype(BF16)


def _fox_attn(p, cum, *, batch, seq_len):
    m = p.shape[0]
    hd = HEAD_DIM
    per = FOX_HEADS_PER_STEP
    steps = FOX_HEADS // per
    return pl.pallas_call(
        functools.partial(_fox_kernel, seq_len=seq_len),
        grid=(batch, steps),
        in_specs=[
            pl.BlockSpec((seq_len, per * hd), lambda b, s: (b, s)),
            pl.BlockSpec((seq_len, per * hd), lambda b, s: (b, steps + s)),
            pl.BlockSpec((seq_len, per * hd), lambda b, s: (b, 2 * steps + s)),
            pl.BlockSpec((seq_len, LANES), lambda b, s: (b, 0)),
        ],
        out_specs=pl.BlockSpec((seq_len, per * hd), lambda b, s: (b, s)),
        out_shape=jax.ShapeDtypeStruct((m, FOX_HEADS * hd), BF16),
        scratch_shapes=[pltpu.VMEM((seq_len, LANES), BF16),
                        pltpu.VMEM((seq_len, LANES), BF16),
                        pltpu.VMEM((per, seq_len, hd + LANES), BF16),
                        pltpu.VMEM((per, seq_len, 2 * hd), BF16),
                        pltpu.VMEM((2, ATTN_Q_BLOCK, seq_len), F32)],
        compiler_params=pltpu.CompilerParams(
            dimension_semantics=("arbitrary", "arbitrary"),
            vmem_limit_bytes=VMEM_LIMIT_BYTES),
        name="fox_attn",
    )(p, p, p, cum)


def _t5_bias(dist, rel_ref, h):
    n = jnp.maximum(dist, 0)
    max_exact = REL_BUCKETS // 2
    nf = jnp.maximum(n, 1).astype(F32)
    large = max_exact + (jnp.log(nf / max_exact) / math.log(REL_MAX_DIST / max_exact)
                         * (REL_BUCKETS - max_exact)).astype(jnp.int32)
    large = jnp.minimum(large, REL_BUCKETS - 1)
    bucket = jnp.where(n < max_exact, n, large)
    bias = jnp.zeros(dist.shape, F32)
    for b in range(REL_BUCKETS):
        bias = jnp.where(bucket == b, rel_ref[b, h], bias)
    return bias


def _moba_kernel(rel_ref, q_ref, k_ref, v_ref, o_ref, town_ref, tprev_ref, kaug_ref, qaug_ref,
                 vaug_ref, kmean_ref, s_ref, *, seq_len):
    h = pl.program_id(0)
    blk = MOBA_BLOCK
    nb = seq_len // blk
    per = MOBA_ROWS_PER_STEP
    r = lax.broadcasted_iota(jnp.int32, (blk, blk), 0)
    c = lax.broadcasted_iota(jnp.int32, (blk, blk), 1)
    lane_grp = lax.broadcasted_iota(jnp.int32, (1, LANES), 1) >> 3

    @pl.when(pl.program_id(1) == 0)
    def _():
        town_ref[...] = jnp.where(c <= r, _t5_bias(r - c, rel_ref, h) * LOG2E, NEG)
        tprev_ref[...] = _t5_bias(r - c + blk, rel_ref, h) * LOG2E
        s = lax.broadcasted_iota(jnp.int32, (seq_len, LANES), 0)
        ln = lax.broadcasted_iota(jnp.int32, (seq_len, LANES), 1)
        first_key = (ln & 7) * blk
        onehot = (ln < 32) & (s >= first_key) & (s < first_key + blk)
        for bb in range(per):
            kaug_ref[bb, :, HEAD_DIM:] = jnp.where(onehot, 1.0, 0.0).astype(BF16)
            vaug_ref[bb, :, HEAD_DIM:] = jnp.ones((seq_len, HEAD_DIM), BF16)

    far = jnp.full((1, LANES), rel_ref[REL_BUCKETS - 1, h] * LOG2E, F32)
    far_hi, far_mid, far_lo = _split3(far)
    far_parts = jnp.where(lane_grp == 1, far_hi.astype(F32),
                          jnp.where(lane_grp == 2, far_mid.astype(F32),
                                    jnp.where(lane_grp == 3, far_lo.astype(F32), 0.0)))

    def far_bias(row0, nrows):
        rows = row0 + lax.broadcasted_iota(jnp.int32, (nrows, LANES), 0)
        lanes = lax.broadcasted_iota(jnp.int32, (nrows, LANES), 1)
        own = rows >> (blk.bit_length() - 1)
        return jnp.where((lanes & 7) <= own - 2, far_parts, 0.0), lanes, own

    late = min(seq_len, (MOBA_TOPK + 1) * blk)
    for bb in range(per):
        base = bb * seq_len
        kaug_ref[bb, :, :HEAD_DIM] = k_ref[base:base + seq_len, :]
        vaug_ref[bb, :, :HEAD_DIM] = v_ref[base:base + seq_len, :]
        qaug_ref[bb, :, :HEAD_DIM] = q_ref[base:base + seq_len, :]
        qaug_ref[bb, :late, HEAD_DIM:] = far_bias(0, late)[0].astype(BF16)
        if seq_len > late:
            kmean_ref[bb] = jnp.zeros(kmean_ref.shape[1:], F32)
            for j in range(nb):
                kmean_ref[bb, j:j + 1, :] = jnp.mean(
                    k_ref[base + j * blk:base + (j + 1) * blk, :].astype(F32), axis=0, keepdims=True)
            km_hi, km_mid, km_lo = _split3(kmean_ref[bb])
            ql = q_ref[base + late:base + seq_len, :]
            gate = _dot_nt(ql, km_hi) + _dot_nt(ql, km_mid) + _dot_nt(ql, km_lo)
            aug, lanes_l, own_l = far_bias(late, seq_len - late)
            valid = lanes_l < own_l
            g = jnp.where(valid, gate, NEG)
            sel = jnp.zeros(g.shape, jnp.bool_)
            lane_f = lanes_l.astype(F32)
            for _ in range(MOBA_TOPK):
                best = jnp.max(g, axis=-1, keepdims=True)
                first = jnp.min(jnp.where(g == best, lane_f, float(LANES)), axis=-1, keepdims=True)
                pick = lane_f == first
                sel = sel | pick
                g = jnp.where(pick, -jnp.inf, g)
            qaug_ref[bb, late:, HEAD_DIM:] = jnp.where(valid & jnp.logical_not(sel), NEG,
                                                       aug).astype(BF16)

    n_early = min(nb, MOBA_TOPK + 1)
    order = list(reversed(range(n_early))) + list(reversed(range(n_early, nb)))
    work = [(bb, i) for bb in range(per) for i in order]

    def qk(pos):
        bb, i = work[pos]
        n = (i + 1) * blk
        s_ref[pos % 2, :, 0:n] = _dot_nt(qaug_ref[bb, i * blk:(i + 1) * blk, :], kaug_ref[bb, 0:n, :])

    qk(0)
    for pos, (bb, i) in enumerate(work):
        n = (i + 1) * blk
        if pos + 1 < len(work):
            qk(pos + 1)
        s = s_ref[pos % 2, :, 0:n]
        pieces = [s[:, i * blk:] + town_ref[...]]
        if i >= 1:
            pieces.insert(0, s[:, (i - 1) * blk:i * blk] + tprev_ref[...])
        if i >= 2:
            pieces.insert(0, s[:, :(i - 1) * blk])
        logits = jnp.concatenate(pieces, axis=1) if len(pieces) > 1 else pieces[0]
        rows = slice(bb * seq_len + i * blk, bb * seq_len + (i + 1) * blk)
        o_ref[rows, :] = _softmax_pv(logits, vaug_ref[bb, 0:n, :]).astype(BF16)


def _moba_attn(p, rel_bias, *, batch, seq_len, col0):
    m = p.shape[0]
    hd = HEAD_DIM
    blk = MOBA_BLOCK
    per = MOBA_ROWS_PER_STEP
    assert seq_len % blk == 0 and seq_len // blk <= 8 and batch % per == 0
    rows = per * seq_len
    return pl.pallas_call(
        functools.partial(_moba_kernel, seq_len=seq_len),
        grid=(MOBA_HEADS, batch // per),
        in_specs=[
            pl.BlockSpec(memory_space=pltpu.SMEM),
            pl.BlockSpec((rows, hd), lambda h, b: (b, col0 + h)),
            pl.BlockSpec((rows, hd), lambda h, b: (b, col0 + MOBA_HEADS + h)),
            pl.BlockSpec((rows, hd), lambda h, b: (b, col0 + 2 * MOBA_HEADS + h)),
        ],
        out_specs=pl.BlockSpec((rows, hd), lambda h, b: (b, h)),
        out_shape=jax.ShapeDtypeStruct((m, MOBA_HEADS * hd), BF16),
        scratch_shapes=[
            pltpu.VMEM((blk, blk), F32),
            pltpu.VMEM((blk, blk), F32),
            pltpu.VMEM((per, seq_len, hd + LANES), BF16),
            pltpu.VMEM((per, seq_len, hd + LANES), BF16),
            pltpu.VMEM((per, seq_len, 2 * hd), BF16),
            pltpu.VMEM((per, LANES, hd), F32),
            pltpu.VMEM((2, blk, seq_len), F32),
        ],
        compiler_params=pltpu.CompilerParams(
            dimension_semantics=("arbitrary", "arbitrary"),
            vmem_limit_bytes=VMEM_LIMIT_BYTES),
        name="moba_attn",
    )(rel_bias, p, p, p)


def _mem_kernel(q_ref, k_ref, v_ref, o_ref, s_ref, *, seq_len, tq):
    heads = q_ref.shape[1] // HEAD_DIM
    cols = [slice(h * HEAD_DIM, (h + 1) * HEAD_DIM) for h in range(heads)]
    ones = jnp.ones((v_ref.shape[0], HEAD_DIM), BF16)
    v_ones = [jnp.concatenate([v_ref[:, c], ones], axis=1) for c in cols]
    work = [(h, i) for h in range(heads) for i in range(seq_len // tq)]

    def qk(pos):
        h, i = work[pos]
        s_ref[pos % 2] = _dot_nt(q_ref[i * tq:(i + 1) * tq, cols[h]], k_ref[:, cols[h]])

    qk(0)
    for pos, (h, i) in enumerate(work):
        if pos + 1 < len(work):
            qk(pos + 1)
        o_ref[i * tq:(i + 1) * tq, cols[h]] = _softmax_pv(s_ref[pos % 2], v_ones[h]).astype(BF16)


def _mem_attn(p, mkv, *, batch, seq_len, n_mem, col0):
    m = p.shape[0]
    width = MEM_HEADS * HEAD_DIM
    tq = min(seq_len, 512)
    assert col0 % MEM_HEADS == 0
    return pl.pallas_call(
        functools.partial(_mem_kernel, seq_len=seq_len, tq=tq),
        grid=(batch,),
        in_specs=[
            pl.BlockSpec((seq_len, width), lambda b: (b, col0 // MEM_HEADS)),
            pl.BlockSpec((n_mem, width), lambda b: (b, 0)),
            pl.BlockSpec((n_mem, width), lambda b: (b, 1)),
        ],
        out_specs=pl.BlockSpec((seq_len, width), lambda b: (b, 0)),
        out_shape=jax.ShapeDtypeStruct((m, width), BF16),
        scratch_shapes=[pltpu.VMEM((2, tq, n_mem), F32)],
        compiler_params=pltpu.CompilerParams(
            dimension_semantics=("arbitrary",), vmem_limit_bytes=VMEM_LIMIT_BYTES),
        name="mem_attn",
    )(p, mkv, mkv)


def _out_proj_kernel(x_ref, of_ref, ob_ref, om_ref, w_ref, o_ref):
    wf = of_ref.shape[1]
    wb = ob_ref.shape[1]
    acc = _dot(of_ref[...], w_ref[0:wf, :])
    acc += _dot(ob_ref[...], w_ref[wf:wf + wb, :])
    acc += _dot(om_ref[...], w_ref[wf + wb:, :])
    o_ref[...] = x_ref[...] + acc


def _out_proj(x2d, o_fox, o_moba, o_mem, w_out, *, tm):
    m, d = x2d.shape
    return pl.pallas_call(
        _out_proj_kernel,
        grid=(m // tm,),
        in_specs=[
            pl.BlockSpec((tm, d), lambda i: (i, 0)),
            pl.BlockSpec((tm, o_fox.shape[1]), lambda i: (i, 0)),
            pl.BlockSpec((tm, o_moba.shape[1]), lambda i: (i, 0)),
            pl.BlockSpec((tm, o_mem.shape[1]), lambda i: (i, 0)),
            pl.BlockSpec(w_out.shape, lambda i: (0, 0), pipeline_mode=pl.Buffered(1)),
        ],
        out_specs=pl.BlockSpec((tm, d), lambda i: (i, 0)),
        out_shape=jax.ShapeDtypeStruct((m, d), F32),
        compiler_params=pltpu.CompilerParams(
            dimension_semantics=("arbitrary",), vmem_limit_bytes=VMEM_LIMIT_BYTES),
        name="out_proj",
    )(x2d, o_fox, o_moba, o_mem, w_out)


def _tile(total, want):
    t = min(total, want)
    assert total % t == 0, (total, want)
    return t


def kernel(x, mem, ffn1_norm, ffn1_w1, ffn1_w3, ffn1_w2, mix_norm, mem_norm, w_in, b_forget,
           w_mem_kv, fox_q_gain, fox_k_gain, moba_q_gain, moba_k_gain, mem_q_gain, mem_k_gain,
           w_out, ffn2_norm, ffn2_w1, ffn2_w3, ffn2_w2, rel_bias):
    batch, seq_len, d = x.shape
    n_mem = mem.shape[1]
    depth = w_in.shape[0]
    fox_w = FOX_HEADS * HEAD_DIM
    moba_w = MOBA_HEADS * HEAD_DIM
    mem_w = MEM_HEADS * HEAD_DIM
    m = batch * seq_len
    tm = _tile(seq_len, TOKEN_TILE)
    ones = jnp.ones((HEAD_DIM,), F32)
    q_scale = HEAD_DIM ** -0.5 * LOG2E

    x2d = x.reshape(m, d)
    mem2d = mem.reshape(batch * n_mem, d)
    for l in range(depth):
        tf = _tile(ffn1_w1.shape[2], FF_TILE)
        n_i, n_f = m // tm, ffn1_w1.shape[2] // tf
        d_ff2 = ffn2_w1.shape[2]
        rows, ff_cols = d // n_i, d_ff2 // n_f
        n_steps = n_i * n_f
        jobs = (
            _cast_job(ffn2_w1[l], (rows, ff_cols), lambda i, f: (i, f)),
            _cast_job(ffn2_w3[l], (rows, ff_cols), lambda i, f: (i, f)),
            _cast_job(ffn2_w2[l], (ff_cols, rows), lambda i, f: (f, i)),
            _cast_job(w_out[l], (_row_slabs(w_out.shape[1], n_f, n_steps)[0], d),
                      _row_slabs(w_out.shape[1], n_f, n_steps)[1]),
            _cast_job(w_mem_kv[l], (_row_slabs(d, n_f, n_steps)[0], w_mem_kv.shape[2]),
                      _row_slabs(d, n_f, n_steps)[1]),
        ) + _w_in_jobs(jnp.swapaxes(w_in[l], 0, 1), n_f, n_steps)
        head, w1_1, w3_1, w2_1 = _ffn_head(x2d, ffn1_norm[l], ffn1_w1[l], ffn1_w3[l], ffn1_w2[l],
                                           tm=tm, tf=_tile(ffn1_w1.shape[2], HEAD_FF_TILE))
        x2d, (w1_2, w3_2, w2_2, w_out_bf, w_mem_bf, w_main, w_forget) = _ffn(
            x2d, ffn1_norm[l], w1_1, w3_1, w2_1, tm=tm, tf=tf, jobs=jobs, head=head)

        b_pad = jnp.pad(b_forget[l].astype(F32), (0, LANES - FOX_HEADS)).reshape(1, LANES)
        head_gain = jnp.concatenate(
            [jnp.tile(fox_q_gain[l] * q_scale, FOX_HEADS), jnp.tile(fox_k_gain[l], FOX_HEADS),
             jnp.tile(ones, FOX_HEADS), jnp.tile(moba_q_gain[l] * q_scale, MOBA_HEADS),
             jnp.tile(moba_k_gain[l], MOBA_HEADS), jnp.tile(ones, MOBA_HEADS),
             jnp.tile(mem_q_gain[l] * q_scale, MEM_HEADS)]).astype(F32)
        head_flag = jnp.concatenate(
            [jnp.ones((2 * fox_w,), F32), jnp.zeros((fox_w,), F32), jnp.ones((2 * moba_w,), F32),
             jnp.zeros((moba_w,), F32), jnp.ones((mem_w,), F32)])
        proj, cum = _norm_proj(x2d, mix_norm[l], w_main, head_gain, head_flag, tm=tm,
                               tn=_tile(w_main.shape[0], PROJ_COL_TILE), forget=(w_forget, b_pad),
                               seq_len=seq_len)

        kv_gain = jnp.concatenate([jnp.tile(mem_k_gain[l], MEM_HEADS), jnp.tile(ones, MEM_HEADS)])
        kv_flag = jnp.concatenate([jnp.ones((mem_w,), F32), jnp.zeros((mem_w,), F32)])
        mkv = _norm_proj(mem2d, mem_norm[l], w_mem_bf, kv_gain.astype(F32), kv_flag,
                         tm=_tile(batch * n_mem, MEM_PROJ_TILE), tn=_tile(2 * mem_w, MEM_PROJ_TILE))

        o_fox = _fox_attn(proj, cum, batch=batch, seq_len=seq_len)
        o_moba = _moba_attn(proj, rel_bias.astype(F32), batch=batch, seq_len=seq_len,
                            col0=3 * FOX_HEADS)
        o_mem = _mem_attn(proj, mkv, batch=batch, seq_len=seq_len, n_mem=n_mem,
                          col0=3 * FOX_HEADS + 3 * MOBA_HEADS)
        x2d = _out_proj(x2d, o_fox, o_moba, o_mem, w_out_bf, tm=tm)

        x2d, _ = _ffn(x2d, ffn2_norm[l], w1_2, w3_2, w2_2, tm=tm, tf=_tile(d_ff2, FF_TILE))
    return x2d.reshape(batch, seq_len, d)
```

```python
import functools
import math

import jax
import jax.numpy as jnp
from jax import lax
from jax.experimental import pallas as pl
from jax.experimental.pallas import tpu as pltpu

HEAD_DIM = 128
FOX_HEADS = 8
MOBA_HEADS = 4
MEM_HEADS = 4
MOBA_BLOCK = 256
MOBA_TOPK = 3
REL_BUCKETS = 32
REL_MAX_DIST = 128
EPS = 1e-6
NEG = -1e30
LOG2E = math.log2(math.e)

LANES = 128
BF16_SUBLANES = 16
MXU_WIDTH = 256
ATTN_Q_BLOCK = 256
NORM_ROW_CHUNKS = 4
TOKEN_TILE = 1024
FF_TILE = 512
HEAD_FF_TILE = 256
PROJ_COL_TILE = 2560
MEM_PROJ_TILE = 1024
VMEM_LIMIT_BYTES = 60 * 1024 * 1024

F32 = jnp.float32
BF16 = jnp.bfloat16


def _rms_scale(x):
    return lax.rsqrt(jnp.mean(x * x, axis=-1, keepdims=True) + EPS)


def _dot(a, b):
    return jnp.dot(a, b, preferred_element_type=F32)


def _dot_nt(a, b):
    return lax.dot_general(a, b, (((1,), (1,)), ((), ())), preferred_element_type=F32)


def _split3(x):
    hi = x.astype(BF16)
    r1 = x - hi.astype(F32)
    mid = r1.astype(BF16)
    lo = (r1 - mid.astype(F32)).astype(BF16)
    return hi, mid, lo


def _softmax_pv(logits2, v_ones):
    mx = jnp.max(logits2, axis=-1, keepdims=True)
    p = jnp.exp2(logits2 - mx).astype(BF16)
    pv = _dot(p, v_ones)
    return pv[:, :HEAD_DIM] / pv[:, HEAD_DIM:]


class _SideJob:
    def __init__(self, inputs, outputs, body):
        self.inputs, self.outputs, self.body = inputs, outputs, body


def _norm_to(x_ref, g_ref, xn_ref, o_ref):
    x = x_ref[...]
    xn_ref[...] = (x * _rms_scale(x) * g_ref[...]).astype(BF16)
    o_ref[...] = x


def _swiglu_half(xn, w1, w3, w2):
    chunks = [slice(c, c + MXU_WIDTH) for c in range(0, w1.shape[1], MXU_WIDTH)]
    ups = [(_dot(xn, w1[:, c]), _dot(xn, w3[:, c])) for c in chunks]
    out = None
    for c, (h1, h3) in zip(chunks, ups):
        act = (0.5 * h1 * jax.nn.sigmoid(h1) * h3).astype(BF16)
        part = _dot(act, w2[c, :])
        out = part if out is None else out + part
    return out


def _ffn_head_kernel(x_ref, g_ref, w1_ref, w3_ref, w2_ref, o_ref, w1b_ref, w3b_ref, w2b_ref, xn_ref):
    pl.when(pl.program_id(0) == 0)(functools.partial(_norm_to, x_ref, g_ref, xn_ref, o_ref))
    w1b_ref[...] = w1_ref[...].astype(BF16)
    w3b_ref[...] = w3_ref[...].astype(BF16)
    w2b_ref[...] = w2_ref[...].astype(BF16)
    o_ref[...] += _swiglu_half(xn_ref[...], w1b_ref[...], w3b_ref[...], w2b_ref[...])


def _ffn_head(x2d, gain, w1, w3, w2, *, tm, tf):
    d = x2d.shape[1]
    d_ff = w1.shape[1]
    up = pl.BlockSpec((d, tf), lambda f: (0, f))
    down = pl.BlockSpec((tf, d), lambda f: (f, 0))
    row = pl.BlockSpec((tm, d), lambda f: (0, 0))
    return pl.pallas_call(
        _ffn_head_kernel,
        grid=(d_ff // tf,),
        in_specs=[row, pl.BlockSpec((1, d), lambda f: (0, 0)), up, up, down],
        out_specs=[row, up, up, down],
        out_shape=[jax.ShapeDtypeStruct((tm, d), F32), jax.ShapeDtypeStruct(w1.shape, BF16),
                   jax.ShapeDtypeStruct(w3.shape, BF16), jax.ShapeDtypeStruct(w2.shape, BF16)],
        scratch_shapes=[pltpu.VMEM((tm, d), BF16)],
        compiler_params=pltpu.CompilerParams(
            dimension_semantics=("arbitrary",), vmem_limit_bytes=VMEM_LIMIT_BYTES),
        name="ffn_head",
    )(x2d, gain.reshape(1, d), w1, w3, w2)


def _ffn_kernel(*refs, jobs, has_head):
    x_ref, g_ref, w1_ref, w3_ref, w2_ref = refs[:5]
    n_in = 5 + has_head + sum(len(j.inputs) for j in jobs)
    side_in = refs[5 + has_head:n_in]
    o_ref = refs[n_in]
    side_out = refs[n_in + 1:-1]
    xn_ref = refs[-1]
    i = pl.program_id(0)
    first = pl.program_id(1) == 0

    def first_step():
        chunk = x_ref.shape[0] // NORM_ROW_CHUNKS
        for rc in range(NORM_ROW_CHUNKS):
            rows = slice(rc * chunk, (rc + 1) * chunk)
            x = x_ref[rows, :]
            xn = (x * _rms_scale(x) * g_ref[...]).astype(BF16)
            xn_ref[rows, :] = xn
            o_ref[rows, :] = x + _swiglu_half(xn, w1_ref[...], w3_ref[...], w2_ref[...])

    def later_step():
        o_ref[...] += _swiglu_half(xn_ref[...], w1_ref[...], w3_ref[...], w2_ref[...])

    def tile():
        pl.when(first)(first_step)
        pl.when(jnp.logical_not(first))(later_step)

    if has_head:
        head_ref = refs[5]
        pl.when((i == 0) & first)(lambda: pltpu.sync_copy(head_ref, o_ref))
        pl.when(i > 0)(tile)
    else:
        tile()

    for job in jobs:
        ins, side_in = side_in[:len(job.inputs)], side_in[len(job.inputs):]
        outs, side_out = side_out[:len(job.outputs)], side_out[len(job.outputs):]
        job.body(ins, outs)


def _ffn(x2d, gain, w1, w3, w2, *, tm, tf, jobs=(), head=None):
    m, d = x2d.shape
    d_ff = w1.shape[1]
    has_head = head is not None
    col = (lambda i, f: jnp.where(i == 0, 0, f)) if has_head else (lambda i, f: f)
    side_in = [io for j in jobs for io in j.inputs]
    side_out = [io for j in jobs for io in j.outputs]
    res = pl.pallas_call(
        functools.partial(_ffn_kernel, jobs=jobs, has_head=has_head),
        grid=(m // tm, d_ff // tf),
        in_specs=[
            pl.BlockSpec((tm, d), lambda i, f: (i, 0)),
            pl.BlockSpec((1, d), lambda i, f: (0, 0)),
            pl.BlockSpec((d, tf), lambda i, f: (0, col(i, f))),
            pl.BlockSpec((d, tf), lambda i, f: (0, col(i, f))),
            pl.BlockSpec((tf, d), lambda i, f: (col(i, f), 0)),
        ] + [pl.BlockSpec(memory_space=pl.ANY)] * has_head + [spec for _, spec in side_in],
        out_specs=[pl.BlockSpec((tm, d), lambda i, f: (i, 0))] + [spec for _, spec in side_out],
        out_shape=[jax.ShapeDtypeStruct((m, d), F32)] + [struct for struct, _ in side_out],
        scratch_shapes=[pltpu.VMEM((tm, d), BF16)],
        compiler_params=pltpu.CompilerParams(
            dimension_semantics=("arbitrary", "arbitrary"),
            vmem_limit_bytes=VMEM_LIMIT_BYTES),
        name="ffn",
    )(x2d, gain.reshape(1, d), w1, w3, w2, *([head] if has_head else []),
      *[arr for arr, _ in side_in])
    return res[0], res[1:]


def _cast_job(w, block, index_map):
    def body(ins, outs):
        outs[0][...] = ins[0][...].astype(BF16)
    spec = pl.BlockSpec(block, index_map)
    return _SideJob([(w, spec)], [(jax.ShapeDtypeStruct(w.shape, BF16), spec)], body)


def _row_slabs(nrows, n_f, n_steps):
    rows = BF16_SUBLANES
    while nrows % rows or nrows // rows > n_steps:
        rows += BF16_SUBLANES
    last = nrows // rows - 1
    return rows, lambda i, f: (jnp.minimum(i * n_f + f, last), 0)


def _w_in_jobs(w_in_t, n_f, n_steps):
    in_w, d = w_in_t.shape
    f0 = 3 * FOX_HEADS * HEAD_DIM
    main_w = in_w - FOX_HEADS
    rows = BF16_SUBLANES
    while main_w % rows or f0 % rows or main_w // rows > n_steps:
        rows += BF16_SUBLANES
    last = main_w // rows - 1
    step = lambda i, f: jnp.minimum(i * n_f + f, last)
    per_row_block = rows // FOX_HEADS

    def main_body(ins, outs):
        t = jnp.minimum(pl.program_id(0) * n_f + pl.program_id(1), last)
        a = ins[0][...]
        shifted = jnp.concatenate([a[FOX_HEADS:], ins[1][...]], axis=0)
        outs[0][...] = jnp.where(t * rows >= f0, shifted, a).astype(BF16)

    def forget_body(ins, outs):
        pad = jnp.zeros((LANES - FOX_HEADS, d), F32)
        outs[0][...] = jnp.concatenate([ins[0][...], pad], axis=0).astype(BF16)

    main = _SideJob(
        [(w_in_t, pl.BlockSpec((rows, d), lambda i, f: (step(i, f), 0))),
         (w_in_t, pl.BlockSpec((FOX_HEADS, d), lambda i, f: ((step(i, f) + 1) * per_row_block, 0)))],
        [(jax.ShapeDtypeStruct((main_w, d), BF16),
          pl.BlockSpec((rows, d), lambda i, f: (step(i, f), 0)))],
        main_body)
    forget = _SideJob(
        [(w_in_t, pl.BlockSpec((FOX_HEADS, d), lambda i, f: (f0 // FOX_HEADS, 0)))],
        [(jax.ShapeDtypeStruct((LANES, d), BF16), pl.BlockSpec((LANES, d), lambda i, f: (0, 0)))],
        forget_body)
    return main, forget


def _project_heads(xn_ref, w_ref, hg_ref, hflag_ref, p_ref, w_transposed, rows=slice(None)):
    xn = xn_ref[rows, :]
    for sb in range(p_ref.shape[1] // MXU_WIDTH):
        if w_transposed:
            y = _dot_nt(xn, w_ref[sb * MXU_WIDTH:(sb + 1) * MXU_WIDTH, :])
        else:
            y = _dot(xn, w_ref[:, sb * MXU_WIDTH:(sb + 1) * MXU_WIDTH])
        for hh in range(MXU_WIDTH // HEAD_DIM):
            cols = slice(sb * MXU_WIDTH + hh * HEAD_DIM, sb * MXU_WIDTH + (hh + 1) * HEAD_DIM)
            yh = y[:, hh * HEAD_DIM:(hh + 1) * HEAD_DIM]
            normed = yh * _rms_scale(yh) * hg_ref[:, cols]
            p_ref[rows, cols] = jnp.where(hflag_ref[:, cols] > 0.0, normed, yh).astype(BF16)


def _proj_kernel(x_ref, g_ref, w_ref, hg_ref, hflag_ref, p_ref, xn_ref):
    @pl.when(pl.program_id(1) == 0)
    def _():
        x = x_ref[...]
        xn_ref[...] = (x * _rms_scale(x) * g_ref[...]).astype(BF16)

    _project_heads(xn_ref, w_ref, hg_ref, hflag_ref, p_ref, w_transposed=False)


def _proj_forget_kernel(x_ref, g_ref, w_ref, hg_ref, hflag_ref, wf_ref, bf_ref,
                        p_ref, c_ref, xn_ref, carry_ref, *, tiles_per_seq, cum_block):
    i = pl.program_id(0)
    j = pl.program_id(1)

    heads = functools.partial(_project_heads, xn_ref, w_ref, hg_ref, hflag_ref, p_ref,
                              w_transposed=True)

    @pl.when(j == 0)
    def _():
        chunk = x_ref.shape[0] // NORM_ROW_CHUNKS
        for rc in range(NORM_ROW_CHUNKS):
            rows = slice(rc * chunk, (rc + 1) * chunk)
            x = x_ref[rows, :]
            xn_ref[rows, :] = (x * _rms_scale(x) * g_ref[...]).astype(BF16)
            heads(rows=rows)

    def forget_gates():
        @pl.when(i % tiles_per_seq == 0)
        def _():
            carry_ref[...] = jnp.zeros_like(carry_ref)

        z = _dot_nt(xn_ref[...], wf_ref[...]) + bf_ref[...]
        logf = jnp.minimum(z, 0.0) - jnp.log1p(jnp.exp(-jnp.abs(z)))
        r = lax.broadcasted_iota(jnp.int32, (cum_block, cum_block), 0)
        c = lax.broadcasted_iota(jnp.int32, (cum_block, cum_block), 1)
        tril = (c <= r).astype(BF16)
        local = []
        for blk in range(logf.shape[0] // cum_block):
            hi, mid, lo = _split3(logf[blk * cum_block:(blk + 1) * cum_block, :])
            local.append(_dot(tril, hi) + _dot(tril, mid) + _dot(tril, lo))
        carry = carry_ref[...]
        for blk, loc in enumerate(local):
            cum = loc + carry
            c_ref[blk * cum_block:(blk + 1) * cum_block, :] = cum
            carry = cum[cum_block - 1:cum_block, :]
        carry_ref[...] = carry

    @pl.when(j == 1)
    def _():
        forget_gates()
        heads()

    @pl.when(j > 1)
    def _():
        heads()


def _norm_proj(x2d, gain, w, head_gain, head_flag, *, tm, tn, forget=None, seq_len=None):
    m, d = x2d.shape
    n = w.shape[1] if forget is None else w.shape[0]
    w_spec = (pl.BlockSpec((d, tn), lambda i, j: (0, j)) if forget is None
              else pl.BlockSpec((tn, d), lambda i, j: (j, 0)))
    in_specs = [
        pl.BlockSpec((tm, d), lambda i, j: (i, 0)),
        pl.BlockSpec((1, d), lambda i, j: (0, 0)),
        w_spec,
        pl.BlockSpec((1, tn), lambda i, j: (0, j)),
        pl.BlockSpec((1, tn), lambda i, j: (0, j)),
    ]
    args = [x2d, gain.reshape(1, d), w, head_gain.reshape(1, n), head_flag.reshape(1, n)]
    p_spec = pl.BlockSpec((tm, tn), lambda i, j: (i, j))
    p_shape = jax.ShapeDtypeStruct((m, n), BF16)
    params = pltpu.CompilerParams(dimension_semantics=("arbitrary", "arbitrary"),
                                  vmem_limit_bytes=VMEM_LIMIT_BYTES)
    if forget is None:
        return pl.pallas_call(
            _proj_kernel, grid=(m // tm, n // tn), in_specs=in_specs, out_specs=p_spec,
            out_shape=p_shape, scratch_shapes=[pltpu.VMEM((tm, d), BF16)],
            compiler_params=params, name="mem_proj")(*args)
    wf, bf = forget
    assert n // tn >= 2
    in_specs += [pl.BlockSpec((LANES, d), lambda i, j: (0, 0)),
                 pl.BlockSpec((1, LANES), lambda i, j: (0, 0))]
    kern = functools.partial(_proj_forget_kernel, tiles_per_seq=seq_len // tm,
                             cum_block=min(tm, 256))
    return pl.pallas_call(
        kern, grid=(m // tm, n // tn), in_specs=in_specs,
        out_specs=[p_spec, pl.BlockSpec((tm, LANES), lambda i, j: (i, 0))],
        out_shape=[p_shape, jax.ShapeDtypeStruct((m, LANES), F32)],
        scratch_shapes=[pltpu.VMEM((tm, d), BF16), pltpu.VMEM((1, LANES), F32)],
        compiler_params=params, name="mix_proj")(*args, wf, bf)


FORGET_LANES = 6
FOX_HEADS_PER_STEP = 4
MOBA_ROWS_PER_STEP = 4


def _lane_set(lanes, offset):
    hit = lanes < 0
    for hh in range(FOX_HEADS):
        lo = FORGET_LANES * hh + offset
        hit = hit | ((lanes >= lo) & (lanes < lo + 3))
    return hit


def _fox_kernel(q_ref, k_ref, v_ref, c_ref, o_ref, qc_ref, kc_ref, kaug_ref, vaug_ref, s_ref,
                *, seq_len):
    step = pl.program_id(1)
    tq = ATTN_Q_BLOCK
    nq = seq_len // tq
    r = lax.broadcasted_iota(jnp.int32, (tq, tq), 0)
    c = lax.broadcasted_iota(jnp.int32, (tq, tq), 1)
    causal = c <= r

    @pl.when(step == 0)
    def _():
        hi, mid, lo = _split3(c_ref[...] * LOG2E)
        row = lax.broadcasted_iota(jnp.int32, (3 * LANES, 2 * LANES), 0)
        col = lax.broadcasted_iota(jnp.int32, (3 * LANES, 2 * LANES), 1)
        piece, head = row >> 7, row & (LANES - 1)
        slot = FORGET_LANES * head + piece
        route = jnp.where((head < FOX_HEADS) & (col == slot), 1.0,
                          jnp.where((head < FOX_HEADS) & (col == LANES + 3 + slot), -1.0, 0.0))
        routed = _dot(jnp.concatenate([hi, mid, lo], axis=1), route.astype(BF16))
        lane1 = lax.broadcasted_iota(jnp.int32, (1, LANES), 1)
        qc_ref[...] = (routed[:, :LANES] + jnp.where(_lane_set(lane1, 3), 1.0, 0.0)).astype(BF16)
        kc_ref[...] = (routed[:, LANES:] + jnp.where(_lane_set(lane1, 0), 1.0, 0.0)).astype(BF16)
        for hh in range(FOX_HEADS_PER_STEP):
            vaug_ref[hh, :, HEAD_DIM:] = jnp.ones((seq_len, HEAD_DIM), BF16)

    lanes = lax.broadcasted_iota(jnp.int32, (seq_len, LANES), 1)
    for hh in range(FOX_HEADS_PER_STEP):
        h = step * FOX_HEADS_PER_STEP + hh
        cols = slice(hh * HEAD_DIM, (hh + 1) * HEAD_DIM)
        mine = (lanes >= FORGET_LANES * h) & (lanes < FORGET_LANES * (h + 1))
        kaug_ref[hh, :, :HEAD_DIM] = k_ref[:, cols]
        kaug_ref[hh, :, HEAD_DIM:] = jnp.where(mine, kc_ref[...], jnp.zeros((), BF16))
        vaug_ref[hh, :, :HEAD_DIM] = v_ref[:, cols]

    work = [(hh, i) for hh in range(FOX_HEADS_PER_STEP) for i in reversed(range(nq))]

    def qk(pos):
        hh, i = work[pos]
        n = (i + 1) * tq
        rows = slice(i * tq, (i + 1) * tq)
        q_aug = jnp.concatenate([q_ref[rows, hh * HEAD_DIM:(hh + 1) * HEAD_DIM], qc_ref[rows, :]],
                                axis=1)
        s_ref[pos % 2, :, 0:n] = _dot_nt(q_aug, kaug_ref[hh, 0:n, :])

    qk(0)
    for pos, (hh, i) in enumerate(work):
        n = (i + 1) * tq
        if pos + 1 < len(work):
            qk(pos + 1)
        logits = s_ref[pos % 2, :, 0:n]
        own = jnp.where(causal, logits[:, i * tq:], NEG)
        if i > 0:
            logits = jnp.concatenate([logits[:, :i * tq], own], axis=1)
        else:
            logits = own
        o_ref[i * tq:(i + 1) * tq, hh * HEAD_DIM:(hh + 1) * HEAD_DIM] = _softmax_pv(
            logits, vaug_ref[hh, 0:n, :]).astype(BF16)


def _fox_attn(p, cum, *, batch, seq_len):
    m = p.shape[0]
    hd = HEAD_DIM
    per = FOX_HEADS_PER_STEP
    steps = FOX_HEADS // per
    return pl.pallas_call(
        functools.partial(_fox_kernel, seq_len=seq_len),
        grid=(batch, steps),
        in_specs=[
            pl.BlockSpec((seq_len, per * hd), lambda b, s: (b, s)),
            pl.BlockSpec((seq_len, per * hd), lambda b, s: (b, steps + s)),
            pl.BlockSpec((seq_len, per * hd), lambda b, s: (b, 2 * steps + s)),
            pl.BlockSpec((seq_len, LANES), lambda b, s: (b, 0)),
        ],
        out_specs=pl.BlockSpec((seq_len, per * hd), lambda b, s: (b, s)),
        out_shape=jax.ShapeDtypeStruct((m, FOX_HEADS * hd), BF16),
        scratch_shapes=[pltpu.VMEM((seq_len, LANES), BF16),
                        pltpu.VMEM((seq_len, LANES), BF16),
                        pltpu.VMEM((per, seq_len, hd + LANES), BF16),
                        pltpu.VMEM((per, seq_len, 2 * hd), BF16),
                        pltpu.VMEM((2, ATTN_Q_BLOCK, seq_len), F32)],
        compiler_params=pltpu.CompilerParams(
            dimension_semantics=("arbitrary", "arbitrary"),
            vmem_limit_bytes=VMEM_LIMIT_BYTES),
        name="fox_attn",
    )(p, p, p, cum)


def _t5_bias(dist, rel_ref, h):
    n = jnp.maximum(dist, 0)
    max_exact = REL_BUCKETS // 2
    nf = jnp.maximum(n, 1).astype(F32)
    large = max_exact + (jnp.log(nf / max_exact) / math.log(REL_MAX_DIST / max_exact)
                         * (REL_BUCKETS - max_exact)).astype(jnp.int32)
    large = jnp.minimum(large, REL_BUCKETS - 1)
    bucket = jnp.where(n < max_exact, n, large)
    bias = jnp.zeros(dist.shape, F32)
    for b in range(REL_BUCKETS):
        bias = jnp.where(bucket == b, rel_ref[b, h], bias)
    return bias


def _moba_kernel(rel_ref, q_ref, k_ref, v_ref, o_ref, town_ref, tprev_ref, kaug_ref, qaug_ref,
                 vaug_ref, kmean_ref, s_ref, *, seq_len):
    h = pl.program_id(0)
    blk = MOBA_BLOCK
    nb = seq_len // blk
    per = MOBA_ROWS_PER_STEP
    r = lax.broadcasted_iota(jnp.int32, (blk, blk), 0)
    c = lax.broadcasted_iota(jnp.int32, (blk, blk), 1)
    lane_grp = lax.broadcasted_iota(jnp.int32, (1, LANES), 1) >> 3

    @pl.when(pl.program_id(1) == 0)
    def _():
        town_ref[...] = jnp.where(c <= r, _t5_bias(r - c, rel_ref, h) * LOG2E, NEG)
        tprev_ref[...] = _t5_bias(r - c + blk, rel_ref, h) * LOG2E
        s = lax.broadcasted_iota(jnp.int32, (seq_len, LANES), 0)
        ln = lax.broadcasted_iota(jnp.int32, (seq_len, LANES), 1)
        first_key = (ln & 7) * blk
        onehot = (ln < 32) & (s >= first_key) & (s < first_key + blk)
        for bb in range(per):
            kaug_ref[bb, :, HEAD_DIM:] = jnp.where(onehot, 1.0, 0.0).astype(BF16)
            vaug_ref[bb, :, HEAD_DIM:] = jnp.ones((seq_len, HEAD_DIM), BF16)

    far = jnp.full((1, LANES), rel_ref[REL_BUCKETS - 1, h] * LOG2E, F32)
    far_hi, far_mid, far_lo = _split3(far)
    far_parts = jnp.where(lane_grp == 1, far_hi.astype(F32),
                          jnp.where(lane_grp == 2, far_mid.astype(F32),
                                    jnp.where(lane_grp == 3, far_lo.astype(F32), 0.0)))

    def far_bias(row0, nrows):
        rows = row0 + lax.broadcasted_iota(jnp.int32, (nrows, LANES), 0)
        lanes = lax.broadcasted_iota(jnp.int32, (nrows, LANES), 1)
        own = rows >> (blk.bit_length() - 1)
        return jnp.where((lanes & 7) <= own - 2, far_parts, 0.0), lanes, own

    late = min(seq_len, (MOBA_TOPK + 1) * blk)
    for bb in range(per):
        base = bb * seq_len
        kaug_ref[bb, :, :HEAD_DIM] = k_ref[base:base + seq_len, :]
        vaug_ref[bb, :, :HEAD_DIM] = v_ref[base:base + seq_len, :]
        qaug_ref[bb, :, :HEAD_DIM] = q_ref[base:base + seq_len, :]
        qaug_ref[bb, :late, HEAD_DIM:] = far_bias(0, late)[0].astype(BF16)
        if seq_len > late:
            kmean_ref[bb] = jnp.zeros(kmean_ref.shape[1:], F32)
            for j in range(nb):
                kmean_ref[bb, j:j + 1, :] = jnp.mean(
                    k_ref[base + j * blk:base + (j + 1) * blk, :].astype(F32), axis=0, keepdims=True)
            km_hi, km_mid, km_lo = _split3(kmean_ref[bb])
            ql = q_ref[base + late:base + seq_len, :]
            gate = _dot_nt(ql, km_hi) + _dot_nt(ql, km_mid) + _dot_nt(ql, km_lo)
            aug, lanes_l, own_l = far_bias(late, seq_len - late)
            valid = lanes_l < own_l
            g = jnp.where(valid, gate, NEG)
            sel = jnp.zeros(g.shape, jnp.bool_)
            lane_f = lanes_l.astype(F32)
            for _ in range(MOBA_TOPK):
                best = jnp.max(g, axis=-1, keepdims=True)
                first = jnp.min(jnp.where(g == best, lane_f, float(LANES)), axis=-1, keepdims=True)
                pick = lane_f == first
                sel = sel | pick
                g = jnp.where(pick, -jnp.inf, g)
            qaug_ref[bb, late:, HEAD_DIM:] = jnp.where(valid & jnp.logical_not(sel), NEG,
                                                       aug).astype(BF16)

    n_early = min(nb, MOBA_TOPK + 1)
    order = list(reversed(range(n_early))) + list(reversed(range(n_early, nb)))
    work = [(bb, i) for bb in range(per) for i in order]

    def qk(pos):
        bb, i = work[pos]
        n = (i + 1) * blk
        s_ref[pos % 2, :, 0:n] = _dot_nt(qaug_ref[bb, i * blk:(i + 1) * blk, :], kaug_ref[bb, 0:n, :])

    qk(0)
    for pos, (bb, i) in enumerate(work):
        n = (i + 1) * blk
        if pos + 1 < len(work):
            qk(pos + 1)
        s = s_ref[pos % 2, :, 0:n]
        pieces = [s[:, i * blk:] + town_ref[...]]
        if i >= 1:
            pieces.insert(0, s[:, (i - 1) * blk:i * blk] + tprev_ref[...])
        if i >= 2:
            pieces.insert(0, s[:, :(i - 1) * blk])
        logits = jnp.concatenate(pieces, axis=1) if len(pieces) > 1 else pieces[0]
        rows = slice(bb * seq_len + i * blk, bb * seq_len + (i + 1) * blk)
        o_ref[rows, :] = _softmax_pv(logits, vaug_ref[bb, 0:n, :]).astype(BF16)


def _moba_attn(p, rel_bias, *, batch, seq_len, col0):
    m = p.shape[0]
    hd = HEAD_DIM
    blk = MOBA_BLOCK
    per = MOBA_ROWS_PER_STEP
    assert seq_len % blk == 0 and seq_len // blk <= 8 and batch % per == 0
    rows = per * seq_len
    return pl.pallas_call(
        functools.partial(_moba_kernel, seq_len=seq_len),
        grid=(MOBA_HEADS, batch // per),
        in_specs=[
            pl.BlockSpec(memory_space=pltpu.SMEM),
            pl.BlockSpec((rows, hd), lambda h, b: (b, col0 + h)),
            pl.BlockSpec((rows, hd), lambda h, b: (b, col0 + MOBA_HEADS + h)),
            pl.BlockSpec((rows, hd), lambda h, b: (b, col0 + 2 * MOBA_HEADS + h)),
        ],
        out_specs=pl.BlockSpec((rows, hd), lambda h, b: (b, h)),
        out_shape=jax.ShapeDtypeStruct((m, MOBA_HEADS * hd), BF16),
        scratch_shapes=[
            pltpu.VMEM((blk, blk), F32),
            pltpu.VMEM((blk, blk), F32),
            pltpu.VMEM((per, seq_len, hd + LANES), BF16),
            pltpu.VMEM((per, seq_len, hd + LANES), BF16),
            pltpu.VMEM((per, seq_len, 2 * hd), BF16),
            pltpu.VMEM((per, LANES, hd), F32),
            pltpu.VMEM((2, blk, seq_len), F32),
        ],
        compiler_params=pltpu.CompilerParams(
            dimension_semantics=("arbitrary", "arbitrary"),
            vmem_limit_bytes=VMEM_LIMIT_BYTES),
        name="moba_attn",
    )(rel_bias, p, p, p)


def _mem_kernel(q_ref, k_ref, v_ref, o_ref, s_ref, *, seq_len, tq):
    heads = q_ref.shape[1] // HEAD_DIM
    cols = [slice(h * HEAD_DIM, (h + 1) * HEAD_DIM) for h in range(heads)]
    ones = jnp.ones((v_ref.shape[0], HEAD_DIM), BF16)
    v_ones = [jnp.concatenate([v_ref[:, c], ones], axis=1) for c in cols]
    work = [(h, i) for h in range(heads) for i in range(seq_len // tq)]

    def qk(pos):
        h, i = work[pos]
        s_ref[pos % 2] = _dot_nt(q_ref[i * tq:(i + 1) * tq, cols[h]], k_ref[:, cols[h]])

    qk(0)
    for pos, (h, i) in enumerate(work):
        if pos + 1 < len(work):
            qk(pos + 1)
        o_ref[i * tq:(i + 1) * tq, cols[h]] = _softmax_pv(s_ref[pos % 2], v_ones[h]).astype(BF16)


def _mem_attn(p, mkv, *, batch, seq_len, n_mem, col0):
    m = p.shape[0]
    width = MEM_HEADS * HEAD_DIM
    tq = min(seq_len, 512)
    assert col0 % MEM_HEADS == 0
    return pl.pallas_call(
        functools.partial(_mem_kernel, seq_len=seq_len, tq=tq),
        grid=(batch,),
        in_specs=[
            pl.BlockSpec((seq_len, width), lambda b: (b, col0 // MEM_HEADS)),
            pl.BlockSpec((n_mem, width), lambda b: (b, 0)),
            pl.BlockSpec((n_mem, width), lambda b: (b, 1)),
        ],
        out_specs=pl.BlockSpec((seq_len, width), lambda b: (b, 0)),
        out_shape=jax.ShapeDtypeStruct((m, width), BF16),
        scratch_shapes=[pltpu.VMEM((2, tq, n_mem), F32)],
        compiler_params=pltpu.CompilerParams(
            dimension_semantics=("arbitrary",), vmem_limit_bytes=VMEM_LIMIT_BYTES),
        name="mem_attn",
    )(p, mkv, mkv)


def _out_proj_kernel(x_ref, of_ref, ob_ref, om_ref, w_ref, o_ref):
    wf = of_ref.shape[1]
    wb = ob_ref.shape[1]
    acc = _dot(of_ref[...], w_ref[0:wf, :])
    acc += _dot(ob_ref[...], w_ref[wf:wf + wb, :])
    acc += _dot(om_ref[...], w_ref[wf + wb:, :])
    o_ref[...] = x_ref[...] + acc


def _out_proj(x2d, o_fox, o_moba, o_mem, w_out, *, tm):
    m, d = x2d.shape
    return pl.pallas_call(
        _out_proj_kernel,
        grid=(m // tm,),
        in_specs=[
            pl.BlockSpec((tm, d), lambda i: (i, 0)),
            pl.BlockSpec((tm, o_fox.shape[1]), lambda i: (i, 0)),
            pl.BlockSpec((tm, o_moba.shape[1]), lambda i: (i, 0)),
            pl.BlockSpec((tm, o_mem.shape[1]), lambda i: (i, 0)),
            pl.BlockSpec(w_out.shape, lambda i: (0, 0), pipeline_mode=pl.Buffered(1)),
        ],
        out_specs=pl.BlockSpec((tm, d), lambda i: (i, 0)),
        out_shape=jax.ShapeDtypeStruct((m, d), F32),
        compiler_params=pltpu.CompilerParams(
            dimension_semantics=("arbitrary",), vmem_limit_bytes=VMEM_LIMIT_BYTES),
        name="out_proj",
    )(x2d, o_fox, o_moba, o_mem, w_out)


def _tile(total, want):
    t = min(total, want)
    assert total % t == 0, (total, want)
    return t


def kernel(x, mem, ffn1_norm, ffn1_w1, ffn1_w3, ffn1_w2, mix_norm, mem_norm, w_in, b_forget,
           w_mem_kv, fox_q_gain, fox_k_gain, moba_q_gain, moba_k_gain, mem_q_gain, mem_k_gain,
           w_out, ffn2_norm, ffn2_w1, ffn2_w3, ffn2_w2, rel_bias):
    batch, seq_len, d = x.shape
    n_mem = mem.shape[1]
    depth = w_in.shape[0]
    fox_w = FOX_HEADS * HEAD_DIM
    moba_w = MOBA_HEADS * HEAD_DIM
    mem_w = MEM_HEADS * HEAD_DIM
    m = batch * seq_len
    tm = _tile(seq_len, TOKEN_TILE)
    ones = jnp.ones((HEAD_DIM,), F32)
    q_scale = HEAD_DIM ** -0.5 * LOG2E

    x2d = x.reshape(m, d)
    mem2d = mem.reshape(batch * n_mem, d)
    for l in range(depth):
        tf = _tile(ffn1_w1.shape[2], FF_TILE)
        n_i, n_f = m // tm, ffn1_w1.shape[2] // tf
        d_ff2 = ffn2_w1.shape[2]
        rows, ff_cols = d // n_i, d_ff2 // n_f
        n_steps = n_i * n_f
        jobs = (
            _cast_job(ffn2_w1[l], (rows, ff_cols), lambda i, f: (i, f)),
            _cast_job(ffn2_w3[l], (rows, ff_cols), lambda i, f: (i, f)),
            _cast_job(ffn2_w2[l], (ff_cols, rows), lambda i, f: (f, i)),
            _cast_job(w_out[l], (_row_slabs(w_out.shape[1], n_f, n_steps)[0], d),
                      _row_slabs(w_out.shape[1], n_f, n_steps)[1]),
            _cast_job(w_mem_kv[l], (_row_slabs(d, n_f, n_steps)[0], w_mem_kv.shape[2]),
                      _row_slabs(d, n_f, n_steps)[1]),
        ) + _w_in_jobs(jnp.swapaxes(w_in[l], 0, 1), n_f, n_steps)
        head, w1_1, w3_1, w2_1 = _ffn_head(x2d, ffn1_norm[l], ffn1_w1[l], ffn1_w3[l], ffn1_w2[l],
                                           tm=tm, tf=_tile(ffn1_w1.shape[2], HEAD_FF_TILE))
        x2d, (w1_2, w3_2, w2_2, w_out_bf, w_mem_bf, w_main, w_forget) = _ffn(
            x2d, ffn1_norm[l], w1_1, w3_1, w2_1, tm=tm, tf=tf, jobs=jobs, head=head)

        b_pad = jnp.pad(b_forget[l].astype(F32), (0, LANES - FOX_HEADS)).reshape(1, LANES)
        head_gain = jnp.concatenate(
            [jnp.tile(fox_q_gain[l] * q_scale, FOX_HEADS), jnp.tile(fox_k_gain[l], FOX_HEADS),
             jnp.tile(ones, FOX_HEADS), jnp.tile(moba_q_gain[l] * q_scale, MOBA_HEADS),
             jnp.tile(moba_k_gain[l], MOBA_HEADS), jnp.tile(ones, MOBA_HEADS),
             jnp.tile(mem_q_gain[l] * q_scale, MEM_HEADS)]).astype(F32)
        head_flag = jnp.concatenate(
            [jnp.ones((2 * fox_w,), F32), jnp.zeros((fox_w,), F32), jnp.ones((2 * moba_w,), F32),
             jnp.zeros((moba_w,), F32), jnp.ones((mem_w,), F32)])
        proj, cum = _norm_proj(x2d, mix_norm[l], w_main, head_gain, head_flag, tm=tm,
                               tn=_tile(w_main.shape[0], PROJ_COL_TILE), forget=(w_forget, b_pad),
                               seq_len=seq_len)

        kv_gain = jnp.concatenate([jnp.tile(mem_k_gain[l], MEM_HEADS), jnp.tile(ones, MEM_HEADS)])
        kv_flag = jnp.concatenate([jnp.ones((mem_w,), F32), jnp.zeros((mem_w,), F32)])
        mkv = _norm_proj(mem2d, mem_norm[l], w_mem_bf, kv_gain.astype(F32), kv_flag,
                         tm=_tile(batch * n_mem, MEM_PROJ_TILE), tn=_tile(2 * mem_w, MEM_PROJ_TILE))

        o_fox = _fox_attn(proj, cum, batch=batch, seq_len=seq_len)
        o_moba = _moba_attn(proj, rel_bias.astype(F32), batch=batch, seq_len=seq_len,
                            col0=3 * FOX_HEADS)
        o_mem = _mem_attn(proj, mkv, batch=batch, seq_len=seq_len, n_mem=n_mem,
                          col0=3 * FOX_HEADS + 3 * MOBA_HEADS)
        x2d = _out_proj(x2d, o_fox, o_moba, o_mem, w_out_bf, tm=tm)

        x2d, _ = _ffn(x2d, ffn2_norm[l], w1_2, w3_2, w2_2, tm=tm, tf=_tile(d_ff2, FF_TILE))
    return x2d.reshape(batch, seq_len, d)
```

```python
import functools
import math

import jax
import jax.numpy as jnp
from jax import lax
from jax.experimental import pallas as pl
from jax.experimental.pallas import tpu as pltpu

HEAD_DIM = 128
FOX_HEADS = 8
MOBA_HEADS = 4
MEM_HEADS = 4
MOBA_BLOCK = 256
MOBA_TOPK = 3
REL_BUCKETS = 32
REL_MAX_DIST = 128
EPS = 1e-6
NEG = -1e30
LOG2E = math.log2(math.e)

LANES = 128
BF16_SUBLANES = 16
MXU_WIDTH = 256
ATTN_Q_BLOCK = 256
NORM_ROW_CHUNKS = 4
TOKEN_TILE = 1024
FF_TILE = 512
HEAD_FF_TILE = 256
PROJ_COL_TILE = 2560
MEM_PROJ_TILE = 1024
VMEM_LIMIT_BYTES = 60 * 1024 * 1024

F32 = jnp.float32
BF16 = jnp.bfloat16


def _rms_scale(x):
    return lax.rsqrt(jnp.mean(x * x, axis=-1, keepdims=True) + EPS)


def _dot(a, b):
    return jnp.dot(a, b, preferred_element_type=F32)


def _dot_nt(a, b):
    return lax.dot_general(a, b, (((1,), (1,)), ((), ())), preferred_element_type=F32)


def _split3(x):
    hi = x.astype(BF16)
    r1 = x - hi.astype(F32)
    mid = r1.astype(BF16)
    lo = (r1 - mid.astype(F32)).astype(BF16)
    return hi, mid, lo


def _softmax_pv(logits2, v_ones):
    mx = jnp.max(logits2, axis=-1, keepdims=True)
    p = jnp.exp2(logits2 - mx).astype(BF16)
    pv = _dot(p, v_ones)
    return pv[:, :HEAD_DIM] / pv[:, HEAD_DIM:]


class _SideJob:
    def __init__(self, inputs, outputs, body):
        self.inputs, self.outputs, self.body = inputs, outputs, body


def _norm_to(x_ref, g_ref, xn_ref, o_ref):
    x = x_ref[...]
    xn_ref[...] = (x * _rms_scale(x) * g_ref[...]).astype(BF16)
    o_ref[...] = x


def _swiglu_half(xn, w1, w3, w2):
    chunks = [slice(c, c + MXU_WIDTH) for c in range(0, w1.shape[1], MXU_WIDTH)]
    ups = [(_dot(xn, w1[:, c]), _dot(xn, w3[:, c])) for c in chunks]
    out = None
    for c, (h1, h3) in zip(chunks, ups):
        act = (0.5 * h1 * jax.nn.sigmoid(h1) * h3).astype(BF16)
        part = _dot(act, w2[c, :])
        out = part if out is None else out + part
    return out


def _ffn_head_kernel(x_ref, g_ref, w1_ref, w3_ref, w2_ref, o_ref, w1b_ref, w3b_ref, w2b_ref, xn_ref):
    pl.when(pl.program_id(0) == 0)(functools.partial(_norm_to, x_ref, g_ref, xn_ref, o_ref))
    w1b_ref[...] = w1_ref[...].astype(BF16)
    w3b_ref[...] = w3_ref[...].astype(BF16)
    w2b_ref[...] = w2_ref[...].astype(BF16)
    o_ref[...] += _swiglu_half(xn_ref[...], w1b_ref[...], w3b_ref[...], w2b_ref[...])


def _ffn_head(x2d, gain, w1, w3, w2, *, tm, tf):
    d = x2d.shape[1]
    d_ff = w1.shape[1]
    up = pl.BlockSpec((d, tf), lambda f: (0, f))
    down = pl.BlockSpec((tf, d), lambda f: (f, 0))
    row = pl.BlockSpec((tm, d), lambda f: (0, 0))
    return pl.pallas_call(
        _ffn_head_kernel,
        grid=(d_ff // tf,),
        in_specs=[row, pl.BlockSpec((1, d), lambda f: (0, 0)), up, up, down],
        out_specs=[row, up, up, down],
        out_shape=[jax.ShapeDtypeStruct((tm, d), F32), jax.ShapeDtypeStruct(w1.shape, BF16),
                   jax.ShapeDtypeStruct(w3.shape, BF16), jax.ShapeDtypeStruct(w2.shape, BF16)],
        scratch_shapes=[pltpu.VMEM((tm, d), BF16)],
        compiler_params=pltpu.CompilerParams(
            dimension_semantics=("arbitrary",), vmem_limit_bytes=VMEM_LIMIT_BYTES),
        name="ffn_head",
    )(x2d, gain.reshape(1, d), w1, w3, w2)


def _ffn_kernel(*refs, jobs, has_head):
    x_ref, g_ref, w1_ref, w3_ref, w2_ref = refs[:5]
    n_in = 5 + has_head + sum(len(j.inputs) for j in jobs)
    side_in = refs[5 + has_head:n_in]
    o_ref = refs[n_in]
    side_out = refs[n_in + 1:-1]
    xn_ref = refs[-1]
    i = pl.program_id(0)
    first = pl.program_id(1) == 0

    def first_step():
        chunk = x_ref.shape[0] // NORM_ROW_CHUNKS
        for rc in range(NORM_ROW_CHUNKS):
            rows = slice(rc * chunk, (rc + 1) * chunk)
            x = x_ref[rows, :]
            xn = (x * _rms_scale(x) * g_ref[...]).astype(BF16)
            xn_ref[rows, :] = xn
            o_ref[rows, :] = x + _swiglu_half(xn, w1_ref[...], w3_ref[...], w2_ref[...])

    def later_step():
        o_ref[...] += _swiglu_half(xn_ref[...], w1_ref[...], w3_ref[...], w2_ref[...])

    def tile():
        pl.when(first)(first_step)
        pl.when(jnp.logical_not(first))(later_step)

    if has_head:
        head_ref = refs[5]
        pl.when((i == 0) & first)(lambda: pltpu.sync_copy(head_ref, o_ref))
        pl.when(i > 0)(tile)
    else:
        tile()

    for job in jobs:
        ins, side_in = side_in[:len(job.inputs)], side_in[len(job.inputs):]
        outs, side_out = side_out[:len(job.outputs)], side_out[len(job.outputs):]
        job.body(ins, outs)


def _ffn(x2d, gain, w1, w3, w2, *, tm, tf, jobs=(), head=None):
    m, d = x2d.shape
    d_ff = w1.shape[1]
    has_head = head is not None
    col = (lambda i, f: jnp.where(i == 0, 0, f)) if has_head else (lambda i, f: f)
    side_in = [io for j in jobs for io in j.inputs]
    side_out = [io for j in jobs for io in j.outputs]
    res = pl.pallas_call(
        functools.partial(_ffn_kernel, jobs=jobs, has_head=has_head),
        grid=(m // tm, d_ff // tf),
        in_specs=[
            pl.BlockSpec((tm, d), lambda i, f: (i, 0)),
            pl.BlockSpec((1, d), lambda i, f: (0, 0)),
            pl.BlockSpec((d, tf), lambda i, f: (0, col(i, f))),
            pl.BlockSpec((d, tf), lambda i, f: (0, col(i, f))),
            pl.BlockSpec((tf, d), lambda i, f: (col(i, f), 0)),
        ] + [pl.BlockSpec(memory_space=pl.ANY)] * has_head + [spec for _, spec in side_in],
        out_specs=[pl.BlockSpec((tm, d), lambda i, f: (i, 0))] + [spec for _, spec in side_out],
        out_shape=[jax.ShapeDtypeStruct((m, d), F32)] + [struct for struct, _ in side_out],
        scratch_shapes=[pltpu.VMEM((tm, d), BF16)],
        compiler_params=pltpu.CompilerParams(
            dimension_semantics=("arbitrary", "arbitrary"),
            vmem_limit_bytes=VMEM_LIMIT_BYTES),
        name="ffn",
    )(x2d, gain.reshape(1, d), w1, w3, w2, *([head] if has_head else []),
      *[arr for arr, _ in side_in])
    return res[0], res[1:]


def _cast_job(w, block, index_map):
    def body(ins, outs):
        outs[0][...] = ins[0][...].astype(BF16)
    spec = pl.BlockSpec(block, index_map)
    return _SideJob([(w, spec)], [(jax.ShapeDtypeStruct(w.shape, BF16), spec)], body)


def _row_slabs(nrows, n_f, n_steps):
    rows = BF16_SUBLANES
    while nrows % rows or nrows // rows > n_steps:
        rows += BF16_SUBLANES
    last = nrows // rows - 1
    return rows, lambda i, f: (jnp.minimum(i * n_f + f, last), 0)


def _w_in_jobs(w_in_t, n_f, n_steps):
    in_w, d = w_in_t.shape
    f0 = 3 * FOX_HEADS * HEAD_DIM
    main_w = in_w - FOX_HEADS
    rows = BF16_SUBLANES
    while main_w % rows or f0 % rows or main_w // rows > n_steps:
        rows += BF16_SUBLANES
    last = main_w // rows - 1
    step = lambda i, f: jnp.minimum(i * n_f + f, last)
    per_row_block = rows // FOX_HEADS

    def main_body(ins, outs):
        t = jnp.minimum(pl.program_id(0) * n_f + pl.program_id(1), last)
        a = ins[0][...]
        shifted = jnp.concatenate([a[FOX_HEADS:], ins[1][...]], axis=0)
        outs[0][...] = jnp.where(t * rows >= f0, shifted, a).astype(BF16)

    def forget_body(ins, outs):
        pad = jnp.zeros((LANES - FOX_HEADS, d), F32)
        outs[0][...] = jnp.concatenate([ins[0][...], pad], axis=0).astype(BF16)

    main = _SideJob(
        [(w_in_t, pl.BlockSpec((rows, d), lambda i, f: (step(i, f), 0))),
         (w_in_t, pl.BlockSpec((FOX_HEADS, d), lambda i, f: ((step(i, f) + 1) * per_row_block, 0)))],
        [(jax.ShapeDtypeStruct((main_w, d), BF16),
          pl.BlockSpec((rows, d), lambda i, f: (step(i, f), 0)))],
        main_body)
    forget = _SideJob(
        [(w_in_t, pl.BlockSpec((FOX_HEADS, d), lambda i, f: (f0 // FOX_HEADS, 0)))],
        [(jax.ShapeDtypeStruct((LANES, d), BF16), pl.BlockSpec((LANES, d), lambda i, f: (0, 0)))],
        forget_body)
    return main, forget


def _project_heads(xn_ref, w_ref, hg_ref, hflag_ref, p_ref, w_transposed):
    xn = xn_ref[...]
    for sb in range(p_ref.shape[1] // MXU_WIDTH):
        if w_transposed:
            y = _dot_nt(xn, w_ref[sb * MXU_WIDTH:(sb + 1) * MXU_WIDTH, :])
        else:
            y = _dot(xn, w_ref[:, sb * MXU_WIDTH:(sb + 1) * MXU_WIDTH])
        for hh in range(MXU_WIDTH // HEAD_DIM):
            cols = slice(sb * MXU_WIDTH + hh * HEAD_DIM, sb * MXU_WIDTH + (hh + 1) * HEAD_DIM)
            yh = y[:, hh * HEAD_DIM:(hh + 1) * HEAD_DIM]
            normed = yh * _rms_scale(yh) * hg_ref[:, cols]
            p_ref[:, cols] = jnp.where(hflag_ref[:, cols] > 0.0, normed, yh).astype(BF16)


def _proj_kernel(x_ref, g_ref, w_ref, hg_ref, hflag_ref, p_ref, xn_ref):
    @pl.when(pl.program_id(1) == 0)
    def _():
        x = x_ref[...]
        xn_ref[...] = (x * _rms_scale(x) * g_ref[...]).astype(BF16)

    _project_heads(xn_ref, w_ref, hg_ref, hflag_ref, p_ref, w_transposed=False)


def _proj_forget_kernel(x_ref, g_ref, w_ref, hg_ref, hflag_ref, wf_ref, bf_ref,
                        p_ref, c_ref, xn_ref, carry_ref, *, tiles_per_seq, cum_block):
    i = pl.program_id(0)
    j = pl.program_id(1)

    @pl.when(j == 0)
    def _():
        x = x_ref[...]
        xn_ref[...] = (x * _rms_scale(x) * g_ref[...]).astype(BF16)

    def forget_gates():
        @pl.when(i % tiles_per_seq == 0)
        def _():
            carry_ref[...] = jnp.zeros_like(carry_ref)

        z = _dot_nt(xn_ref[...], wf_ref[...]) + bf_ref[...]
        logf = jnp.minimum(z, 0.0) - jnp.log1p(jnp.exp(-jnp.abs(z)))
        r = lax.broadcasted_iota(jnp.int32, (cum_block, cum_block), 0)
        c = lax.broadcasted_iota(jnp.int32, (cum_block, cum_block), 1)
        tril = (c <= r).astype(BF16)
        local = []
        for blk in range(logf.shape[0] // cum_block):
            hi, mid, lo = _split3(logf[blk * cum_block:(blk + 1) * cum_block, :])
            local.append(_dot(tril, hi) + _dot(tril, mid) + _dot(tril, lo))
        carry = carry_ref[...]
        for blk, loc in enumerate(local):
            cum = loc + carry
            c_ref[blk * cum_block:(blk + 1) * cum_block, :] = cum
            carry = cum[cum_block - 1:cum_block, :]
        carry_ref[...] = carry

    heads = functools.partial(_project_heads, xn_ref, w_ref, hg_ref, hflag_ref, p_ref,
                              w_transposed=True)

    @pl.when(j == 1)
    def _():
        forget_gates()
        heads()

    @pl.when(j != 1)
    def _():
        heads()


def _norm_proj(x2d, gain, w, head_gain, head_flag, *, tm, tn, forget=None, seq_len=None):
    m, d = x2d.shape
    n = w.shape[1] if forget is None else w.shape[0]
    w_spec = (pl.BlockSpec((d, tn), lambda i, j: (0, j)) if forget is None
              else pl.BlockSpec((tn, d), lambda i, j: (j, 0)))
    in_specs = [
        pl.BlockSpec((tm, d), lambda i, j: (i, 0)),
        pl.BlockSpec((1, d), lambda i, j: (0, 0)),
        w_spec,
        pl.BlockSpec((1, tn), lambda i, j: (0, j)),
        pl.BlockSpec((1, tn), lambda i, j: (0, j)),
    ]
    args = [x2d, gain.reshape(1, d), w, head_gain.reshape(1, n), head_flag.reshape(1, n)]
    p_spec = pl.BlockSpec((tm, tn), lambda i, j: (i, j))
    p_shape = jax.ShapeDtypeStruct((m, n), BF16)
    params = pltpu.CompilerParams(dimension_semantics=("arbitrary", "arbitrary"),
                                  vmem_limit_bytes=VMEM_LIMIT_BYTES)
    if forget is None:
        return pl.pallas_call(
            _proj_kernel, grid=(m // tm, n // tn), in_specs=in_specs, out_specs=p_spec,
            out_shape=p_shape, scratch_shapes=[pltpu.VMEM((tm, d), BF16)],
            compiler_params=params, name="mem_proj")(*args)
    wf, bf = forget
    assert n // tn >= 2
    in_specs += [pl.BlockSpec((LANES, d), lambda i, j: (0, 0)),
                 pl.BlockSpec((1, LANES), lambda i, j: (0, 0))]
    kern = functools.partial(_proj_forget_kernel, tiles_per_seq=seq_len // tm,
                             cum_block=min(tm, 256))
    return pl.pallas_call(
        kern, grid=(m // tm, n // tn), in_specs=in_specs,
        out_specs=[p_spec, pl.BlockSpec((tm, LANES), lambda i, j: (i, 0))],
        out_shape=[p_shape, jax.ShapeDtypeStruct((m, LANES), F32)],
        scratch_shapes=[pltpu.VMEM((tm, d), BF16), pltpu.VMEM((1, LANES), F32)],
        compiler_params=params, name="mix_proj")(*args, wf, bf)


FORGET_LANES = 6
FOX_HEADS_PER_STEP = 4
MOBA_ROWS_PER_STEP = 4


def _lane_set(lanes, offset):
    hit = lanes < 0
    for hh in range(FOX_HEADS):
        lo = FORGET_LANES * hh + offset
        hit = hit | ((lanes >= lo) & (lanes < lo + 3))
    return hit


def _fox_kernel(q_ref, k_ref, v_ref, c_ref, o_ref, qc_ref, kc_ref, kaug_ref, vaug_ref, s_ref,
                *, seq_len):
    step = pl.program_id(1)
    tq = ATTN_Q_BLOCK
    nq = seq_len // tq
    r = lax.broadcasted_iota(jnp.int32, (tq, tq), 0)
    c = lax.broadcasted_iota(jnp.int32, (tq, tq), 1)
    causal = c <= r

    @pl.when(step == 0)
    def _():
        hi, mid, lo = _split3(c_ref[...] * LOG2E)
        row = lax.broadcasted_iota(jnp.int32, (3 * LANES, 2 * LANES), 0)
        col = lax.broadcasted_iota(jnp.int32, (3 * LANES, 2 * LANES), 1)
        piece, head = row >> 7, row & (LANES - 1)
        slot = FORGET_LANES * head + piece
        route = jnp.where((head < FOX_HEADS) & (col == slot), 1.0,
                          jnp.where((head < FOX_HEADS) & (col == LANES + 3 + slot), -1.0, 0.0))
        routed = _dot(jnp.concatenate([hi, mid, lo], axis=1), route.astype(BF16))
        lane1 = lax.broadcasted_iota(jnp.int32, (1, LANES), 1)
        qc_ref[...] = (routed[:, :LANES] + jnp.where(_lane_set(lane1, 3), 1.0, 0.0)).astype(BF16)
        kc_ref[...] = (routed[:, LANES:] + jnp.where(_lane_set(lane1, 0), 1.0, 0.0)).astype(BF16)
        for hh in range(FOX_HEADS_PER_STEP):
            vaug_ref[hh, :, HEAD_DIM:] = jnp.ones((seq_len, HEAD_DIM), BF16)

    lanes = lax.broadcasted_iota(jnp.int32, (seq_len, LANES), 1)
    for hh in range(FOX_HEADS_PER_STEP):
        h = step * FOX_HEADS_PER_STEP + hh
        cols = slice(hh * HEAD_DIM, (hh + 1) * HEAD_DIM)
        mine = (lanes >= FORGET_LANES * h) & (lanes < FORGET_LANES * (h + 1))
        kaug_ref[hh, :, :HEAD_DIM] = k_ref[:, cols]
        kaug_ref[hh, :, HEAD_DIM:] = jnp.where(mine, kc_ref[...], jnp.zeros((), BF16))
        vaug_ref[hh, :, :HEAD_DIM] = v_ref[:, cols]

    work = [(hh, i) for hh in range(FOX_HEADS_PER_STEP) for i in reversed(range(nq))]

    def qk(pos):
        hh, i = work[pos]
        n = (i + 1) * tq
        rows = slice(i * tq, (i + 1) * tq)
        q_aug = jnp.concatenate([q_ref[rows, hh * HEAD_DIM:(hh + 1) * HEAD_DIM], qc_ref[rows, :]],
                                axis=1)
        s_ref[pos % 2, :, 0:n] = _dot_nt(q_aug, kaug_ref[hh, 0:n, :])

    qk(0)
    for pos, (hh, i) in enumerate(work):
        n = (i + 1) * tq
        if pos + 1 < len(work):
            qk(pos + 1)
        logits = s_ref[pos % 2, :, 0:n]
        own = jnp.where(causal, logits[:, i * tq:], NEG)
        if i > 0:
            logits = jnp.concatenate([logits[:, :i * tq], own], axis=1)
        else:
            logits = own
        o_ref[i * tq:(i + 1) * tq, hh * HEAD_DIM:(hh + 1) * HEAD_DIM] = _softmax_pv(
            logits, vaug_ref[hh, 0:n, :]).astype(BF16)


def _fox_attn(p, cum, *, batch, seq_len):
    m = p.shape[0]
    hd = HEAD_DIM
    per = FOX_HEADS_PER_STEP
    steps = FOX_HEADS // per
    return pl.pallas_call(
        functools.partial(_fox_kernel, seq_len=seq_len),
        grid=(batch, steps),
        in_specs=[
            pl.BlockSpec((seq_len, per * hd), lambda b, s: (b, s)),
            pl.BlockSpec((seq_len, per * hd), lambda b, s: (b, steps + s)),
            pl.BlockSpec((seq_len, per * hd), lambda b, s: (b, 2 * steps + s)),
            pl.BlockSpec((seq_len, LANES), lambda b, s: (b, 0)),
        ],
        out_specs=pl.BlockSpec((seq_len, per * hd), lambda b, s: (b, s)),
        out_shape=jax.ShapeDtypeStruct((m, FOX_HEADS * hd), BF16),
        scratch_shapes=[pltpu.VMEM((seq_len, LANES), BF16),
                        pltpu.VMEM((seq_len, LANES), BF16),
                        pltpu.VMEM((per, seq_len, hd + LANES), BF16),
                        pltpu.VMEM((per, seq_len, 2 * hd), BF16),
                        pltpu.VMEM((2, ATTN_Q_BLOCK, seq_len), F32)],
        compiler_params=pltpu.CompilerParams(
            dimension_semantics=("arbitrary", "arbitrary"),
            vmem_limit_bytes=VMEM_LIMIT_BYTES),
        name="fox_attn",
    )(p, p, p, cum)


def _t5_bias(dist, rel_ref, h):
    n = jnp.maximum(dist, 0)
    max_exact = REL_BUCKETS // 2
    nf = jnp.maximum(n, 1).astype(F32)
    large = max_exact + (jnp.log(nf / max_exact) / math.log(REL_MAX_DIST / max_exact)
                         * (REL_BUCKETS - max_exact)).astype(jnp.int32)
    large = jnp.minimum(large, REL_BUCKETS - 1)
    bucket = jnp.where(n < max_exact, n, large)
    bias = jnp.zeros(dist.shape, F32)
    for b in range(REL_BUCKETS):
        bias = jnp.where(bucket == b, rel_ref[b, h], bias)
    return bias


def _moba_kernel(rel_ref, q_ref, k_ref, v_ref, o_ref, town_ref, tprev_ref, kaug_ref, qaug_ref,
                 vaug_ref, kmean_ref, s_ref, *, seq_len):
    h = pl.program_id(0)
    blk = MOBA_BLOCK
    nb = seq_len // blk
    per = MOBA_ROWS_PER_STEP
    r = lax.broadcasted_iota(jnp.int32, (blk, blk), 0)
    c = lax.broadcasted_iota(jnp.int32, (blk, blk), 1)
    lane_grp = lax.broadcasted_iota(jnp.int32, (1, LANES), 1) >> 3

    @pl.when(pl.program_id(1) == 0)
    def _():
        town_ref[...] = jnp.where(c <= r, _t5_bias(r - c, rel_ref, h) * LOG2E, NEG)
        tprev_ref[...] = _t5_bias(r - c + blk, rel_ref, h) * LOG2E
        s = lax.broadcasted_iota(jnp.int32, (seq_len, LANES), 0)
        ln = lax.broadcasted_iota(jnp.int32, (seq_len, LANES), 1)
        first_key = (ln & 7) * blk
        onehot = (ln < 32) & (s >= first_key) & (s < first_key + blk)
        for bb in range(per):
            kaug_ref[bb, :, HEAD_DIM:] = jnp.where(onehot, 1.0, 0.0).astype(BF16)
            vaug_ref[bb, :, HEAD_DIM:] = jnp.ones((seq_len, HEAD_DIM), BF16)

    far = jnp.full((1, LANES), rel_ref[REL_BUCKETS - 1, h] * LOG2E, F32)
    far_hi, far_mid, far_lo = _split3(far)
    far_parts = jnp.where(lane_grp == 1, far_hi.astype(F32),
                          jnp.where(lane_grp == 2, far_mid.astype(F32),
                                    jnp.where(lane_grp == 3, far_lo.astype(F32), 0.0)))

    def far_bias(row0, nrows):
        rows = row0 + lax.broadcasted_iota(jnp.int32, (nrows, LANES), 0)
        lanes = lax.broadcasted_iota(jnp.int32, (nrows, LANES), 1)
        own = rows >> (blk.bit_length() - 1)
        return jnp.where((lanes & 7) <= own - 2, far_parts, 0.0), lanes, own

    late = min(seq_len, (MOBA_TOPK + 1) * blk)
    for bb in range(per):
        base = bb * seq_len
        kaug_ref[bb, :, :HEAD_DIM] = k_ref[base:base + seq_len, :]
        vaug_ref[bb, :, :HEAD_DIM] = v_ref[base:base + seq_len, :]
        qaug_ref[bb, :, :HEAD_DIM] = q_ref[base:base + seq_len, :]
        qaug_ref[bb, :late, HEAD_DIM:] = far_bias(0, late)[0].astype(BF16)
        if seq_len > late:
            kmean_ref[bb] = jnp.zeros(kmean_ref.shape[1:], F32)
            for j in range(nb):
                kmean_ref[bb, j:j + 1, :] = jnp.mean(
                    k_ref[base + j * blk:base + (j + 1) * blk, :].astype(F32), axis=0, keepdims=True)
            km_hi, km_mid, km_lo = _split3(kmean_ref[bb])
            ql = q_ref[base + late:base + seq_len, :]
            gate = _dot_nt(ql, km_hi) + _dot_nt(ql, km_mid) + _dot_nt(ql, km_lo)
            aug, lanes_l, own_l = far_bias(late, seq_len - late)
            valid = lanes_l < own_l
            g = jnp.where(valid, gate, NEG)
            sel = jnp.zeros(g.shape, jnp.bool_)
            lane_f = lanes_l.astype(F32)
            for _ in range(MOBA_TOPK):
                best = jnp.max(g, axis=-1, keepdims=True)
                first = jnp.min(jnp.where(g == best, lane_f, float(LANES)), axis=-1, keepdims=True)
                pick = lane_f == first
                sel = sel | pick
                g = jnp.where(pick, -jnp.inf, g)
            qaug_ref[bb, late:, HEAD_DIM:] = jnp.where(valid & jnp.logical_not(sel), NEG,
                                                       aug).astype(BF16)

    n_early = min(nb, MOBA_TOPK + 1)
    order = list(reversed(range(n_early))) + list(reversed(range(n_early, nb)))
    work = [(bb, i) for bb in range(per) for i in order]

    def qk(pos):
        bb, i = work[pos]
        n = (i + 1) * blk
        s_ref[pos % 2, :, 0:n] = _dot_nt(qaug_ref[bb, i * blk:(i + 1) * blk, :], kaug_ref[bb, 0:n, :])

    qk(0)
    for pos, (bb, i) in enumerate(work):
        n = (i + 1) * blk
        if pos + 1 < len(work):
            qk(pos + 1)
        s = s_ref[pos % 2, :, 0:n]
        pieces = [s[:, i * blk:] + town_ref[...]]
        if i >= 1:
            pieces.insert(0, s[:, (i - 1) * blk:i * blk] + tprev_ref[...])
        if i >= 2:
            pieces.insert(0, s[:, :(i - 1) * blk])
        logits = jnp.concatenate(pieces, axis=1) if len(pieces) > 1 else pieces[0]
        rows = slice(bb * seq_len + i * blk, bb * seq_len + (i + 1) * blk)
        o_ref[rows, :] = _softmax_pv(logits, vaug_ref[bb, 0:n, :]).astype(BF16)


def _moba_attn(p, rel_bias, *, batch, seq_len, col0):
    m = p.shape[0]
    hd = HEAD_DIM
    blk = MOBA_BLOCK
    per = MOBA_ROWS_PER_STEP
    assert seq_len % blk == 0 and seq_len // blk <= 8 and batch % per == 0
    rows = per * seq_len
    return pl.pallas_call(
        functools.partial(_moba_kernel, seq_len=seq_len),
        grid=(MOBA_HEADS, batch // per),
        in_specs=[
            pl.BlockSpec(memory_space=pltpu.SMEM),
            pl.BlockSpec((rows, hd), lambda h, b: (b, col0 + h)),
            pl.BlockSpec((rows, hd), lambda h, b: (b, col0 + MOBA_HEADS + h)),
            pl.BlockSpec((rows, hd), lambda h, b: (b, col0 + 2 * MOBA_HEADS + h)),
        ],
        out_specs=pl.BlockSpec((rows, hd), lambda h, b: (b, h)),
        out_shape=jax.ShapeDtypeStruct((m, MOBA_HEADS * hd), BF16),
        scratch_shapes=[
            pltpu.VMEM((blk, blk), F32),
            pltpu.VMEM((blk, blk), F32),
            pltpu.VMEM((per, seq_len, hd + LANES), BF16),
            pltpu.VMEM((per, seq_len, hd + LANES), BF16),
            pltpu.VMEM((per, seq_len, 2 * hd), BF16),
            pltpu.VMEM((per, LANES, hd), F32),
            pltpu.VMEM((2, blk, seq_len), F32),
        ],
        compiler_params=pltpu.CompilerParams(
            dimension_semantics=("arbitrary", "arbitrary"),
            vmem_limit_bytes=VMEM_LIMIT_BYTES),
        name="moba_attn",
    )(rel_bias, p, p, p)


def _mem_kernel(q_ref, k_ref, v_ref, o_ref, s_ref, *, seq_len, tq):
    heads = q_ref.shape[1] // HEAD_DIM
    cols = [slice(h * HEAD_DIM, (h + 1) * HEAD_DIM) for h in range(heads)]
    ones = jnp.ones((v_ref.shape[0], HEAD_DIM), BF16)
    v_ones = [jnp.concatenate([v_ref[:, c], ones], axis=1) for c in cols]
    work = [(h, i) for h in range(heads) for i in range(seq_len // tq)]

    def qk(pos):
        h, i = work[pos]
        s_ref[pos % 2] = _dot_nt(q_ref[i * tq:(i + 1) * tq, cols[h]], k_ref[:, cols[h]])

    qk(0)
    for pos, (h, i) in enumerate(work):
        if pos + 1 < len(work):
            qk(pos + 1)
        o_ref[i * tq:(i + 1) * tq, cols[h]] = _softmax_pv(s_ref[pos % 2], v_ones[h]).astype(BF16)


def _mem_attn(p, mkv, *, batch, seq_len, n_mem, col0):
    m = p.shape[0]
    width = MEM_HEADS * HEAD_DIM
    tq = min(seq_len, 512)
    assert col0 % MEM_HEADS == 0
    return pl.pallas_call(
        functools.partial(_mem_kernel, seq_len=seq_len, tq=tq),
        grid=(batch,),
        in_specs=[
            pl.BlockSpec((seq_len, width), lambda b: (b, col0 // MEM_HEADS)),
            pl.BlockSpec((n_mem, width), lambda b: (b, 0)),
            pl.BlockSpec((n_mem, width), lambda b: (b, 1)),
        ],
        out_specs=pl.BlockSpec((seq_len, width), lambda b: (b, 0)),
        out_shape=jax.ShapeDtypeStruct((m, width), BF16),
        scratch_shapes=[pltpu.VMEM((2, tq, n_mem), F32)],
        compiler_params=pltpu.CompilerParams(
            dimension_semantics=("arbitrary",), vmem_limit_bytes=VMEM_LIMIT_BYTES),
        name="mem_attn",
    )(p, mkv, mkv)


def _out_proj_kernel(x_ref, of_ref, ob_ref, om_ref, w_ref, o_ref):
    wf = of_ref.shape[1]
    wb = ob_ref.shape[1]
    acc = _dot(of_ref[...], w_ref[0:wf, :])
    acc += _dot(ob_ref[...], w_ref[wf:wf + wb, :])
    acc += _dot(om_ref[...], w_ref[wf + wb:, :])
    o_ref[...] = x_ref[...] + acc


def _out_proj(x2d, o_fox, o_moba, o_mem, w_out, *, tm):
    m, d = x2d.shape
    return pl.pallas_call(
        _out_proj_kernel,
        grid=(m // tm,),
        in_specs=[
            pl.BlockSpec((tm, d), lambda i: (i, 0)),
            pl.BlockSpec((tm, o_fox.shape[1]), lambda i: (i, 0)),
            pl.BlockSpec((tm, o_moba.shape[1]), lambda i: (i, 0)),
            pl.BlockSpec((tm, o_mem.shape[1]), lambda i: (i, 0)),
            pl.BlockSpec(w_out.shape, lambda i: (0, 0), pipeline_mode=pl.Buffered(1)),
        ],
        out_specs=pl.BlockSpec((tm, d), lambda i: (i, 0)),
        out_shape=jax.ShapeDtypeStruct((m, d), F32),
        compiler_params=pltpu.CompilerParams(
            dimension_semantics=("arbitrary",), vmem_limit_bytes=VMEM_LIMIT_BYTES),
        name="out_proj",
    )(x2d, o_fox, o_moba, o_mem, w_out)


def _tile(total, want):
    t = min(total, want)
    assert total % t == 0, (total, want)
    return t


def kernel(x, mem, ffn1_norm, ffn1_w1, ffn1_w3, ffn1_w2, mix_norm, mem_norm, w_in, b_forget,
           w_mem_kv, fox_q_gain, fox_k_gain, moba_q_gain, moba_k_gain, mem_q_gain, mem_k_gain,
           w_out, ffn2_norm, ffn2_w1, ffn2_w3, ffn2_w2, rel_bias):
    batch, seq_len, d = x.shape
    n_mem = mem.shape[1]
    depth = w_in.shape[0]
    fox_w = FOX_HEADS * HEAD_DIM
    moba_w = MOBA_HEADS * HEAD_DIM
    mem_w = MEM_HEADS * HEAD_DIM
    m = batch * seq_len
    tm = _tile(seq_len, TOKEN_TILE)
    ones = jnp.ones((HEAD_DIM,), F32)
    q_scale = HEAD_DIM ** -0.5 * LOG2E

    x2d = x.reshape(m, d)
    mem2d = mem.reshape(batch * n_mem, d)
    for l in range(depth):
        tf = _tile(ffn1_w1.shape[2], FF_TILE)
        n_i, n_f = m // tm, ffn1_w1.shape[2] // tf
        d_ff2 = ffn2_w1.shape[2]
        rows, ff_cols = d // n_i, d_ff2 // n_f
        n_steps = n_i * n_f
        jobs = (
            _cast_job(ffn2_w1[l], (rows, ff_cols), lambda i, f: (i, f)),
            _cast_job(ffn2_w3[l], (rows, ff_cols), lambda i, f: (i, f)),
            _cast_job(ffn2_w2[l], (_row_slabs(d_ff2, n_f, n_steps)[0], d),
                      _row_slabs(d_ff2, n_f, n_steps)[1]),
            _cast_job(w_out[l], (_row_slabs(w_out.shape[1], n_f, n_steps)[0], d),
                      _row_slabs(w_out.shape[1], n_f, n_steps)[1]),
            _cast_job(w_mem_kv[l], (_row_slabs(d, n_f, n_steps)[0], w_mem_kv.shape[2]),
                      _row_slabs(d, n_f, n_steps)[1]),
        ) + _w_in_jobs(jnp.swapaxes(w_in[l], 0, 1), n_f, n_steps)
        head, w1_1, w3_1, w2_1 = _ffn_head(x2d, ffn1_norm[l], ffn1_w1[l], ffn1_w3[l], ffn1_w2[l],
                                           tm=tm, tf=_tile(ffn1_w1.shape[2], HEAD_FF_TILE))
        x2d, (w1_2, w3_2, w2_2, w_out_bf, w_mem_bf, w_main, w_forget) = _ffn(
            x2d, ffn1_norm[l], w1_1, w3_1, w2_1, tm=tm, tf=tf, jobs=jobs, head=head)

        b_pad = jnp.pad(b_forget[l].astype(F32), (0, LANES - FOX_HEADS)).reshape(1, LANES)
        head_gain = jnp.concatenate(
            [jnp.tile(fox_q_gain[l] * q_scale, FOX_HEADS), jnp.tile(fox_k_gain[l], FOX_HEADS),
             jnp.tile(ones, FOX_HEADS), jnp.tile(moba_q_gain[l] * q_scale, MOBA_HEADS),
             jnp.tile(moba_k_gain[l], MOBA_HEADS), jnp.tile(ones, MOBA_HEADS),
             jnp.tile(mem_q_gain[l] * q_scale, MEM_HEADS)]).astype(F32)
        head_flag = jnp.concatenate(
            [jnp.ones((2 * fox_w,), F32), jnp.zeros((fox_w,), F32), jnp.ones((2 * moba_w,), F32),
             jnp.zeros((moba_w,), F32), jnp.ones((mem_w,), F32)])
        proj, cum = _norm_proj(x2d, mix_norm[l], w_main, head_gain, head_flag, tm=tm,
                               tn=_tile(w_main.shape[0], PROJ_COL_TILE), forget=(w_forget, b_pad),
                               seq_len=seq_len)

        kv_gain = jnp.concatenate([jnp.tile(mem_k_gain[l], MEM_HEADS), jnp.tile(ones, MEM_HEADS)])
        kv_flag = jnp.concatenate([jnp.ones((mem_w,), F32), jnp.zeros((mem_w,), F32)])
        mkv = _norm_proj(mem2d, mem_norm[l], w_mem_bf, kv_gain.astype(F32), kv_flag,
                         tm=_tile(batch * n_mem, MEM_PROJ_TILE), tn=_tile(2 * mem_w, MEM_PROJ_TILE))

        o_fox = _fox_attn(proj, cum, batch=batch, seq_len=seq_len)
        o_moba = _moba_attn(proj, rel_bias.astype(F32), batch=batch, seq_len=seq_len,
                            col0=3 * FOX_HEADS)
        o_mem = _mem_attn(proj, mkv, batch=batch, seq_len=seq_len, n_mem=n_mem,
                          col0=3 * FOX_HEADS + 3 * MOBA_HEADS)
        x2d = _out_proj(x2d, o_fox, o_moba, o_mem, w_out_bf, tm=tm)

        x2d, _ = _ffn(x2d, ffn2_norm[l], w1_2, w3_2, w2_2, tm=tm, tf=_tile(d_ff2, FF_TILE))
    return x2d.reshape(batch, seq_len, d)
```

```python
import functools
import math

import jax
import jax.numpy as jnp
from jax import lax
from jax.experimental import pallas as pl
from jax.experimental.pallas import tpu as pltpu

HEAD_DIM = 128
FOX_HEADS = 8
MOBA_HEADS = 4
MEM_HEADS = 4
MOBA_BLOCK = 256
MOBA_TOPK = 3
REL_BUCKETS = 32
REL_MAX_DIST = 128
EPS = 1e-6
NEG = -1e30
LOG2E = math.log2(math.e)

LANES = 128
BF16_SUBLANES = 16
MXU_WIDTH = 256
ATTN_Q_BLOCK = 256
NORM_ROW_CHUNKS = 4
TOKEN_TILE = 1024
FF_TILE = 512
HEAD_FF_TILE = 256
PROJ_COL_TILE = 2560
MEM_PROJ_TILE = 1024
VMEM_LIMIT_BYTES = 60 * 1024 * 1024

F32 = jnp.float32
BF16 = jnp.bfloat16


def _rms_scale(x):
    return lax.rsqrt(jnp.mean(x * x, axis=-1, keepdims=True) + EPS)


def _dot(a, b):
    return jnp.dot(a, b, preferred_element_type=F32)


def _dot_nt(a, b):
    return lax.dot_general(a, b, (((1,), (1,)), ((), ())), preferred_element_type=F32)


def _split3(x):
    hi = x.astype(BF16)
    r1 = x - hi.astype(F32)
    mid = r1.astype(BF16)
    lo = (r1 - mid.astype(F32)).astype(BF16)
    return hi, mid, lo


def _softmax_pv(logits2, v_ones):
    mx = jnp.max(logits2, axis=-1, keepdims=True)
    p = jnp.exp2(logits2 - mx).astype(BF16)
    pv = _dot(p, v_ones)
    return pv[:, :HEAD_DIM] / pv[:, HEAD_DIM:]


class _SideJob:
    def __init__(self, inputs, outputs, body):
        self.inputs, self.outputs, self.body = inputs, outputs, body


def _norm_to(x_ref, g_ref, xn_ref, o_ref):
    x = x_ref[...]
    xn_ref[...] = (x * _rms_scale(x) * g_ref[...]).astype(BF16)
    o_ref[...] = x


def _swiglu_half(xn, w1, w3, w2):
    chunks = [slice(c, c + MXU_WIDTH) for c in range(0, w1.shape[1], MXU_WIDTH)]
    ups = [(_dot(xn, w1[:, c]), _dot(xn, w3[:, c])) for c in chunks]
    out = None
    for c, (h1, h3) in zip(chunks, ups):
        act = (0.5 * h1 * jax.nn.sigmoid(h1) * h3).astype(BF16)
        part = _dot(act, w2[c, :])
        out = part if out is None else out + part
    return out


def _ffn_head_kernel(x_ref, g_ref, w1_ref, w3_ref, w2_ref, o_ref, w1b_ref, w3b_ref, w2b_ref, xn_ref):
    pl.when(pl.program_id(0) == 0)(functools.partial(_norm_to, x_ref, g_ref, xn_ref, o_ref))
    w1b_ref[...] = w1_ref[...].astype(BF16)
    w3b_ref[...] = w3_ref[...].astype(BF16)
    w2b_ref[...] = w2_ref[...].astype(BF16)
    o_ref[...] += _swiglu_half(xn_ref[...], w1b_ref[...], w3b_ref[...], w2b_ref[...])


def _ffn_head(x2d, gain, w1, w3, w2, *, tm, tf):
    d = x2d.shape[1]
    d_ff = w1.shape[1]
    up = pl.BlockSpec((d, tf), lambda f: (0, f))
    down = pl.BlockSpec((tf, d), lambda f: (f, 0))
    row = pl.BlockSpec((tm, d), lambda f: (0, 0))
    return pl.pallas_call(
        _ffn_head_kernel,
        grid=(d_ff // tf,),
        in_specs=[row, pl.BlockSpec((1, d), lambda f: (0, 0)), up, up, down],
        out_specs=[row, up, up, down],
        out_shape=[jax.ShapeDtypeStruct((tm, d), F32), jax.ShapeDtypeStruct(w1.shape, BF16),
                   jax.ShapeDtypeStruct(w3.shape, BF16), jax.ShapeDtypeStruct(w2.shape, BF16)],
        scratch_shapes=[pltpu.VMEM((tm, d), BF16)],
        compiler_params=pltpu.CompilerParams(
            dimension_semantics=("arbitrary",), vmem_limit_bytes=VMEM_LIMIT_BYTES),
        name="ffn_head",
    )(x2d, gain.reshape(1, d), w1, w3, w2)


def _ffn_kernel(*refs, jobs, has_head):
    x_ref, g_ref, w1_ref, w3_ref, w2_ref = refs[:5]
    n_in = 5 + has_head + sum(len(j.inputs) for j in jobs)
    side_in = refs[5 + has_head:n_in]
    o_ref = refs[n_in]
    side_out = refs[n_in + 1:-1]
    xn_ref = refs[-1]
    i = pl.program_id(0)
    first = pl.program_id(1) == 0

    def first_step():
        chunk = x_ref.shape[0] // NORM_ROW_CHUNKS
        for rc in range(NORM_ROW_CHUNKS):
            rows = slice(rc * chunk, (rc + 1) * chunk)
            x = x_ref[rows, :]
            xn = (x * _rms_scale(x) * g_ref[...]).astype(BF16)
            xn_ref[rows, :] = xn
            o_ref[rows, :] = x + _swiglu_half(xn, w1_ref[...], w3_ref[...], w2_ref[...])

    def run_jobs():
        ins_left, outs_left = side_in, side_out
        for job in jobs:
            ins, ins_left = ins_left[:len(job.inputs)], ins_left[len(job.inputs):]
            outs, outs_left = outs_left[:len(job.outputs)], outs_left[len(job.outputs):]
            job.body(ins, outs)

    def later_step():
        o_ref[...] += _swiglu_half(xn_ref[...], w1_ref[...], w3_ref[...], w2_ref[...])
        run_jobs()

    def tile():
        pl.when(first)(first_step)
        pl.when(jnp.logical_not(first))(later_step)

    if has_head:
        head_ref = refs[5]
        pl.when((i == 0) & first)(lambda: pltpu.sync_copy(head_ref, o_ref))
        pl.when(i > 0)(tile)
        pl.when((i == 0) | first)(run_jobs)
    else:
        tile()
        pl.when(first)(run_jobs)


def _ffn(x2d, gain, w1, w3, w2, *, tm, tf, jobs=(), head=None):
    m, d = x2d.shape
    d_ff = w1.shape[1]
    has_head = head is not None
    col = (lambda i, f: jnp.where(i == 0, 0, f)) if has_head else (lambda i, f: f)
    side_in = [io for j in jobs for io in j.inputs]
    side_out = [io for j in jobs for io in j.outputs]
    res = pl.pallas_call(
        functools.partial(_ffn_kernel, jobs=jobs, has_head=has_head),
        grid=(m // tm, d_ff // tf),
        in_specs=[
            pl.BlockSpec((tm, d), lambda i, f: (i, 0)),
            pl.BlockSpec((1, d), lambda i, f: (0, 0)),
            pl.BlockSpec((d, tf), lambda i, f: (0, col(i, f))),
            pl.BlockSpec((d, tf), lambda i, f: (0, col(i, f))),
            pl.BlockSpec((tf, d), lambda i, f: (col(i, f), 0)),
        ] + [pl.BlockSpec(memory_space=pl.ANY)] * has_head + [spec for _, spec in side_in],
        out_specs=[pl.BlockSpec((tm, d), lambda i, f: (i, 0))] + [spec for _, spec in side_out],
        out_shape=[jax.ShapeDtypeStruct((m, d), F32)] + [struct for struct, _ in side_out],
        scratch_shapes=[pltpu.VMEM((tm, d), BF16)],
        compiler_params=pltpu.CompilerParams(
            dimension_semantics=("arbitrary", "arbitrary"),
            vmem_limit_bytes=VMEM_LIMIT_BYTES),
        name="ffn",
    )(x2d, gain.reshape(1, d), w1, w3, w2, *([head] if has_head else []),
      *[arr for arr, _ in side_in])
    return res[0], res[1:]


def _cast_job(w, block, index_map):
    def body(ins, outs):
        outs[0][...] = ins[0][...].astype(BF16)
    spec = pl.BlockSpec(block, index_map)
    return _SideJob([(w, spec)], [(jax.ShapeDtypeStruct(w.shape, BF16), spec)], body)


def _row_slabs(nrows, n_f, n_steps):
    rows = BF16_SUBLANES
    while nrows % rows or nrows // rows > n_steps:
        rows += BF16_SUBLANES
    last = nrows // rows - 1
    return rows, lambda i, f: (jnp.minimum(i * n_f + f, last), 0)


def _w_in_jobs(w_in_t, n_f, n_steps):
    in_w, d = w_in_t.shape
    f0 = 3 * FOX_HEADS * HEAD_DIM
    main_w = in_w - FOX_HEADS
    rows = BF16_SUBLANES
    while main_w % rows or f0 % rows or main_w // rows > n_steps:
        rows += BF16_SUBLANES
    last = main_w // rows - 1
    step = lambda i, f: jnp.minimum(i * n_f + f, last)
    per_row_block = rows // FOX_HEADS

    def main_body(ins, outs):
        t = jnp.minimum(pl.program_id(0) * n_f + pl.program_id(1), last)
        a = ins[0][...]
        shifted = jnp.concatenate([a[FOX_HEADS:], ins[1][...]], axis=0)
        outs[0][...] = jnp.where(t * rows >= f0, shifted, a).astype(BF16)

    def forget_body(ins, outs):
        pad = jnp.zeros((LANES - FOX_HEADS, d), F32)
        outs[0][...] = jnp.concatenate([ins[0][...], pad], axis=0).astype(BF16)

    main = _SideJob(
        [(w_in_t, pl.BlockSpec((rows, d), lambda i, f: (step(i, f), 0))),
         (w_in_t, pl.BlockSpec((FOX_HEADS, d), lambda i, f: ((step(i, f) + 1) * per_row_block, 0)))],
        [(jax.ShapeDtypeStruct((main_w, d), BF16),
          pl.BlockSpec((rows, d), lambda i, f: (step(i, f), 0)))],
        main_body)
    forget = _SideJob(
        [(w_in_t, pl.BlockSpec((FOX_HEADS, d), lambda i, f: (f0 // FOX_HEADS, 0)))],
        [(jax.ShapeDtypeStruct((LANES, d), BF16), pl.BlockSpec((LANES, d), lambda i, f: (0, 0)))],
        forget_body)
    return main, forget


def _project_heads(xn_ref, w_ref, hg_ref, hflag_ref, p_ref, w_transposed):
    xn = xn_ref[...]
    for sb in range(p_ref.shape[1] // MXU_WIDTH):
        if w_transposed:
            y = _dot_nt(xn, w_ref[sb * MXU_WIDTH:(sb + 1) * MXU_WIDTH, :])
        else:
            y = _dot(xn, w_ref[:, sb * MXU_WIDTH:(sb + 1) * MXU_WIDTH])
        for hh in range(MXU_WIDTH // HEAD_DIM):
            cols = slice(sb * MXU_WIDTH + hh * HEAD_DIM, sb * MXU_WIDTH + (hh + 1) * HEAD_DIM)
            yh = y[:, hh * HEAD_DIM:(hh + 1) * HEAD_DIM]
            normed = yh * _rms_scale(yh) * hg_ref[:, cols]
            p_ref[:, cols] = jnp.where(hflag_ref[:, cols] > 0.0, normed, yh).astype(BF16)


def _proj_kernel(x_ref, g_ref, w_ref, hg_ref, hflag_ref, p_ref, xn_ref):
    @pl.when(pl.program_id(1) == 0)
    def _():
        x = x_ref[...]
        xn_ref[...] = (x * _rms_scale(x) * g_ref[...]).astype(BF16)

    _project_heads(xn_ref, w_ref, hg_ref, hflag_ref, p_ref, w_transposed=False)


def _proj_forget_kernel(x_ref, g_ref, w_ref, hg_ref, hflag_ref, wf_ref, bf_ref,
                        p_ref, c_ref, xn_ref, carry_ref, *, tiles_per_seq, cum_block):
    i = pl.program_id(0)
    j = pl.program_id(1)

    @pl.when(j == 0)
    def _():
        x = x_ref[...]
        xn_ref[...] = (x * _rms_scale(x) * g_ref[...]).astype(BF16)

    def forget_gates():
        @pl.when(i % tiles_per_seq == 0)
        def _():
            carry_ref[...] = jnp.zeros_like(carry_ref)

        z = _dot_nt(xn_ref[...], wf_ref[...]) + bf_ref[...]
        logf = jnp.minimum(z, 0.0) - jnp.log1p(jnp.exp(-jnp.abs(z)))
        r = lax.broadcasted_iota(jnp.int32, (cum_block, cum_block), 0)
        c = lax.broadcasted_iota(jnp.int32, (cum_block, cum_block), 1)
        tril = (c <= r).astype(BF16)
        local = []
        for blk in range(logf.shape[0] // cum_block):
            hi, mid, lo = _split3(logf[blk * cum_block:(blk + 1) * cum_block, :])
            local.append(_dot(tril, hi) + _dot(tril, mid) + _dot(tril, lo))
        carry = carry_ref[...]
        for blk, loc in enumerate(local):
            cum = loc + carry
            c_ref[blk * cum_block:(blk + 1) * cum_block, :] = cum
            carry = cum[cum_block - 1:cum_block, :]
        carry_ref[...] = carry

    heads = functools.partial(_project_heads, xn_ref, w_ref, hg_ref, hflag_ref, p_ref,
                              w_transposed=True)

    @pl.when(j == 1)
    def _():
        forget_gates()
        heads()

    @pl.when(j != 1)
    def _():
        heads()


def _norm_proj(x2d, gain, w, head_gain, head_flag, *, tm, tn, forget=None, seq_len=None):
    m, d = x2d.shape
    n = w.shape[1] if forget is None else w.shape[0]
    w_spec = (pl.BlockSpec((d, tn), lambda i, j: (0, j)) if forget is None
              else pl.BlockSpec((tn, d), lambda i, j: (j, 0)))
    in_specs = [
        pl.BlockSpec((tm, d), lambda i, j: (i, 0)),
        pl.BlockSpec((1, d), lambda i, j: (0, 0)),
        w_spec,
        pl.BlockSpec((1, tn), lambda i, j: (0, j)),
        pl.BlockSpec((1, tn), lambda i, j: (0, j)),
    ]
    args = [x2d, gain.reshape(1, d), w, head_gain.reshape(1, n), head_flag.reshape(1, n)]
    p_spec = pl.BlockSpec((tm, tn), lambda i, j: (i, j))
    p_shape = jax.ShapeDtypeStruct((m, n), BF16)
    params = pltpu.CompilerParams(dimension_semantics=("arbitrary", "arbitrary"),
                                  vmem_limit_bytes=VMEM_LIMIT_BYTES)
    if forget is None:
        return pl.pallas_call(
            _proj_kernel, grid=(m // tm, n // tn), in_specs=in_specs, out_specs=p_spec,
            out_shape=p_shape, scratch_shapes=[pltpu.VMEM((tm, d), BF16)],
            compiler_params=params, name="mem_proj")(*args)
    wf, bf = forget
    assert n // tn >= 2
    in_specs += [pl.BlockSpec((LANES, d), lambda i, j: (0, 0)),
                 pl.BlockSpec((1, LANES), lambda i, j: (0, 0))]
    kern = functools.partial(_proj_forget_kernel, tiles_per_seq=seq_len // tm,
                             cum_block=min(tm, 256))
    return pl.pallas_call(
        kern, grid=(m // tm, n // tn), in_specs=in_specs,
        out_specs=[p_spec, pl.BlockSpec((tm, LANES), lambda i, j: (i, 0))],
        out_shape=[p_shape, jax.ShapeDtypeStruct((m, LANES), F32)],
        scratch_shapes=[pltpu.VMEM((tm, d), BF16), pltpu.VMEM((1, LANES), F32)],
        compiler_params=params, name="mix_proj")(*args, wf, bf)


FORGET_LANES = 6
FOX_HEADS_PER_STEP = 4
MOBA_ROWS_PER_STEP = 4


def _lane_set(lanes, offset):
    hit = lanes < 0
    for hh in range(FOX_HEADS):
        lo = FORGET_LANES * hh + offset
        hit = hit | ((lanes >= lo) & (lanes < lo + 3))
    return hit


def _fox_kernel(q_ref, k_ref, v_ref, c_ref, o_ref, qc_ref, kc_ref, kaug_ref, vaug_ref, s_ref,
                *, seq_len):
    step = pl.program_id(1)
    tq = ATTN_Q_BLOCK
    nq = seq_len // tq
    r = lax.broadcasted_iota(jnp.int32, (tq, tq), 0)
    c = lax.broadcasted_iota(jnp.int32, (tq, tq), 1)
    causal = c <= r

    @pl.when(step == 0)
    def _():
        hi, mid, lo = _split3(c_ref[...] * LOG2E)
        row = lax.broadcasted_iota(jnp.int32, (3 * LANES, 2 * LANES), 0)
        col = lax.broadcasted_iota(jnp.int32, (3 * LANES, 2 * LANES), 1)
        piece, head = row >> 7, row & (LANES - 1)
        slot = FORGET_LANES * head + piece
        route = jnp.where((head < FOX_HEADS) & (col == slot), 1.0,
                          jnp.where((head < FOX_HEADS) & (col == LANES + 3 + slot), -1.0, 0.0))
        routed = _dot(jnp.concatenate([hi, mid, lo], axis=1), route.astype(BF16))
        lane1 = lax.broadcasted_iota(jnp.int32, (1, LANES), 1)
        qc_ref[...] = (routed[:, :LANES] + jnp.where(_lane_set(lane1, 3), 1.0, 0.0)).astype(BF16)
        kc_ref[...] = (routed[:, LANES:] + jnp.where(_lane_set(lane1, 0), 1.0, 0.0)).astype(BF16)
        for hh in range(FOX_HEADS_PER_STEP):
            vaug_ref[hh, :, HEAD_DIM:] = jnp.ones((seq_len, HEAD_DIM), BF16)

    lanes = lax.broadcasted_iota(jnp.int32, (seq_len, LANES), 1)
    for hh in range(FOX_HEADS_PER_STEP):
        h = step * FOX_HEADS_PER_STEP + hh
        cols = slice(hh * HEAD_DIM, (hh + 1) * HEAD_DIM)
        mine = (lanes >= FORGET_LANES * h) & (lanes < FORGET_LANES * (h + 1))
        kaug_ref[hh, :, :HEAD_DIM] = k_ref[:, cols]
        kaug_ref[hh, :, HEAD_DIM:] = jnp.where(mine, kc_ref[...], jnp.zeros((), BF16))
        vaug_ref[hh, :, :HEAD_DIM] = v_ref[:, cols]

    work = [(hh, i) for hh in range(FOX_HEADS_PER_STEP) for i in reversed(range(nq))]

    def qk(pos):
        hh, i = work[pos]
        n = (i + 1) * tq
        rows = slice(i * tq, (i + 1) * tq)
        q_aug = jnp.concatenate([q_ref[rows, hh * HEAD_DIM:(hh + 1) * HEAD_DIM], qc_ref[rows, :]],
                                axis=1)
        s_ref[pos % 2, :, 0:n] = _dot_nt(q_aug, kaug_ref[hh, 0:n, :])

    qk(0)
    for pos, (hh, i) in enumerate(work):
        n = (i + 1) * tq
        if pos + 1 < len(work):
            qk(pos + 1)
        logits = s_ref[pos % 2, :, 0:n]
        own = jnp.where(causal, logits[:, i * tq:], NEG)
        if i > 0:
            logits = jnp.concatenate([logits[:, :i * tq], own], axis=1)
        else:
            logits = own
        o_ref[i * tq:(i + 1) * tq, hh * HEAD_DIM:(hh + 1) * HEAD_DIM] = _softmax_pv(
            logits, vaug_ref[hh, 0:n, :]).astype(BF16)


def _fox_attn(p, cum, *, batch, seq_len):
    m = p.shape[0]
    hd = HEAD_DIM
    per = FOX_HEADS_PER_STEP
    steps = FOX_HEADS // per
    return pl.pallas_call(
        functools.partial(_fox_kernel, seq_len=seq_len),
        grid=(batch, steps),
        in_specs=[
            pl.BlockSpec((seq_len, per * hd), lambda b, s: (b, s)),
            pl.BlockSpec((seq_len, per * hd), lambda b, s: (b, steps + s)),
            pl.BlockSpec((seq_len, per * hd), lambda b, s: (b, 2 * steps + s)),
            pl.BlockSpec((seq_len, LANES), lambda b, s: (b, 0)),
        ],
        out_specs=pl.BlockSpec((seq_len, per * hd), lambda b, s: (b, s)),
        out_shape=jax.ShapeDtypeStruct((m, FOX_HEADS * hd), BF16),
        scratch_shapes=[pltpu.VMEM((seq_len, LANES), BF16),
                        pltpu.VMEM((seq_len, LANES), BF16),
                        pltpu.VMEM((per, seq_len, hd + LANES), BF16),
                        pltpu.VMEM((per, seq_len, 2 * hd), BF16),
                        pltpu.VMEM((2, ATTN_Q_BLOCK, seq_len), F32)],
        compiler_params=pltpu.CompilerParams(
            dimension_semantics=("arbitrary", "arbitrary"),
            vmem_limit_bytes=VMEM_LIMIT_BYTES),
        name="fox_attn",
    )(p, p, p, cum)


def _t5_bias(dist, rel_ref, h):
    n = jnp.maximum(dist, 0)
    max_exact = REL_BUCKETS // 2
    nf = jnp.maximum(n, 1).astype(F32)
    large = max_exact + (jnp.log(nf / max_exact) / math.log(REL_MAX_DIST / max_exact)
                         * (REL_BUCKETS - max_exact)).astype(jnp.int32)
    large = jnp.minimum(large, REL_BUCKETS - 1)
    bucket = jnp.where(n < max_exact, n, large)
    bias = jnp.zeros(dist.shape, F32)
    for b in range(REL_BUCKETS):
        bias = jnp.where(bucket == b, rel_ref[b, h], bias)
    return bias


def _moba_kernel(rel_ref, q_ref, k_ref, v_ref, o_ref, town_ref, tprev_ref, kaug_ref, qaug_ref,
                 vaug_ref, kmean_ref, s_ref, *, seq_len):
    h = pl.program_id(0)
    blk = MOBA_BLOCK
    nb = seq_len // blk
    per = MOBA_ROWS_PER_STEP
    r = lax.broadcasted_iota(jnp.int32, (blk, blk), 0)
    c = lax.broadcasted_iota(jnp.int32, (blk, blk), 1)
    lane_grp = lax.broadcasted_iota(jnp.int32, (1, LANES), 1) >> 3

    @pl.when(pl.program_id(1) == 0)
    def _():
        town_ref[...] = jnp.where(c <= r, _t5_bias(r - c, rel_ref, h) * LOG2E, NEG)
        tprev_ref[...] = _t5_bias(r - c + blk, rel_ref, h) * LOG2E
        s = lax.broadcasted_iota(jnp.int32, (seq_len, LANES), 0)
        ln = lax.broadcasted_iota(jnp.int32, (seq_len, LANES), 1)
        first_key = (ln & 7) * blk
        onehot = (ln < 32) & (s >= first_key) & (s < first_key + blk)
        for bb in range(per):
            kaug_ref[bb, :, HEAD_DIM:] = jnp.where(onehot, 1.0, 0.0).astype(BF16)
            vaug_ref[bb, :, HEAD_DIM:] = jnp.ones((seq_len, HEAD_DIM), BF16)

    far = jnp.full((1, LANES), rel_ref[REL_BUCKETS - 1, h] * LOG2E, F32)
    far_hi, far_mid, far_lo = _split3(far)
    far_parts = jnp.where(lane_grp == 1, far_hi.astype(F32),
                          jnp.where(lane_grp == 2, far_mid.astype(F32),
                                    jnp.where(lane_grp == 3, far_lo.astype(F32), 0.0)))

    def far_bias(row0, nrows):
        rows = row0 + lax.broadcasted_iota(jnp.int32, (nrows, LANES), 0)
        lanes = lax.broadcasted_iota(jnp.int32, (nrows, LANES), 1)
        own = rows >> (blk.bit_length() - 1)
        return jnp.where((lanes & 7) <= own - 2, far_parts, 0.0), lanes, own

    late = min(seq_len, (MOBA_TOPK + 1) * blk)
    for bb in range(per):
        base = bb * seq_len
        kaug_ref[bb, :, :HEAD_DIM] = k_ref[base:base + seq_len, :]
        vaug_ref[bb, :, :HEAD_DIM] = v_ref[base:base + seq_len, :]
        qaug_ref[bb, :, :HEAD_DIM] = q_ref[base:base + seq_len, :]
        qaug_ref[bb, :late, HEAD_DIM:] = far_bias(0, late)[0].astype(BF16)
        if seq_len > late:
            kmean_ref[bb] = jnp.zeros(kmean_ref.shape[1:], F32)
            for j in range(nb):
                kmean_ref[bb, j:j + 1, :] = jnp.mean(
                    k_ref[base + j * blk:base + (j + 1) * blk, :].astype(F32), axis=0, keepdims=True)
            km_hi, km_mid, km_lo = _split3(kmean_ref[bb])
            ql = q_ref[base + late:base + seq_len, :]
            gate = _dot_nt(ql, km_hi) + _dot_nt(ql, km_mid) + _dot_nt(ql, km_lo)
            aug, lanes_l, own_l = far_bias(late, seq_len - late)
            valid = lanes_l < own_l
            g = jnp.where(valid, gate, NEG)
            sel = jnp.zeros(g.shape, jnp.bool_)
            lane_f = lanes_l.astype(F32)
            for _ in range(MOBA_TOPK):
                best = jnp.max(g, axis=-1, keepdims=True)
                first = jnp.min(jnp.where(g == best, lane_f, float(LANES)), axis=-1, keepdims=True)
                pick = lane_f == first
                sel = sel | pick
                g = jnp.where(pick, -jnp.inf, g)
            qaug_ref[bb, late:, HEAD_DIM:] = jnp.where(valid & jnp.logical_not(sel), NEG,
                                                       aug).astype(BF16)

    n_early = min(nb, MOBA_TOPK + 1)
    order = list(reversed(range(n_early))) + list(reversed(range(n_early, nb)))
    work = [(bb, i) for bb in range(per) for i in order]

    def qk(pos):
        bb, i = work[pos]
        n = (i + 1) * blk
        s_ref[pos % 2, :, 0:n] = _dot_nt(qaug_ref[bb, i * blk:(i + 1) * blk, :], kaug_ref[bb, 0:n, :])

    qk(0)
    for pos, (bb, i) in enumerate(work):
        n = (i + 1) * blk
        if pos + 1 < len(work):
            qk(pos + 1)
        s = s_ref[pos % 2, :, 0:n]
        pieces = [s[:, i * blk:] + town_ref[...]]
        if i >= 1:
            pieces.insert(0, s[:, (i - 1) * blk:i * blk] + tprev_ref[...])
        if i >= 2:
            pieces.insert(0, s[:, :(i - 1) * blk])
        logits = jnp.concatenate(pieces, axis=1) if len(pieces) > 1 else pieces[0]
        rows = slice(bb * seq_len + i * blk, bb * seq_len + (i + 1) * blk)
        o_ref[rows, :] = _softmax_pv(logits, vaug_ref[bb, 0:n, :]).astype(BF16)


def _moba_attn(p, rel_bias, *, batch, seq_len, col0):
    m = p.shape[0]
    hd = HEAD_DIM
    blk = MOBA_BLOCK
    per = MOBA_ROWS_PER_STEP
    assert seq_len % blk == 0 and seq_len // blk <= 8 and batch % per == 0
    rows = per * seq_len
    return pl.pallas_call(
        functools.partial(_moba_kernel, seq_len=seq_len),
        grid=(MOBA_HEADS, batch // per),
        in_specs=[
            pl.BlockSpec(memory_space=pltpu.SMEM),
            pl.BlockSpec((rows, hd), lambda h, b: (b, col0 + h)),
            pl.BlockSpec((rows, hd), lambda h, b: (b, col0 + MOBA_HEADS + h)),
            pl.BlockSpec((rows, hd), lambda h, b: (b, col0 + 2 * MOBA_HEADS + h)),
        ],
        out_specs=pl.BlockSpec((rows, hd), lambda h, b: (b, h)),
        out_shape=jax.ShapeDtypeStruct((m, MOBA_HEADS * hd), BF16),
        scratch_shapes=[
            pltpu.VMEM((blk, blk), F32),
            pltpu.VMEM((blk, blk), F32),
            pltpu.VMEM((per, seq_len, hd + LANES), BF16),
            pltpu.VMEM((per, seq_len, hd + LANES), BF16),
            pltpu.VMEM((per, seq_len, 2 * hd), BF16),
            pltpu.VMEM((per, LANES, hd), F32),
            pltpu.VMEM((2, blk, seq_len), F32),
        ],
        compiler_params=pltpu.CompilerParams(
            dimension_semantics=("arbitrary", "arbitrary"),
            vmem_limit_bytes=VMEM_LIMIT_BYTES),
        name="moba_attn",
    )(rel_bias, p, p, p)


def _mem_kernel(q_ref, k_ref, v_ref, o_ref, s_ref, *, seq_len, tq):
    heads = q_ref.shape[1] // HEAD_DIM
    cols = [slice(h * HEAD_DIM, (h + 1) * HEAD_DIM) for h in range(heads)]
    ones = jnp.ones((v_ref.shape[0], HEAD_DIM), BF16)
    v_ones = [jnp.concatenate([v_ref[:, c], ones], axis=1) for c in cols]
    work = [(h, i) for h in range(heads) for i in range(seq_len // tq)]

    def qk(pos):
        h, i = work[pos]
        s_ref[pos % 2] = _dot_nt(q_ref[i * tq:(i + 1) * tq, cols[h]], k_ref[:, cols[h]])

    qk(0)
    for pos, (h, i) in enumerate(work):
        if pos + 1 < len(work):
            qk(pos + 1)
        o_ref[i * tq:(i + 1) * tq, cols[h]] = _softmax_pv(s_ref[pos % 2], v_ones[h]).astype(BF16)


def _mem_attn(p, mkv, *, batch, seq_len, n_mem, col0):
    m = p.shape[0]
    width = MEM_HEADS * HEAD_DIM
    tq = min(seq_len, 512)
    assert col0 % MEM_HEADS == 0
    return pl.pallas_call(
        functools.partial(_mem_kernel, seq_len=seq_len, tq=tq),
        grid=(batch,),
        in_specs=[
            pl.BlockSpec((seq_len, width), lambda b: (b, col0 // MEM_HEADS)),
            pl.BlockSpec((n_mem, width), lambda b: (b, 0)),
            pl.BlockSpec((n_mem, width), lambda b: (b, 1)),
        ],
        out_specs=pl.BlockSpec((seq_len, width), lambda b: (b, 0)),
        out_shape=jax.ShapeDtypeStruct((m, width), BF16),
        scratch_shapes=[pltpu.VMEM((2, tq, n_mem), F32)],
        compiler_params=pltpu.CompilerParams(
            dimension_semantics=("arbitrary",), vmem_limit_bytes=VMEM_LIMIT_BYTES),
        name="mem_attn",
    )(p, mkv, mkv)


def _out_proj_kernel(x_ref, of_ref, ob_ref, om_ref, w_ref, o_ref):
    wf = of_ref.shape[1]
    wb = ob_ref.shape[1]
    acc = _dot(of_ref[...], w_ref[0:wf, :])
    acc += _dot(ob_ref[...], w_ref[wf:wf + wb, :])
    acc += _dot(om_ref[...], w_ref[wf + wb:, :])
    o_ref[...] = x_ref[...] + acc


def _out_proj(x2d, o_fox, o_moba, o_mem, w_out, *, tm):
    m, d = x2d.shape
    return pl.pallas_call(
        _out_proj_kernel,
        grid=(m // tm,),
        in_specs=[
            pl.BlockSpec((tm, d), lambda i: (i, 0)),
            pl.BlockSpec((tm, o_fox.shape[1]), lambda i: (i, 0)),
            pl.BlockSpec((tm, o_moba.shape[1]), lambda i: (i, 0)),
            pl.BlockSpec((tm, o_mem.shape[1]), lambda i: (i, 0)),
            pl.BlockSpec(w_out.shape, lambda i: (0, 0), pipeline_mode=pl.Buffered(1)),
        ],
        out_specs=pl.BlockSpec((tm, d), lambda i: (i, 0)),
        out_shape=jax.ShapeDtypeStruct((m, d), F32),
        compiler_params=pltpu.CompilerParams(
            dimension_semantics=("arbitrary",), vmem_limit_bytes=VMEM_LIMIT_BYTES),
        name="out_proj",
    )(x2d, o_fox, o_moba, o_mem, w_out)


def _tile(total, want):
    t = min(total, want)
    assert total % t == 0, (total, want)
    return t


def kernel(x, mem, ffn1_norm, ffn1_w1, ffn1_w3, ffn1_w2, mix_norm, mem_norm, w_in, b_forget,
           w_mem_kv, fox_q_gain, fox_k_gain, moba_q_gain, moba_k_gain, mem_q_gain, mem_k_gain,
           w_out, ffn2_norm, ffn2_w1, ffn2_w3, ffn2_w2, rel_bias):
    batch, seq_len, d = x.shape
    n_mem = mem.shape[1]
    depth = w_in.shape[0]
    fox_w = FOX_HEADS * HEAD_DIM
    moba_w = MOBA_HEADS * HEAD_DIM
    mem_w = MEM_HEADS * HEAD_DIM
    m = batch * seq_len
    tm = _tile(seq_len, TOKEN_TILE)
    ones = jnp.ones((HEAD_DIM,), F32)
    q_scale = HEAD_DIM ** -0.5 * LOG2E

    x2d = x.reshape(m, d)
    mem2d = mem.reshape(batch * n_mem, d)
    for l in range(depth):
        tf = _tile(ffn1_w1.shape[2], FF_TILE)
        n_i, n_f = m // tm, ffn1_w1.shape[2] // tf
        d_ff2 = ffn2_w1.shape[2]
        rows, ff_cols = d // n_i, d_ff2 // n_f
        n_steps = n_i * n_f
        jobs = (
            _cast_job(ffn2_w1[l], (rows, ff_cols), lambda i, f: (i, f)),
            _cast_job(ffn2_w3[l], (rows, ff_cols), lambda i, f: (i, f)),
            _cast_job(ffn2_w2[l], (ff_cols, rows), lambda i, f: (f, i)),
            _cast_job(w_out[l], (_row_slabs(w_out.shape[1], n_f, n_steps)[0], d),
                      _row_slabs(w_out.shape[1], n_f, n_steps)[1]),
            _cast_job(w_mem_kv[l], (_row_slabs(d, n_f, n_steps)[0], w_mem_kv.shape[2]),
                      _row_slabs(d, n_f, n_steps)[1]),
        ) + _w_in_jobs(jnp.swapaxes(w_in[l], 0, 1), n_f, n_steps)
        head, w1_1, w3_1, w2_1 = _ffn_head(x2d, ffn1_norm[l], ffn1_w1[l], ffn1_w3[l], ffn1_w2[l],
                                           tm=tm, tf=_tile(ffn1_w1.shape[2], HEAD_FF_TILE))
        x2d, (w1_2, w3_2, w2_2, w_out_bf, w_mem_bf, w_main, w_forget) = _ffn(
            x2d, ffn1_norm[l], w1_1, w3_1, w2_1, tm=tm, tf=tf, jobs=jobs, head=head)

        b_pad = jnp.pad(b_forget[l].astype(F32), (0, LANES - FOX_HEADS)).reshape(1, LANES)
        head_gain = jnp.concatenate(
            [jnp.tile(fox_q_gain[l] * q_scale, FOX_HEADS), jnp.tile(fox_k_gain[l], FOX_HEADS),
             jnp.tile(ones, FOX_HEADS), jnp.tile(moba_q_gain[l] * q_scale, MOBA_HEADS),
             jnp.tile(moba_k_gain[l], MOBA_HEADS), jnp.tile(ones, MOBA_HEADS),
             jnp.tile(mem_q_gain[l] * q_scale, MEM_HEADS)]).astype(F32)
        head_flag = jnp.concatenate(
            [jnp.ones((2 * fox_w,), F32), jnp.zeros((fox_w,), F32), jnp.ones((2 * moba_w,), F32),
             jnp.zeros((moba_w,), F32), jnp.ones((mem_w,), F32)])
        proj, cum = _norm_proj(x2d, mix_norm[l], w_main, head_gain, head_flag, tm=tm,
                               tn=_tile(w_main.shape[0], PROJ_COL_TILE), forget=(w_forget, b_pad),
                               seq_len=seq_len)

        kv_gain = jnp.concatenate([jnp.tile(mem_k_gain[l], MEM_HEADS), jnp.tile(ones, MEM_HEADS)])
        kv_flag = jnp.concatenate([jnp.ones((mem_w,), F32), jnp.zeros((mem_w,), F32)])
        mkv = _norm_proj(mem2d, mem_norm[l], w_mem_bf, kv_gain.astype(F32), kv_flag,
                         tm=_tile(batch * n_mem, MEM_PROJ_TILE), tn=_tile(2 * mem_w, MEM_PROJ_TILE))

        o_fox = _fox_attn(proj, cum, batch=batch, seq_len=seq_len)
        o_moba = _moba_attn(proj, rel_bias.astype(F32), batch=batch, seq_len=seq_len,
                            col0=3 * FOX_HEADS)
        o_mem = _mem_attn(proj, mkv, batch=batch, seq_len=seq_len, n_mem=n_mem,
                          col0=3 * FOX_HEADS + 3 * MOBA_HEADS)
        x2d = _out_proj(x2d, o_fox, o_moba, o_mem, w_out_bf, tm=tm)

        x2d, _ = _ffn(x2d, ffn2_norm[l], w1_2, w3_2, w2_2, tm=tm, tf=_tile(d_ff2, FF_TILE))
    return x2d.reshape(batch, seq_len, d)
```

```python
import functools
import math

import jax
import jax.numpy as jnp
from jax import lax
from jax.experimental import pallas as pl
from jax.experimental.pallas import tpu as pltpu

HEAD_DIM = 128
FOX_HEADS = 8
MOBA_HEADS = 4
MEM_HEADS = 4
MOBA_BLOCK = 256
MOBA_TOPK = 3
REL_BUCKETS = 32
REL_MAX_DIST = 128
EPS = 1e-6
NEG = -1e30
LOG2E = math.log2(math.e)

LANES = 128
BF16_SUBLANES = 16
MXU_WIDTH = 256
ATTN_Q_BLOCK = 256
NORM_ROW_CHUNKS = 4
TOKEN_TILE = 1024
FF_TILE = 512
HEAD_FF_TILE = 256
PROJ_COL_TILE = 2560
VMEM_LIMIT_BYTES = 60 * 1024 * 1024

F32 = jnp.float32
BF16 = jnp.bfloat16


def _rms_scale(x):
    return lax.rsqrt(jnp.mean(x * x, axis=-1, keepdims=True) + EPS)


def _dot(a, b):
    return jnp.dot(a, b, preferred_element_type=F32)


def _dot_nt(a, b):
    return lax.dot_general(a, b, (((1,), (1,)), ((), ())), preferred_element_type=F32)


def _split3(x):
    hi = x.astype(BF16)
    r1 = x - hi.astype(F32)
    mid = r1.astype(BF16)
    lo = (r1 - mid.astype(F32)).astype(BF16)
    return hi, mid, lo


def _softmax_pv(logits2, v_ones):
    mx = jnp.max(logits2, axis=-1, keepdims=True)
    p = jnp.exp2(logits2 - mx).astype(BF16)
    pv = _dot(p, v_ones)
    return pv[:, :HEAD_DIM] / pv[:, HEAD_DIM:]


class _SideJob:
    def __init__(self, inputs, outputs, body):
        self.inputs, self.outputs, self.body = inputs, outputs, body


def _norm_to(x_ref, g_ref, xn_ref, o_ref):
    x = x_ref[...]
    xn_ref[...] = (x * _rms_scale(x) * g_ref[...]).astype(BF16)
    o_ref[...] = x


def _swiglu_half(xn, w1, w3, w2):
    chunks = [slice(c, c + MXU_WIDTH) for c in range(0, w1.shape[1], MXU_WIDTH)]
    ups = [(_dot(xn, w1[:, c]), _dot(xn, w3[:, c])) for c in chunks]
    out = None
    for c, (h1, h3) in zip(chunks, ups):
        act = (0.5 * h1 * jax.nn.sigmoid(h1) * h3).astype(BF16)
        part = _dot(act, w2[c, :])
        out = part if out is None else out + part
    return out


def _ffn_head_kernel(x_ref, g_ref, w1_ref, w3_ref, w2_ref, o_ref, w1b_ref, w3b_ref, w2b_ref, xn_ref):
    pl.when(pl.program_id(0) == 0)(functools.partial(_norm_to, x_ref, g_ref, xn_ref, o_ref))
    w1b_ref[...] = w1_ref[...].astype(BF16)
    w3b_ref[...] = w3_ref[...].astype(BF16)
    w2b_ref[...] = w2_ref[...].astype(BF16)
    o_ref[...] += _swiglu_half(xn_ref[...], w1b_ref[...], w3b_ref[...], w2b_ref[...])


def _ffn_head(x2d, gain, w1, w3, w2, *, tm, tf):
    d = x2d.shape[1]
    d_ff = w1.shape[1]
    up = pl.BlockSpec((d, tf), lambda f: (0, f))
    down = pl.BlockSpec((tf, d), lambda f: (f, 0))
    row = pl.BlockSpec((tm, d), lambda f: (0, 0))
    return pl.pallas_call(
        _ffn_head_kernel,
        grid=(d_ff // tf,),
        in_specs=[row, pl.BlockSpec((1, d), lambda f: (0, 0)), up, up, down],
        out_specs=[row, up, up, down],
        out_shape=[jax.ShapeDtypeStruct((tm, d), F32), jax.ShapeDtypeStruct(w1.shape, BF16),
                   jax.ShapeDtypeStruct(w3.shape, BF16), jax.ShapeDtypeStruct(w2.shape, BF16)],
        scratch_shapes=[pltpu.VMEM((tm, d), BF16)],
        compiler_params=pltpu.CompilerParams(
            dimension_semantics=("arbitrary",), vmem_limit_bytes=VMEM_LIMIT_BYTES),
        name="ffn_head",
    )(x2d, gain.reshape(1, d), w1, w3, w2)


def _ffn_kernel(*refs, jobs, has_head):
    x_ref, g_ref, w1_ref, w3_ref, w2_ref = refs[:5]
    n_in = 5 + has_head + sum(len(j.inputs) for j in jobs)
    side_in = refs[5 + has_head:n_in]
    o_ref = refs[n_in]
    side_out = refs[n_in + 1:-1]
    xn_ref = refs[-1]
    i = pl.program_id(0)
    first = pl.program_id(1) == 0

    def first_step():
        chunk = x_ref.shape[0] // NORM_ROW_CHUNKS
        for rc in range(NORM_ROW_CHUNKS):
            rows = slice(rc * chunk, (rc + 1) * chunk)
            x = x_ref[rows, :]
            xn = (x * _rms_scale(x) * g_ref[...]).astype(BF16)
            xn_ref[rows, :] = xn
            o_ref[rows, :] = x + _swiglu_half(xn, w1_ref[...], w3_ref[...], w2_ref[...])

    def later_step():
        o_ref[...] += _swiglu_half(xn_ref[...], w1_ref[...], w3_ref[...], w2_ref[...])

    def tile():
        pl.when(first)(first_step)
        pl.when(jnp.logical_not(first))(later_step)

    if has_head:
        head_ref = refs[5]
        pl.when((i == 0) & first)(lambda: pltpu.sync_copy(head_ref, o_ref))
        pl.when(i > 0)(tile)
    else:
        tile()

    for job in jobs:
        ins, side_in = side_in[:len(job.inputs)], side_in[len(job.inputs):]
        outs, side_out = side_out[:len(job.outputs)], side_out[len(job.outputs):]
        job.body(ins, outs)


def _ffn(x2d, gain, w1, w3, w2, *, tm, tf, jobs=(), head=None):
    m, d = x2d.shape
    d_ff = w1.shape[1]
    has_head = head is not None
    col = (lambda i, f: jnp.where(i == 0, 0, f)) if has_head else (lambda i, f: f)
    side_in = [io for j in jobs for io in j.inputs]
    side_out = [io for j in jobs for io in j.outputs]
    res = pl.pallas_call(
        functools.partial(_ffn_kernel, jobs=jobs, has_head=has_head),
        grid=(m // tm, d_ff // tf),
        in_specs=[
            pl.BlockSpec((tm, d), lambda i, f: (i, 0)),
            pl.BlockSpec((1, d), lambda i, f: (0, 0)),
            pl.BlockSpec((d, tf), lambda i, f: (0, col(i, f))),
            pl.BlockSpec((d, tf), lambda i, f: (0, col(i, f))),
            pl.BlockSpec((tf, d), lambda i, f: (col(i, f), 0)),
        ] + [pl.BlockSpec(memory_space=pl.ANY)] * has_head + [spec for _, spec in side_in],
        out_specs=[pl.BlockSpec((tm, d), lambda i, f: (i, 0))] + [spec for _, spec in side_out],
        out_shape=[jax.ShapeDtypeStruct((m, d), F32)] + [struct for struct, _ in side_out],
        scratch_shapes=[pltpu.VMEM((tm, d), BF16)],
        compiler_params=pltpu.CompilerParams(
            dimension_semantics=("arbitrary", "arbitrary"),
            vmem_limit_bytes=VMEM_LIMIT_BYTES),
        name="ffn",
    )(x2d, gain.reshape(1, d), w1, w3, w2, *([head] if has_head else []),
      *[arr for arr, _ in side_in])
    return res[0], res[1:]


def _cast_job(w, block, index_map):
    def body(ins, outs):
        outs[0][...] = ins[0][...].astype(BF16)
    spec = pl.BlockSpec(block, index_map)
    return _SideJob([(w, spec)], [(jax.ShapeDtypeStruct(w.shape, BF16), spec)], body)


def _row_slabs(nrows, n_f, n_steps):
    rows = BF16_SUBLANES
    while nrows % rows or nrows // rows > n_steps:
        rows += BF16_SUBLANES
    last = nrows // rows - 1
    return rows, lambda i, f: (jnp.minimum(i * n_f + f, last), 0)


def _w_in_jobs(w_in_t, n_f, n_steps):
    in_w, d = w_in_t.shape
    f0 = 3 * FOX_HEADS * HEAD_DIM
    main_w = in_w - FOX_HEADS
    rows = BF16_SUBLANES
    while main_w % rows or f0 % rows or main_w // rows > n_steps:
        rows += BF16_SUBLANES
    last = main_w // rows - 1
    step = lambda i, f: jnp.minimum(i * n_f + f, last)
    per_row_block = rows // FOX_HEADS

    def main_body(ins, outs):
        t = jnp.minimum(pl.program_id(0) * n_f + pl.program_id(1), last)
        a = ins[0][...]
        shifted = jnp.concatenate([a[FOX_HEADS:], ins[1][...]], axis=0)
        outs[0][...] = jnp.where(t * rows >= f0, shifted, a).astype(BF16)

    def forget_body(ins, outs):
        pad = jnp.zeros((LANES - FOX_HEADS, d), F32)
        outs[0][...] = jnp.concatenate([ins[0][...], pad], axis=0).astype(BF16)

    main = _SideJob(
        [(w_in_t, pl.BlockSpec((rows, d), lambda i, f: (step(i, f), 0))),
         (w_in_t, pl.BlockSpec((FOX_HEADS, d), lambda i, f: ((step(i, f) + 1) * per_row_block, 0)))],
        [(jax.ShapeDtypeStruct((main_w, d), BF16),
          pl.BlockSpec((rows, d), lambda i, f: (step(i, f), 0)))],
        main_body)
    forget = _SideJob(
        [(w_in_t, pl.BlockSpec((FOX_HEADS, d), lambda i, f: (f0 // FOX_HEADS, 0)))],
        [(jax.ShapeDtypeStruct((LANES, d), BF16), pl.BlockSpec((LANES, d), lambda i, f: (0, 0)))],
        forget_body)
    return main, forget


def _project_heads(xn_ref, w_ref, hg_ref, hflag_ref, p_ref, w_transposed):
    xn = xn_ref[...]
    for sb in range(p_ref.shape[1] // MXU_WIDTH):
        if w_transposed:
            y = _dot_nt(xn, w_ref[sb * MXU_WIDTH:(sb + 1) * MXU_WIDTH, :])
        else:
            y = _dot(xn, w_ref[:, sb * MXU_WIDTH:(sb + 1) * MXU_WIDTH])
        for hh in range(MXU_WIDTH // HEAD_DIM):
            cols = slice(sb * MXU_WIDTH + hh * HEAD_DIM, sb * MXU_WIDTH + (hh + 1) * HEAD_DIM)
            yh = y[:, hh * HEAD_DIM:(hh + 1) * HEAD_DIM]
            normed = yh * _rms_scale(yh) * hg_ref[:, cols]
            p_ref[:, cols] = jnp.where(hflag_ref[:, cols] > 0.0, normed, yh).astype(BF16)


def _proj_forget_kernel(x_ref, g_ref, w_ref, hg_ref, hflag_ref, wf_ref, bf_ref,
                        p_ref, c_ref, xn_ref, carry_ref, *, tiles_per_seq, cum_block):
    i = pl.program_id(0)
    j = pl.program_id(1)

    @pl.when(j == 0)
    def _():
        x = x_ref[...]
        xn_ref[...] = (x * _rms_scale(x) * g_ref[...]).astype(BF16)

    def forget_gates():
        @pl.when(i % tiles_per_seq == 0)
        def _():
            carry_ref[...] = jnp.zeros_like(carry_ref)

        z = _dot_nt(xn_ref[...], wf_ref[...]) + bf_ref[...]
        logf = jnp.minimum(z, 0.0) - jnp.log1p(jnp.exp(-jnp.abs(z)))
        r = lax.broadcasted_iota(jnp.int32, (cum_block, cum_block), 0)
        c = lax.broadcasted_iota(jnp.int32, (cum_block, cum_block), 1)
        tril = (c <= r).astype(BF16)
        local = []
        for blk in range(logf.shape[0] // cum_block):
            hi, mid, lo = _split3(logf[blk * cum_block:(blk + 1) * cum_block, :])
            local.append(_dot(tril, hi) + _dot(tril, mid) + _dot(tril, lo))
        carry = carry_ref[...]
        for blk, loc in enumerate(local):
            cum = loc + carry
            c_ref[blk * cum_block:(blk + 1) * cum_block, :] = cum
            carry = cum[cum_block - 1:cum_block, :]
        carry_ref[...] = carry

    heads = functools.partial(_project_heads, xn_ref, w_ref, hg_ref, hflag_ref, p_ref,
                              w_transposed=True)

    @pl.when(j == 1)
    def _():
        forget_gates()
        heads()

    @pl.when(j != 1)
    def _():
        heads()


def _norm_proj(x2d, gain, w, head_gain, head_flag, *, tm, tn, forget, seq_len):
    m, d = x2d.shape
    n = w.shape[0]
    in_specs = [
        pl.BlockSpec((tm, d), lambda i, j: (i, 0)),
        pl.BlockSpec((1, d), lambda i, j: (0, 0)),
        pl.BlockSpec((tn, d), lambda i, j: (j, 0)),
        pl.BlockSpec((1, tn), lambda i, j: (0, j)),
        pl.BlockSpec((1, tn), lambda i, j: (0, j)),
    ]
    args = [x2d, gain.reshape(1, d), w, head_gain.reshape(1, n), head_flag.reshape(1, n)]
    p_spec = pl.BlockSpec((tm, tn), lambda i, j: (i, j))
    p_shape = jax.ShapeDtypeStruct((m, n), BF16)
    params = pltpu.CompilerParams(dimension_semantics=("arbitrary", "arbitrary"),
                                  vmem_limit_bytes=VMEM_LIMIT_BYTES)
    wf, bf = forget
    assert n // tn >= 2
    in_specs += [pl.BlockSpec((LANES, d), lambda i, j: (0, 0)),
                 pl.BlockSpec((1, LANES), lambda i, j: (0, 0))]
    kern = functools.partial(_proj_forget_kernel, tiles_per_seq=seq_len // tm,
                             cum_block=min(tm, 256))
    return pl.pallas_call(
        kern, grid=(m // tm, n // tn), in_specs=in_specs,
        out_specs=[p_spec, pl.BlockSpec((tm, LANES), lambda i, j: (i, 0))],
        out_shape=[p_shape, jax.ShapeDtypeStruct((m, LANES), F32)],
        scratch_shapes=[pltpu.VMEM((tm, d), BF16), pltpu.VMEM((1, LANES), F32)],
        compiler_params=params, name="mix_proj")(*args, wf, bf)


FORGET_LANES = 6
FOX_HEADS_PER_STEP = 4
MOBA_ROWS_PER_STEP = 4


def _lane_set(lanes, offset):
    hit = lanes < 0
    for hh in range(FOX_HEADS):
        lo = FORGET_LANES * hh + offset
        hit = hit | ((lanes >= lo) & (lanes < lo + 3))
    return hit


def _fox_kernel(q_ref, k_ref, v_ref, c_ref, o_ref, qc_ref, kc_ref, kaug_ref, vaug_ref, s_ref,
                *, seq_len):
    step = pl.program_id(1)
    tq = ATTN_Q_BLOCK
    nq = seq_len // tq
    r = lax.broadcasted_iota(jnp.int32, (tq, tq), 0)
    c = lax.broadcasted_iota(jnp.int32, (tq, tq), 1)
    causal = c <= r

    @pl.when(step == 0)
    def _():
        hi, mid, lo = _split3(c_ref[...] * LOG2E)
        row = lax.broadcasted_iota(jnp.int32, (3 * LANES, 2 * LANES), 0)
        col = lax.broadcasted_iota(jnp.int32, (3 * LANES, 2 * LANES), 1)
        piece, head = row >> 7, row & (LANES - 1)
        slot = FORGET_LANES * head + piece
        route = jnp.where((head < FOX_HEADS) & (col == slot), 1.0,
                          jnp.where((head < FOX_HEADS) & (col == LANES + 3 + slot), -1.0, 0.0))
        routed = _dot(jnp.concatenate([hi, mid, lo], axis=1), route.astype(BF16))
        lane1 = lax.broadcasted_iota(jnp.int32, (1, LANES), 1)
        qc_ref[...] = (routed[:, :LANES] + jnp.where(_lane_set(lane1, 3), 1.0, 0.0)).astype(BF16)
        kc_ref[...] = (routed[:, LANES:] + jnp.where(_lane_set(lane1, 0), 1.0, 0.0)).astype(BF16)
        for hh in range(FOX_HEADS_PER_STEP):
            vaug_ref[hh, :, HEAD_DIM:] = jnp.ones((seq_len, HEAD_DIM), BF16)

    lanes = lax.broadcasted_iota(jnp.int32, (seq_len, LANES), 1)
    for hh in range(FOX_HEADS_PER_STEP):
        h = step * FOX_HEADS_PER_STEP + hh
        cols = slice(hh * HEAD_DIM, (hh + 1) * HEAD_DIM)
        mine = (lanes >= FORGET_LANES * h) & (lanes < FORGET_LANES * (h + 1))
        kaug_ref[hh, :, :HEAD_DIM] = k_ref[:, cols]
        kaug_ref[hh, :, HEAD_DIM:] = jnp.where(mine, kc_ref[...], jnp.zeros((), BF16))
        vaug_ref[hh, :, :HEAD_DIM] = v_ref[:, cols]

    work = [(hh, i) for hh in range(FOX_HEADS_PER_STEP) for i in reversed(range(nq))]

    def qk(pos):
        hh, i = work[pos]
        n = (i + 1) * tq
        rows = slice(i * tq, (i + 1) * tq)
        q_aug = jnp.concatenate([q_ref[rows, hh * HEAD_DIM:(hh + 1) * HEAD_DIM], qc_ref[rows, :]],
                                axis=1)
        s_ref[pos % 2, :, 0:n] = _dot_nt(q_aug, kaug_ref[hh, 0:n, :])

    qk(0)
    for pos, (hh, i) in enumerate(work):
        n = (i + 1) * tq
        if pos + 1 < len(work):
            qk(pos + 1)
        logits = s_ref[pos % 2, :, 0:n]
        own = jnp.where(causal, logits[:, i * tq:], NEG)
        if i > 0:
            logits = jnp.concatenate([logits[:, :i * tq], own], axis=1)
        else:
            logits = own
        o_ref[i * tq:(i + 1) * tq, hh * HEAD_DIM:(hh + 1) * HEAD_DIM] = _softmax_pv(
            logits, vaug_ref[hh, 0:n, :]).astype(BF16)


def _fox_attn(p, cum, *, batch, seq_len):
    m = p.shape[0]
    hd = HEAD_DIM
    per = FOX_HEADS_PER_STEP
    steps = FOX_HEADS // per
    return pl.pallas_call(
        functools.partial(_fox_kernel, seq_len=seq_len),
        grid=(batch, steps),
        in_specs=[
            pl.BlockSpec((seq_len, per * hd), lambda b, s: (b, s)),
            pl.BlockSpec((seq_len, per * hd), lambda b, s: (b, steps + s)),
            pl.BlockSpec((seq_len, per * hd), lambda b, s: (b, 2 * steps + s)),
            pl.BlockSpec((seq_len, LANES), lambda b, s: (b, 0)),
        ],
        out_specs=pl.BlockSpec((seq_len, per * hd), lambda b, s: (b, s)),
        out_shape=jax.ShapeDtypeStruct((m, FOX_HEADS * hd), BF16),
        scratch_shapes=[pltpu.VMEM((seq_len, LANES), BF16),
                        pltpu.VMEM((seq_len, LANES), BF16),
                        pltpu.VMEM((per, seq_len, hd + LANES), BF16),
                        pltpu.VMEM((per, seq_len, 2 * hd), BF16),
                        pltpu.VMEM((2, ATTN_Q_BLOCK, seq_len), F32)],
        compiler_params=pltpu.CompilerParams(
            dimension_semantics=("arbitrary", "arbitrary"),
            vmem_limit_bytes=VMEM_LIMIT_BYTES),
        name="fox_attn",
    )(p, p, p, cum)


def _t5_bias(dist, rel_ref, h):
    n = jnp.maximum(dist, 0)
    max_exact = REL_BUCKETS // 2
    nf = jnp.maximum(n, 1).astype(F32)
    large = max_exact + (jnp.log(nf / max_exact) / math.log(REL_MAX_DIST / max_exact)
                         * (REL_BUCKETS - max_exact)).astype(jnp.int32)
    large = jnp.minimum(large, REL_BUCKETS - 1)
    bucket = jnp.where(n < max_exact, n, large)
    bias = jnp.zeros(dist.shape, F32)
    for b in range(REL_BUCKETS):
        bias = jnp.where(bucket == b, rel_ref[b, h], bias)
    return bias


def _moba_kernel(rel_ref, q_ref, k_ref, v_ref, o_ref, town_ref, tprev_ref, kaug_ref, qaug_ref,
                 vaug_ref, kmean_ref, s_ref, *, seq_len):
    h = pl.program_id(0)
    blk = MOBA_BLOCK
    nb = seq_len // blk
    per = MOBA_ROWS_PER_STEP
    r = lax.broadcasted_iota(jnp.int32, (blk, blk), 0)
    c = lax.broadcasted_iota(jnp.int32, (blk, blk), 1)
    lane_grp = lax.broadcasted_iota(jnp.int32, (1, LANES), 1) >> 3

    @pl.when(pl.program_id(1) == 0)
    def _():
        town_ref[...] = jnp.where(c <= r, _t5_bias(r - c, rel_ref, h) * LOG2E, NEG)
        tprev_ref[...] = _t5_bias(r - c + blk, rel_ref, h) * LOG2E
        s = lax.broadcasted_iota(jnp.int32, (seq_len, LANES), 0)
        ln = lax.broadcasted_iota(jnp.int32, (seq_len, LANES), 1)
        first_key = (ln & 7) * blk
        onehot = (ln < 32) & (s >= first_key) & (s < first_key + blk)
        for bb in range(per):
            kaug_ref[bb, :, HEAD_DIM:] = jnp.where(onehot, 1.0, 0.0).astype(BF16)
            vaug_ref[bb, :, HEAD_DIM:] = jnp.ones((seq_len, HEAD_DIM), BF16)

    far = jnp.full((1, LANES), rel_ref[REL_BUCKETS - 1, h] * LOG2E, F32)
    far_hi, far_mid, far_lo = _split3(far)
    far_parts = jnp.where(lane_grp == 1, far_hi.astype(F32),
                          jnp.where(lane_grp == 2, far_mid.astype(F32),
                                    jnp.where(lane_grp == 3, far_lo.astype(F32), 0.0)))

    def far_bias(row0, nrows):
        rows = row0 + lax.broadcasted_iota(jnp.int32, (nrows, LANES), 0)
        lanes = lax.broadcasted_iota(jnp.int32, (nrows, LANES), 1)
        own = rows >> (blk.bit_length() - 1)
        return jnp.where((lanes & 7) <= own - 2, far_parts, 0.0), lanes, own

    late = min(seq_len, (MOBA_TOPK + 1) * blk)
    for bb in range(per):
        base = bb * seq_len
        kaug_ref[bb, :, :HEAD_DIM] = k_ref[base:base + seq_len, :]
        vaug_ref[bb, :, :HEAD_DIM] = v_ref[base:base + seq_len, :]
        qaug_ref[bb, :, :HEAD_DIM] = q_ref[base:base + seq_len, :]
        qaug_ref[bb, :late, HEAD_DIM:] = far_bias(0, late)[0].astype(BF16)
        if seq_len > late:
            kmean_ref[bb] = jnp.zeros(kmean_ref.shape[1:], F32)
            for j in range(nb):
                kmean_ref[bb, j:j + 1, :] = jnp.mean(
                    k_ref[base + j * blk:base + (j + 1) * blk, :].astype(F32), axis=0, keepdims=True)
            km_hi, km_mid, km_lo = _split3(kmean_ref[bb])
            ql = q_ref[base + late:base + seq_len, :]
            gate = _dot_nt(ql, km_hi) + _dot_nt(ql, km_mid) + _dot_nt(ql, km_lo)
            aug, lanes_l, own_l = far_bias(late, seq_len - late)
            valid = lanes_l < own_l
            g = jnp.where(valid, gate, NEG)
            sel = jnp.zeros(g.shape, jnp.bool_)
            lane_f = lanes_l.astype(F32)
            for _ in range(MOBA_TOPK):
                best = jnp.max(g, axis=-1, keepdims=True)
                first = jnp.min(jnp.where(g == best, lane_f, float(LANES)), axis=-1, keepdims=True)
                pick = lane_f == first
                sel = sel | pick
                g = jnp.where(pick, -jnp.inf, g)
            qaug_ref[bb, late:, HEAD_DIM:] = jnp.where(valid & jnp.logical_not(sel), NEG,
                                                       aug).astype(BF16)

    n_early = min(nb, MOBA_TOPK + 1)
    order = list(reversed(range(n_early))) + list(reversed(range(n_early, nb)))
    work = [(bb, i) for bb in range(per) for i in order]

    def qk(pos):
        bb, i = work[pos]
        n = (i + 1) * blk
        s_ref[pos % 2, :, 0:n] = _dot_nt(qaug_ref[bb, i * blk:(i + 1) * blk, :], kaug_ref[bb, 0:n, :])

    qk(0)
    for pos, (bb, i) in enumerate(work):
        n = (i + 1) * blk
        if pos + 1 < len(work):
            qk(pos + 1)
        s = s_ref[pos % 2, :, 0:n]
        pieces = [s[:, i * blk:] + town_ref[...]]
        if i >= 1:
            pieces.insert(0, s[:, (i - 1) * blk:i * blk] + tprev_ref[...])
        if i >= 2:
            pieces.insert(0, s[:, :(i - 1) * blk])
        logits = jnp.concatenate(pieces, axis=1) if len(pieces) > 1 else pieces[0]
        rows = slice(bb * seq_len + i * blk, bb * seq_len + (i + 1) * blk)
        o_ref[rows, :] = _softmax_pv(logits, vaug_ref[bb, 0:n, :]).astype(BF16)


def _moba_attn(p, rel_bias, *, batch, seq_len, col0):
    m = p.shape[0]
    hd = HEAD_DIM
    blk = MOBA_BLOCK
    per = MOBA_ROWS_PER_STEP
    assert seq_len % blk == 0 and seq_len // blk <= 8 and batch % per == 0
    rows = per * seq_len
    return pl.pallas_call(
        functools.partial(_moba_kernel, seq_len=seq_len),
        grid=(MOBA_HEADS, batch // per),
        in_specs=[
            pl.BlockSpec(memory_space=pltpu.SMEM),
            pl.BlockSpec((rows, hd), lambda h, b: (b, col0 + h)),
            pl.BlockSpec((rows, hd), lambda h, b: (b, col0 + MOBA_HEADS + h)),
            pl.BlockSpec((rows, hd), lambda h, b: (b, col0 + 2 * MOBA_HEADS + h)),
        ],
        out_specs=pl.BlockSpec((rows, hd), lambda h, b: (b, h)),
        out_shape=jax.ShapeDtypeStruct((m, MOBA_HEADS * hd), BF16),
        scratch_shapes=[
            pltpu.VMEM((blk, blk), F32),
            pltpu.VMEM((blk, blk), F32),
            pltpu.VMEM((per, seq_len, hd + LANES), BF16),
            pltpu.VMEM((per, seq_len, hd + LANES), BF16),
            pltpu.VMEM((per, seq_len, 2 * hd), BF16),
            pltpu.VMEM((per, LANES, hd), F32),
            pltpu.VMEM((2, blk, seq_len), F32),
        ],
        compiler_params=pltpu.CompilerParams(
            dimension_semantics=("arbitrary", "arbitrary"),
            vmem_limit_bytes=VMEM_LIMIT_BYTES),
        name="moba_attn",
    )(rel_bias, p, p, p)


def _mem_kernel(q_ref, mem_ref, g_ref, w_ref, kg_ref, o_ref, s_ref, *, seq_len, tq):
    heads = q_ref.shape[1] // HEAD_DIM
    width = heads * HEAD_DIM
    cols = [slice(h * HEAD_DIM, (h + 1) * HEAD_DIM) for h in range(heads)]
    x = mem_ref[...]
    kv = _dot((x * _rms_scale(x) * g_ref[...]).astype(BF16), w_ref[...])
    ones = jnp.ones((x.shape[0], HEAD_DIM), BF16)
    keys, v_ones = [], []
    for c in cols:
        yk = kv[:, c]
        keys.append((yk * _rms_scale(yk) * kg_ref[:, c]).astype(BF16))
        yv = kv[:, width + c.start:width + c.stop]
        v_ones.append(jnp.concatenate([yv.astype(BF16), ones], axis=1))
    work = [(h, i) for h in range(heads) for i in range(seq_len // tq)]

    def qk(pos):
        h, i = work[pos]
        s_ref[pos % 2] = _dot_nt(q_ref[i * tq:(i + 1) * tq, cols[h]], keys[h])

    qk(0)
    for pos, (h, i) in enumerate(work):
        if pos + 1 < len(work):
            qk(pos + 1)
        o_ref[i * tq:(i + 1) * tq, cols[h]] = _softmax_pv(s_ref[pos % 2], v_ones[h]).astype(BF16)


def _mem_attn(p, mem2d, mem_gain, w_kv, k_gain, *, batch, seq_len, n_mem, col0):
    m = p.shape[0]
    d = mem2d.shape[1]
    width = MEM_HEADS * HEAD_DIM
    tq = min(seq_len, 512)
    assert col0 % MEM_HEADS == 0
    return pl.pallas_call(
        functools.partial(_mem_kernel, seq_len=seq_len, tq=tq),
        grid=(batch,),
        in_specs=[
            pl.BlockSpec((seq_len, width), lambda b: (b, col0 // MEM_HEADS)),
            pl.BlockSpec((n_mem, d), lambda b: (b, 0)),
            pl.BlockSpec((1, d), lambda b: (0, 0)),
            pl.BlockSpec(w_kv.shape, lambda b: (0, 0), pipeline_mode=pl.Buffered(1)),
            pl.BlockSpec((1, width), lambda b: (0, 0)),
        ],
        out_specs=pl.BlockSpec((seq_len, width), lambda b: (b, 0)),
        out_shape=jax.ShapeDtypeStruct((m, width), BF16),
        scratch_shapes=[pltpu.VMEM((2, tq, n_mem), F32)],
        compiler_params=pltpu.CompilerParams(
            dimension_semantics=("arbitrary",), vmem_limit_bytes=VMEM_LIMIT_BYTES),
        name="mem_attn",
    )(p, mem2d, mem_gain.reshape(1, d), w_kv, k_gain.reshape(1, width))


def _out_proj_kernel(x_ref, of_ref, ob_ref, om_ref, w_ref, o_ref):
    wf = of_ref.shape[1]
    wb = ob_ref.shape[1]
    acc = _dot(of_ref[...], w_ref[0:wf, :])
    acc += _dot(ob_ref[...], w_ref[wf:wf + wb, :])
    acc += _dot(om_ref[...], w_ref[wf + wb:, :])
    o_ref[...] = x_ref[...] + acc


def _out_proj(x2d, o_fox, o_moba, o_mem, w_out, *, tm):
    m, d = x2d.shape
    return pl.pallas_call(
        _out_proj_kernel,
        grid=(m // tm,),
        in_specs=[
            pl.BlockSpec((tm, d), lambda i: (i, 0)),
            pl.BlockSpec((tm, o_fox.shape[1]), lambda i: (i, 0)),
            pl.BlockSpec((tm, o_moba.shape[1]), lambda i: (i, 0)),
            pl.BlockSpec((tm, o_mem.shape[1]), lambda i: (i, 0)),
            pl.BlockSpec(w_out.shape, lambda i: (0, 0), pipeline_mode=pl.Buffered(1)),
        ],
        out_specs=pl.BlockSpec((tm, d), lambda i: (i, 0)),
        out_shape=jax.ShapeDtypeStruct((m, d), F32),
        compiler_params=pltpu.CompilerParams(
            dimension_semantics=("arbitrary",), vmem_limit_bytes=VMEM_LIMIT_BYTES),
        name="out_proj",
    )(x2d, o_fox, o_moba, o_mem, w_out)


def _tile(total, want):
    t = min(total, want)
    assert total % t == 0, (total, want)
    return t


def kernel(x, mem, ffn1_norm, ffn1_w1, ffn1_w3, ffn1_w2, mix_norm, mem_norm, w_in, b_forget,
           w_mem_kv, fox_q_gain, fox_k_gain, moba_q_gain, moba_k_gain, mem_q_gain, mem_k_gain,
           w_out, ffn2_norm, ffn2_w1, ffn2_w3, ffn2_w2, rel_bias):
    batch, seq_len, d = x.shape
    n_mem = mem.shape[1]
    depth = w_in.shape[0]
    fox_w = FOX_HEADS * HEAD_DIM
    moba_w = MOBA_HEADS * HEAD_DIM
    mem_w = MEM_HEADS * HEAD_DIM
    m = batch * seq_len
    tm = _tile(seq_len, TOKEN_TILE)
    ones = jnp.ones((HEAD_DIM,), F32)
    q_scale = HEAD_DIM ** -0.5 * LOG2E

    x2d = x.reshape(m, d)
    mem2d = mem.reshape(batch * n_mem, d)
    for l in range(depth):
        tf = _tile(ffn1_w1.shape[2], FF_TILE)
        n_i, n_f = m // tm, ffn1_w1.shape[2] // tf
        d_ff2 = ffn2_w1.shape[2]
        rows, ff_cols = d // n_i, d_ff2 // n_f
        n_steps = n_i * n_f
        jobs = (
            _cast_job(ffn2_w1[l], (rows, ff_cols), lambda i, f: (i, f)),
            _cast_job(ffn2_w3[l], (rows, ff_cols), lambda i, f: (i, f)),
            _cast_job(ffn2_w2[l], (ff_cols, rows), lambda i, f: (f, i)),
            _cast_job(w_out[l], (_row_slabs(w_out.shape[1], n_f, n_steps)[0], d),
                      _row_slabs(w_out.shape[1], n_f, n_steps)[1]),
            _cast_job(w_mem_kv[l], (_row_slabs(d, n_f, n_steps)[0], w_mem_kv.shape[2]),
                      _row_slabs(d, n_f, n_steps)[1]),
        ) + _w_in_jobs(jnp.swapaxes(w_in[l], 0, 1), n_f, n_steps)
        head, w1_1, w3_1, w2_1 = _ffn_head(x2d, ffn1_norm[l], ffn1_w1[l], ffn1_w3[l], ffn1_w2[l],
                                           tm=tm, tf=_tile(ffn1_w1.shape[2], HEAD_FF_TILE))
        x2d, (w1_2, w3_2, w2_2, w_out_bf, w_mem_bf, w_main, w_forget) = _ffn(
            x2d, ffn1_norm[l], w1_1, w3_1, w2_1, tm=tm, tf=tf, jobs=jobs, head=head)

        b_pad = jnp.pad(b_forget[l].astype(F32), (0, LANES - FOX_HEADS)).reshape(1, LANES)
        head_gain = jnp.concatenate(
            [jnp.tile(fox_q_gain[l] * q_scale, FOX_HEADS), jnp.tile(fox_k_gain[l], FOX_HEADS),
             jnp.tile(ones, FOX_HEADS), jnp.tile(moba_q_gain[l] * q_scale, MOBA_HEADS),
             jnp.tile(moba_k_gain[l], MOBA_HEADS), jnp.tile(ones, MOBA_HEADS),
             jnp.tile(mem_q_gain[l] * q_scale, MEM_HEADS)]).astype(F32)
        head_flag = jnp.concatenate(
            [jnp.ones((2 * fox_w,), F32), jnp.zeros((fox_w,), F32), jnp.ones((2 * moba_w,), F32),
             jnp.zeros((moba_w,), F32), jnp.ones((mem_w,), F32)])
        proj, cum = _norm_proj(x2d, mix_norm[l], w_main, head_gain, head_flag, tm=tm,
                               tn=_tile(w_main.shape[0], PROJ_COL_TILE), forget=(w_forget, b_pad),
                               seq_len=seq_len)

        o_fox = _fox_attn(proj, cum, batch=batch, seq_len=seq_len)
        o_moba = _moba_attn(proj, rel_bias.astype(F32), batch=batch, seq_len=seq_len,
                            col0=3 * FOX_HEADS)
        o_mem = _mem_attn(proj, mem2d, mem_norm[l], w_mem_bf,
                          jnp.tile(mem_k_gain[l], MEM_HEADS).astype(F32), batch=batch,
                          seq_len=seq_len, n_mem=n_mem, col0=3 * FOX_HEADS + 3 * MOBA_HEADS)
        x2d = _out_proj(x2d, o_fox, o_moba, o_mem, w_out_bf, tm=tm)

        x2d, _ = _ffn(x2d, ffn2_norm[l], w1_2, w3_2, w2_2, tm=tm, tf=_tile(d_ff2, FF_TILE))
    return x2d.reshape(batch, seq_len, d)
```
